```python
import jax, jax.numpy as jnp
from jax import lax
import numpy as np

D_MODEL = 1024
BATCH = 8
SEQ = 4096
DEPTH = 2

HEAD_DIM = 64
N_HEADS_FOX = 6
N_HEADS_SB = 6
N_HEADS_DSA = 4
N_IDX_HEADS = 4
IDX_DIM = 64
TOPK_MAX = 256
D_FF = 2816
ROPE_THETA = 500000.0
ROT_DIM = HEAD_DIM // 4
Q_BLOCK = 128
EPS = 1e-6
N_BRANCHES = 3

W_FOX = N_HEADS_FOX * HEAD_DIM
W_SB = N_HEADS_SB * HEAD_DIM
W_DSA = N_HEADS_DSA * HEAD_DIM
IN_SPLITS = (W_FOX, W_FOX, W_FOX, N_HEADS_FOX,
             W_SB, W_SB, W_SB,
             W_DSA, W_DSA, W_DSA,
             N_IDX_HEADS * IDX_DIM, IDX_DIM, N_IDX_HEADS,
             D_MODEL, D_MODEL, D_MODEL)
N_IN = sum(IN_SPLITS)

kernel_name = "hybrid_fox_stickbreak_dsa_macaron"


def _rmsnorm(x, g):
    x32 = x.astype(jnp.float32)
    y = x32 * lax.rsqrt(jnp.mean(x32 * x32, axis=-1, keepdims=True) + EPS)
    return (y * g.astype(jnp.float32)).astype(x.dtype)


def _rope_partial(x, positions):
    half = ROT_DIM // 2
    inv_freq = jnp.power(ROPE_THETA, -jnp.arange(half, dtype=jnp.float32) * 2.0 / ROT_DIM)
    ang = positions.astype(jnp.float32)[..., None] * inv_freq
    cos = jnp.cos(ang)[:, :, None, :]
    sin = jnp.sin(ang)[:, :, None, :]
    x32 = x.astype(jnp.float32)
    x1 = x32[..., :half]
    x2 = x32[..., half:ROT_DIM]
    out = jnp.concatenate([x1 * cos - x2 * sin, x2 * cos + x1 * sin, x32[..., ROT_DIM:]], axis=-1)
    return out.astype(x.dtype)


def _swiglu(h, w_gate, w_up, w_down):
    a = jnp.einsum('bsd,df->bsf', h, w_gate)
    u = jnp.einsum('bsd,df->bsf', h, w_up)
    return jnp.einsum('bsf,fd->bsd', jax.nn.silu(a) * u, w_down)


def _to_blocks(a):
    b, s = a.shape[:2]
    a = a.reshape((b, s // Q_BLOCK, Q_BLOCK) + a.shape[2:])
    return jnp.moveaxis(a, 1, 0)


def _from_blocks(a):
    a = jnp.moveaxis(a, 0, 1)
    return a.reshape((a.shape[0], a.shape[1] * a.shape[2]) + a.shape[3:])


def _query_pos_blocks(s):
    return jnp.arange(s).reshape(s // Q_BLOCK, Q_BLOCK)


def _forgetting_attention(q, k, v, log_f):
    s_len, dh = q.shape[1], q.shape[-1]
    scale = dh ** -0.5
    cum = jnp.cumsum(log_f, axis=1)
    cum_k = jnp.transpose(cum, (0, 2, 1))
    kpos = jnp.arange(s_len)

    def block(args):
        qb, cq, tq = args
        logits = jnp.einsum('bqhd,bkhd->bhqk', qb, k, preferred_element_type=jnp.float32) * scale
        logits = logits + jnp.transpose(cq, (0, 2, 1))[..., :, None] - cum_k[..., None, :]
        causal = kpos[None, :] <= tq[:, None]
        p = jax.nn.softmax(jnp.where(causal, logits, -jnp.inf), axis=-1)
        return jnp.einsum('bhqk,bkhd->bqhd', p.astype(v.dtype), v)

    out = lax.map(block, (_to_blocks(q), _to_blocks(cum), _query_pos_blocks(s_len)))
    return _from_blocks(out)


def _stick_breaking_attention(q, k, v):
    s_len, dh = q.shape[1], q.shape[-1]
    scale = dh ** -0.5
    kpos = jnp.arange(s_len)

    def block(args):
        qb, tq = args
        z = jnp.einsum('bqhd,bkhd->bhqk', qb, k, preferred_element_type=jnp.float32) * scale
        strict = kpos[None, :] < tq[:, None]
        log_one_minus = jnp.where(strict, jax.nn.log_sigmoid(-z), 0.0)
        suffix = lax.cumsum(log_one_minus, axis=3, reverse=True) - log_one_minus
        log_a = jax.nn.log_sigmoid(z) + suffix
        a = jnp.where(strict, jnp.exp(log_a), 0.0)
        return jnp.einsum('bhqk,bkhd->bqhd', a.astype(v.dtype), v)

    out = lax.map(block, (_to_blocks(q), _query_pos_blocks(s_len)))
    return _from_blocks(out)


def _dsa_attention(q, k, v, q_idx, k_idx, w_idx, topk):
    s_len, dh = q.shape[1], q.shape[-1]
    scale = dh ** -0.5
    idx_scale = IDX_DIM ** -0.5
    w_scale = N_IDX_HEADS ** -0.5
    kpos = jnp.arange(s_len)
    gather = jax.vmap(lambda tb, ib: tb[ib])

    def block(args):
        qb, qib, wib, tq = args
        dots = jnp.einsum('bqhd,bkd->bqhk', qib, k_idx, preferred_element_type=jnp.float32) * idx_scale
        score = jnp.einsum('bqh,bqhk->bqk', wib.astype(jnp.float32) * w_scale, jax.nn.relu(dots))
        causal = kpos[None, :] <= tq[:, None]
        score = jnp.where(causal[None], score, -jnp.inf)
        _, sel = lax.top_k(score, topk)
        valid = sel <= tq[None, :, None]
        k_sel = gather(k, sel)
        v_sel = gather(v, sel)
        logits = jnp.einsum('bqhd,bqkhd->bhqk', qb, k_sel, preferred_element_type=jnp.float32) * scale
        p = jax.nn.softmax(jnp.where(valid[:, None], logits, -jnp.inf), axis=-1)
        return jnp.einsum('bhqk,bqkhd->bqhd', p.astype(v.dtype), v_sel)

    out = lax.map(block, (_to_blocks(q), _to_blocks(q_idx), _to_blocks(w_idx), _query_pos_blocks(s_len)))
    return _from_blocks(out)


def _hybrid_mixer(h, positions, w_in, b_forget, b_gates, qn_f, kn_f, qn_s, kn_s, qn_c, kn_c,
                  w_br_f, w_br_s, w_br_c, w_out, topk):
    b, s, _ = h.shape
    offs = np.cumsum(IN_SPLITS)[:-1].tolist()
    z = jnp.einsum('bsd,dn->bsn', h, w_in)
    (qf, kf, vf, ff, qs, ks, vs, qc, kc, vc, qi, ki, wi, ga, gb, gc) = jnp.split(z, offs, axis=-1)

    def heads(t):
        return t.reshape(b, s, -1, HEAD_DIM)

    log_f = jax.nn.log_sigmoid(ff.astype(jnp.float32) + b_forget.astype(jnp.float32))
    o_f = _forgetting_attention(_rmsnorm(heads(qf), qn_f), _rmsnorm(heads(kf), kn_f), heads(vf), log_f)
    o_s = _stick_breaking_attention(_rmsnorm(heads(qs), qn_s), _rmsnorm(heads(ks), kn_s), heads(vs))
    q_c = _rope_partial(_rmsnorm(heads(qc), qn_c), positions)
    k_c = _rope_partial(_rmsnorm(heads(kc), kn_c), positions)
    q_i = _rope_partial(qi.reshape(b, s, N_IDX_HEADS, IDX_DIM), positions)
    k_i = _rope_partial(ki[:, :, None, :], positions)[:, :, 0]
    o_c = _dsa_attention(q_c, k_c, heads(vc), q_i, k_i, wi, topk)

    br_f = jnp.einsum('bsm,md->bsd', o_f.reshape(b, s, W_FOX), w_br_f)
    br_s = jnp.einsum('bsm,md->bsd', o_s.reshape(b, s, W_SB), w_br_s)
    br_c = jnp.einsum('bsm,md->bsd', o_c.reshape(b, s, W_DSA), w_br_c)
    merged = (jax.nn.sigmoid(ga + b_gates[0]) * br_f
              + jax.nn.sigmoid(gb + b_gates[1]) * br_s
              + jax.nn.sigmoid(gc + b_gates[2]) * br_c)
    return jnp.einsum('bsd,de->bse', merged, w_out)


def setup_inputs(seed: int = 0) -> dict:
    key = jax.random.key(seed)
    ks = jax.random.split(key, 24)
    f32 = jnp.float32

    def w(k, shape, fan_in):
        return jax.random.normal(k, shape, f32) * (fan_in ** -0.5)

    def gain(k, shape):
        return 1.0 + 0.02 * jax.random.normal(k, shape, f32)

    return {
        "x": jax.random.normal(ks[0], (BATCH, SEQ, D_MODEL), f32),
        "positions": jnp.broadcast_to(jnp.arange(SEQ, dtype=jnp.int32)[None, :], (BATCH, SEQ)),
        "ffn1_norm": gain(ks[1], (DEPTH, D_MODEL)),
        "ffn1_w_gate": w(ks[2], (DEPTH, D_MODEL, D_FF), D_MODEL),
        "ffn1_w_up": w(ks[3], (DEPTH, D_MODEL, D_FF), D_MODEL),
        "ffn1_w_down": w(ks[4], (DEPTH, D_FF, D_MODEL), D_FF),
        "mix_norm": gain(ks[5], (DEPTH, D_MODEL)),
        "w_in": w(ks[6], (DEPTH, D_MODEL, N_IN), D_MODEL),
        "b_forget": 1.0 + 3.0 * jax.random.uniform(ks[7], (DEPTH, N_HEADS_FOX), f32),
        "b_gates": 0.02 * jax.random.normal(ks[8], (DEPTH, N_BRANCHES, D_MODEL), f32),
        "q_norm_fox": gain(ks[9], (DEPTH, HEAD_DIM)),
        "k_norm_fox": gain(ks[10], (DEPTH, HEAD_DIM)),
        "q_norm_sb": gain(ks[11], (DEPTH, HEAD_DIM)),
        "k_norm_sb": gain(ks[12], (DEPTH, HEAD_DIM)),
        "q_norm_dsa": gain(ks[13], (DEPTH, HEAD_DIM)),
        "k_norm_dsa": gain(ks[14], (DEPTH, HEAD_DIM)),
        "w_branch_fox": w(ks[15], (DEPTH, W_FOX, D_MODEL), W_FOX),
        "w_branch_sb": w(ks[16], (DEPTH, W_SB, D_MODEL), W_SB),
        "w_branch_dsa": w(ks[17], (DEPTH, W_DSA, D_MODEL), W_DSA),
        "w_out": w(ks[18], (DEPTH, D_MODEL, D_MODEL), D_MODEL),
        "ffn2_norm": gain(ks[19], (DEPTH, D_MODEL)),
        "ffn2_w_gate": w(ks[20], (DEPTH, D_MODEL, D_FF), D_MODEL),
        "ffn2_w_up": w(ks[21], (DEPTH, D_MODEL, D_FF), D_MODEL),
        "ffn2_w_down": w(ks[22], (DEPTH, D_FF, D_MODEL), D_FF),
    }


def reference(x, positions, ffn1_norm, ffn1_w_gate, ffn1_w_up, ffn1_w_down, mix_norm, w_in,
              b_forget, b_gates, q_norm_fox, k_norm_fox, q_norm_sb, k_norm_sb, q_norm_dsa, k_norm_dsa,
              w_branch_fox, w_branch_sb, w_branch_dsa, w_out, ffn2_norm, ffn2_w_gate, ffn2_w_up,
              ffn2_w_down):
    topk = min(TOPK_MAX, x.shape[1] // 4)
    for l in range(DEPTH):
        x = x + 0.5 * _swiglu(_rmsnorm(x, ffn1_norm[l]), ffn1_w_gate[l], ffn1_w_up[l], ffn1_w_down[l])
        x = x + _hybrid_mixer(_rmsnorm(x, mix_norm[l]), positions, w_in[l], b_forget[l], b_gates[l],
                              q_norm_fox[l], k_norm_fox[l], q_norm_sb[l], k_norm_sb[l],
                              q_norm_dsa[l], k_norm_dsa[l], w_branch_fox[l], w_branch_sb[l],
                              w_branch_dsa[l], w_out[l], topk)
        x = x + 0.5 * _swiglu(_rmsnorm(x, ffn2_norm[l]), ffn2_w_gate[l], ffn2_w_up[l], ffn2_w_down[l])
    return x
```

```python
import functools

import jax
import jax.numpy as jnp
import numpy as np
from jax import lax
from jax.experimental import pallas as pl
from jax.experimental.pallas import tpu as pltpu

F32 = jnp.float32
BF16 = jnp.bfloat16

HEAD_DIM = 64
N_HEADS_FOX = 6
N_HEADS_SB = 6
N_HEADS_DSA = 4
N_IDX_HEADS = 4
IDX_DIM = 64
TOPK_MAX = 256
ROPE_THETA = 500000.0
ROT_DIM = HEAD_DIM // 4
EPS = 1e-6

LANES = 128
W_FOX = N_HEADS_FOX * HEAD_DIM
W_SB = N_HEADS_SB * HEAD_DIM
W_DSA = N_HEADS_DSA * HEAD_DIM
W_IDX = N_IDX_HEADS * IDX_DIM

C_FQ, C_FK, C_FV = 0, W_FOX, 2 * W_FOX
C_SQ, C_SK, C_SV = 3 * W_FOX, 3 * W_FOX + W_SB, 3 * W_FOX + 2 * W_SB
C_CQ = 3 * W_FOX + 3 * W_SB
C_CK, C_CV = C_CQ + W_DSA, C_CQ + 2 * W_DSA
C_IQ = C_CQ + 3 * W_DSA
C_MISC = C_IQ + W_IDX
N_PROJ = C_MISC + LANES
L_FORGET = IDX_DIM
L_WIDX = IDX_DIM + N_HEADS_FOX

TQ = 256
TK = 256
TM_PROJ = 512
TM_FFN = 512
TM_MERGE = 512
NEG_BIG = -1e30
VMEM_LIMIT = 56 * 1024 * 1024

_NT = (((1,), (1,)), ((), ()))


def _dot(a, b):
    return jnp.dot(a, b, preferred_element_type=F32)


def _dot_nt(a, b):
    return lax.dot_general(a, b, _NT, preferred_element_type=F32)


def _split2(x):
    hi = x.astype(BF16)
    lo = (x - hi.astype(F32)).astype(BF16)
    return hi, lo


def _split3(x):
    hi = x.astype(BF16)
    r = x - hi.astype(F32)
    mid = r.astype(BF16)
    lo = (r - mid.astype(F32)).astype(BF16)
    return hi, mid, lo


def _rms_rows(x, g):
    ms = jnp.mean(x * x, axis=-1, keepdims=True)
    return x * lax.rsqrt(ms + EPS) * g


def _softplus(z):
    return jnp.maximum(z, 0.0) + jnp.log(1.0 + jnp.exp(-jnp.abs(z)))


def _ffn_body(x_ref, g_ref, wg_ref, wu_ref, wd_ref, o_ref, h_scr, acc_scr):
    f = pl.program_id(1)

    @pl.when(f == 0)
    def _():
        h_scr[...] = _rms_rows(x_ref[...], g_ref[...]).astype(BF16)
        acc_scr[...] = jnp.zeros_like(acc_scr)

    h = h_scr[...]
    a = _dot(h, wg_ref[...])
    u = _dot(h, wu_ref[...])
    p = (a * (1.0 / (1.0 + jnp.exp(-a))) * u).astype(BF16)
    acc_scr[...] += _dot(p, wd_ref[...])

    @pl.when(f == pl.num_programs(1) - 1)
    def _():
        o_ref[...] = x_ref[...] + 0.5 * acc_scr[...]


def _ffn_tile_f(d_ff):
    best = LANES
    for t in range(LANES, d_ff + 1, LANES):
        if d_ff % t == 0 and t <= 1536:
            best = t
    return best


def _ffn(x2, g, wg, wu, wd):
    m, d = x2.shape
    d_ff = wg.shape[1]
    tm = min(TM_FFN, m)
    tf = _ffn_tile_f(d_ff)
    return pl.pallas_call(
        _ffn_body,
        grid=(m // tm, d_ff // tf),
        in_specs=[
            pl.BlockSpec((tm, d), lambda i, f: (i, 0)),
            pl.BlockSpec((1, d), lambda i, f: (0, 0)),
            pl.BlockSpec((d, tf), lambda i, f: (0, f)),
            pl.BlockSpec((d, tf), lambda i, f: (0, f)),
            pl.BlockSpec((tf, d), lambda i, f: (f, 0)),
        ],
        out_specs=pl.BlockSpec((tm, d), lambda i, f: (i, 0)),
        out_shape=jax.ShapeDtypeStruct((m, d), F32),
        scratch_shapes=[pltpu.VMEM((tm, d), BF16), pltpu.VMEM((tm, d), F32)],
        compiler_params=pltpu.CompilerParams(
            dimension_semantics=("parallel", "arbitrary"), vmem_limit_bytes=VMEM_LIMIT),
        name="ffn",
    )(x2, g, wg, wu, wd)


def _proj_body(x_ref, pos_ref, g_ref, w_ref, gains_ref, bd_ref, freq_ref, fbias_ref, tri_ref,
               zb_ref, misc_ref, cumt_ref, carry_scr):
    t = pl.program_id(1)
    tm = x_ref.shape[1]
    h = _rms_rows(x_ref[0], g_ref[...]).astype(BF16)

    lane = lax.broadcasted_iota(jnp.int32, (1, LANES), 1)
    d_in_head = lane % HEAD_DIM
    ang = pos_ref[0].astype(F32) * freq_ref[...]
    cos = jnp.cos(ang)
    sin_signed = jnp.where(d_in_head < ROT_DIM // 2, -jnp.sin(ang), jnp.sin(ang))
    first_half = d_in_head < ROT_DIM // 2

    def rope(x):
        partner = jnp.where(first_half, pltpu.roll(x, LANES - ROT_DIM // 2, 1), pltpu.roll(x, ROT_DIM // 2, 1))
        return x * cos + partner * sin_signed

    def head_norm(z, c0, width):
        hi, lo = _split2(z * z)
        bd = bd_ref[:width, :width]
        ss = _dot(hi, bd) + _dot(lo, bd)
        return z * lax.rsqrt(ss * (1.0 / HEAD_DIM) + EPS) * gains_ref[:, c0:c0 + width]

    def group(c0, width, norm, rot):
        z = _dot(h, w_ref[:, c0:c0 + width])
        if norm:
            z = head_norm(z, c0, width)
        else:
            z = z * gains_ref[:, c0:c0 + width]
        if rot:
            z = jnp.concatenate([rope(z[:, c:c + LANES]) for c in range(0, width, LANES)], axis=1)
        zb_ref[0, :, c0:c0 + width] = z.astype(BF16)

    group(C_FQ, W_FOX, True, False)
    group(C_FK, W_FOX, True, False)
    group(C_FV, W_FOX, False, False)
    group(C_SQ, W_SB, True, False)
    group(C_SK, W_SB, True, False)
    group(C_SV, W_SB, False, False)
    group(C_CQ, W_DSA, True, True)
    group(C_CK, W_DSA, True, True)
    group(C_CV, W_DSA, False, False)
    group(C_IQ, W_IDX, False, True)

    zm = _dot(h, w_ref[:, C_MISC:C_MISC + LANES])
    ki = rope(zm)
    zb_ref[0, :, C_MISC:C_MISC + LANES] = jnp.where(lane < IDX_DIM, ki, pltpu.roll(ki, IDX_DIM, 1)).astype(BF16)

    @pl.when(t == 0)
    def _():
        carry_scr[...] = jnp.zeros_like(carry_scr)

    logf = -_softplus(-(zm + fbias_ref[...]))
    hi, mid, lo = _split3(logf)
    tri = tri_ref[...]
    cum = _dot(tri, hi) + _dot(tri, mid) + _dot(tri, lo) + carry_scr[0:1, :]
    carry_scr[...] = jnp.broadcast_to(cum[tm - 1:tm, :], carry_scr.shape)

    is_forget = (lane >= L_FORGET) & (lane < L_FORGET + N_HEADS_FOX)
    is_widx = (lane >= L_WIDX) & (lane < L_WIDX + N_IDX_HEADS)
    misc = jnp.where(is_forget, cum, jnp.where(is_widx, zm * (N_IDX_HEADS ** -0.5), 0.0))
    misc_ref[0] = misc
    misc_t = misc.T
    for c in range(tm // TK):
        cumt_ref[0, c] = misc_t[L_FORGET:L_FORGET + 8, c * TK:(c + 1) * TK]


def _proj(x, pos3, g, w, gains, bd, freq, fbias, tri):
    b, s, d = x.shape
    tm = min(TM_PROJ, s)
    const = lambda shape: pl.BlockSpec(shape, lambda bi, ti: (0,) * len(shape))
    return pl.pallas_call(
        _proj_body,
        grid=(b, s // tm),
        in_specs=[
            pl.BlockSpec((1, tm, d), lambda bi, ti: (bi, ti, 0)),
            pl.BlockSpec((1, tm, 1), lambda bi, ti: (bi, ti, 0)),
            const((1, d)),
            const((d, N_PROJ)),
            const((1, N_PROJ)),
            const((W_FOX, W_FOX)),
            const((1, LANES)),
            const((1, LANES)),
            const((tm, tm)),
        ],
        out_specs=[
            pl.BlockSpec((1, tm, N_PROJ), lambda bi, ti: (bi, ti, 0)),
            pl.BlockSpec((1, tm, LANES), lambda bi, ti: (bi, ti, 0)),
            pl.BlockSpec((1, tm // TK, 8, TK), lambda bi, ti: (bi, ti, 0, 0)),
        ],
        out_shape=[
            jax.ShapeDtypeStruct((b, s, N_PROJ), BF16),
            jax.ShapeDtypeStruct((b, s, LANES), F32),
            jax.ShapeDtypeStruct((b, s // TK, 8, TK), F32),
        ],
        scratch_shapes=[pltpu.VMEM((8, LANES), F32)],
        compiler_params=pltpu.CompilerParams(
            dimension_semantics=("parallel", "arbitrary"), vmem_limit_bytes=VMEM_LIMIT),
        name="proj",
    )(x, pos3, g, w, gains, bd, freq, fbias, tri)


def _head_masks():
    lane = lax.broadcasted_iota(jnp.int32, (1, LANES), 1)
    return lane < HEAD_DIM, lane >= HEAD_DIM


def _causal(tq, tk, strict):
    row = lax.broadcasted_iota(jnp.int32, (tq, tk), 0)
    col = lax.broadcasted_iota(jnp.int32, (tq, tk), 1)
    return col < row if strict else col <= row


def _softmax_step(s, vb, m_scr, l_scr, acc_scr, hd):
    m_old = m_scr[hd]
    m_new = jnp.maximum(m_old, jnp.max(s, axis=1, keepdims=True))
    alpha = jnp.exp(m_old - m_new)
    p = jnp.exp(s - m_new)
    l_scr[hd] = alpha * l_scr[hd] + jnp.sum(p, axis=1, keepdims=True)
    acc_scr[hd] = alpha * acc_scr[hd] + _dot(p.astype(BF16), vb)
    m_scr[hd] = m_new


def _fox_body(q_ref, k_ref, v_ref, misc_ref, cumt_ref, o_ref, m_scr, l_scr, acc_scr):
    p = pl.program_id(1)
    i = pl.program_id(2)
    tq = q_ref.shape[1]
    lo_half, hi_half = _head_masks()
    lane = lax.broadcasted_iota(jnp.int32, (1, LANES), 1)
    sub = lax.broadcasted_iota(jnp.int32, (8, 1), 0)

    q = q_ref[0]
    q_heads = (jnp.where(lo_half, q, jnp.zeros_like(q)), jnp.where(hi_half, q, jnp.zeros_like(q)))
    misc = misc_ref[0]
    cq = tuple(jnp.sum(jnp.where(lane == L_FORGET + 2 * p + hd, misc, 0.0), axis=1, keepdims=True)
               for hd in range(2))

    m_scr[...] = jnp.full_like(m_scr, -jnp.inf)
    l_scr[...] = jnp.zeros_like(l_scr)
    acc_scr[...] = jnp.zeros_like(acc_scr)

    def step(j, masked):
        start = pl.multiple_of(j * TK, TK)
        kb = k_ref[0, pl.ds(start, TK), :]
        vb = v_ref[0, pl.ds(start, TK), :]
        ct = cumt_ref[0, j]
        for hd in range(2):
            ck = jnp.sum(jnp.where(sub == 2 * p + hd, ct, 0.0), axis=0, keepdims=True)
            s = _dot_nt(q_heads[hd], kb) + (cq[hd] - ck)
            if masked:
                s = jnp.where(_causal(tq, TK, False), s, -jnp.inf)
            _softmax_step(s, vb, m_scr, l_scr, acc_scr, hd)

    def body(j, c):
        step(j, False)
        return c

    lax.fori_loop(0, i, body, 0)
    step(i, True)
    o_ref[0] = jnp.where(lo_half, acc_scr[0] / l_scr[0], acc_scr[1] / l_scr[1]).astype(BF16)


def _fox(zb, misc, cumt):
    b, s, _ = zb.shape
    nq = s // TQ
    n_pairs = W_FOX // LANES
    return pl.pallas_call(
        _fox_body,
        grid=(b, n_pairs, nq),
        in_specs=[
            pl.BlockSpec((1, TQ, LANES), lambda bi, p, i: (bi, i, C_FQ // LANES + p)),
            pl.BlockSpec((1, s, LANES), lambda bi, p, i: (bi, 0, C_FK // LANES + p)),
            pl.BlockSpec((1, s, LANES), lambda bi, p, i: (bi, 0, C_FV // LANES + p)),
            pl.BlockSpec((1, TQ, LANES), lambda bi, p, i: (bi, i, 0)),
            pl.BlockSpec((1, s // TK, 8, TK), lambda bi, p, i: (bi, 0, 0, 0)),
        ],
        out_specs=pl.BlockSpec((1, TQ, LANES), lambda bi, p, i: (bi, i, p)),
        out_shape=jax.ShapeDtypeStruct((b, s, W_FOX), BF16),
        scratch_shapes=[pltpu.VMEM((2, TQ, 1), F32), pltpu.VMEM((2, TQ, 1), F32),
                        pltpu.VMEM((2, TQ, LANES), F32)],
        compiler_params=pltpu.CompilerParams(
            dimension_semantics=("parallel", "parallel", "arbitrary"), vmem_limit_bytes=VMEM_LIMIT),
        name="fox",
    )(zb, zb, zb, misc, cumt)


def _sb_body(q_ref, k_ref, v_ref, tri_ref, o_ref, carry_scr, acc_scr):
    i = pl.program_id(2)
    tq = q_ref.shape[1]
    lo_half, hi_half = _head_masks()
    q = q_ref[0]
    q_heads = (jnp.where(lo_half, q, jnp.zeros_like(q)), jnp.where(hi_half, q, jnp.zeros_like(q)))

    carry_scr[...] = jnp.zeros_like(carry_scr)
    acc_scr[...] = jnp.zeros_like(acc_scr)

    def step(j, masked):
        start = pl.multiple_of(j * TK, TK)
        kb = k_ref[0, pl.ds(start, TK), :]
        vb = v_ref[0, pl.ds(start, TK), :]
        tri = tri_ref[...]
        for hd in range(2):
            z = _dot_nt(q_heads[hd], kb)
            lom = -_softplus(z)
            if masked:
                strict = _causal(tq, TK, True)
                lom = jnp.where(strict, lom, 0.0)
            hi, lo = _split2(lom)
            suffix = _dot(hi, tri) + _dot(lo, tri)
            a = jnp.exp(z + suffix + carry_scr[hd])
            if masked:
                a = jnp.where(strict, a, 0.0)
            acc_scr[hd] = acc_scr[hd] + _dot(a.astype(BF16), vb)
            carry_scr[hd] = carry_scr[hd] + suffix[:, 0:1]

    step(i, True)

    def body(n, c):
        step(i - 1 - n, False)
        return c

    lax.fori_loop(0, i, body, 0)
    o_ref[0] = jnp.where(lo_half, acc_scr[0], acc_scr[1]).astype(BF16)


def _sb(zb, tri_ge):
    b, s, _ = zb.shape
    nq = s // TQ
    n_pairs = W_SB // LANES
    return pl.pallas_call(
        _sb_body,
        grid=(b, n_pairs, nq),
        in_specs=[
            pl.BlockSpec((1, TQ, LANES), lambda bi, p, i: (bi, i, C_SQ // LANES + p)),
            pl.BlockSpec((1, s, LANES), lambda bi, p, i: (bi, 0, C_SK // LANES + p)),
            pl.BlockSpec((1, s, LANES), lambda bi, p, i: (bi, 0, C_SV // LANES + p)),
            pl.BlockSpec((TK, TK), lambda bi, p, i: (0, 0)),
        ],
        out_specs=pl.BlockSpec((1, TQ, LANES), lambda bi, p, i: (bi, i, p)),
        out_shape=jax.ShapeDtypeStruct((b, s, W_SB), BF16),
        scratch_shapes=[pltpu.VMEM((2, TQ, 1), F32), pltpu.VMEM((2, TQ, LANES), F32)],
        compiler_params=pltpu.CompilerParams(
            dimension_semantics=("parallel", "parallel", "arbitrary"), vmem_limit_bytes=VMEM_LIMIT),
        name="sb",
    )(zb, zb, zb, tri_ge)


def _dsa_body(iq_ref, ik_ref, q_ref, k_ref, v_ref, misc_ref, tri_ref, o_ref,
              keys_scr, bias_scr, m_scr, l_scr, acc_scr, *, topk):
    i = pl.program_id(1)
    tq = q_ref.shape[1]
    lo_half, hi_half = _head_masks()
    halves = (lo_half, hi_half)
    int_min = jnp.int32(-2 ** 31)

    iq = iq_ref[0]
    misc = misc_ref[0]
    iq_heads = []
    w_heads = []
    for hh in range(N_IDX_HEADS):
        q128 = iq[:, (hh // 2) * LANES:(hh // 2 + 1) * LANES]
        iq_heads.append(jnp.where(halves[hh % 2], q128, jnp.zeros_like(q128)))
        w_heads.append(misc[:, L_WIDX + hh:L_WIDX + hh + 1])

    def score_step(j, masked):
        start = pl.multiple_of(j * TK, TK)
        kb = ik_ref[0, pl.ds(start, TK), :]
        sc = jnp.zeros((tq, TK), F32)
        for hh in range(N_IDX_HEADS):
            sc = sc + w_heads[hh] * jnp.maximum(_dot_nt(iq_heads[hh], kb), 0.0)
        if masked:
            sc = jnp.where(_causal(tq, TK, False), sc, -jnp.inf)
        bits = lax.bitcast_convert_type(sc, jnp.int32)
        keys_scr[j] = jnp.where(bits < 0, bits ^ jnp.int32(0x7FFFFFFF), bits)

    def score_body(j, c):
        score_step(j, False)
        return c

    lax.fori_loop(0, i, score_body, 0)
    score_step(i, True)

    def count_ge(cand):
        def body(j, acc):
            c = jnp.where(keys_scr[j] >= cand, 1.0, 0.0)
            for ch in range(TK // LANES):
                acc = acc + c[:, ch * LANES:(ch + 1) * LANES]
            return acc
        acc = lax.fori_loop(0, i + 1, body, jnp.zeros((tq, LANES), F32))
        return jnp.sum(acc, axis=1, keepdims=True)

    kf = jnp.float32(topk)

    def bisect(it, thr):
        cand = thr + lax.shift_left(jnp.int32(1), 31 - it)
        return jnp.where(count_ge(cand) >= kf, cand, thr)

    thr = lax.fori_loop(0, 32, bisect, jnp.full((tq, 1), int_min, jnp.int32))
    need = kf - count_ge(thr + 1)

    def bias_step(j, run, masked):
        kblk = keys_scr[j]
        eq = kblk == thr
        eqf = jnp.where(eq, 1.0, 0.0)
        before = _dot(eqf.astype(BF16), tri_ref[...]) + run
        sel = (kblk > thr) | (eq & (before < need))
        if masked:
            sel = sel & _causal(tq, TK, False)
        bias_scr[j] = jnp.where(sel, 0.0, NEG_BIG)
        return run + jnp.sum(eqf, axis=1, keepdims=True)

    run = lax.fori_loop(0, i, lambda j, r: bias_step(j, r, False), jnp.zeros((tq, 1), F32))
    bias_step(i, run, True)

    q = q_ref[0]
    q_heads = []
    for hh in range(N_HEADS_DSA):
        q128 = q[:, (hh // 2) * LANES:(hh // 2 + 1) * LANES]
        q_heads.append(jnp.where(halves[hh % 2], q128, jnp.zeros_like(q128)))

    m_scr[...] = jnp.full_like(m_scr, -jnp.inf)
    l_scr[...] = jnp.zeros_like(l_scr)
    acc_scr[...] = jnp.zeros_like(acc_scr)

    def attn_body(j, c):
        start = pl.multiple_of(j * TK, TK)
        kb = k_ref[0, pl.ds(start, TK), :]
        vb = v_ref[0, pl.ds(start, TK), :]
        bias = bias_scr[j]
        for hh in range(N_HEADS_DSA):
            sl = slice((hh // 2) * LANES, (hh // 2 + 1) * LANES)
            s = _dot_nt(q_heads[hh], kb[:, sl]) + bias
            _softmax_step(s, vb[:, sl], m_scr, l_scr, acc_scr, hh)
        return c

    lax.fori_loop(0, i + 1, attn_body, 0)
    outs = []
    for pr in range(N_HEADS_DSA // 2):
        a, b = 2 * pr, 2 * pr + 1
        outs.append(jnp.where(lo_half, acc_scr[a] / l_scr[a], acc_scr[b] / l_scr[b]))
    o_ref[0] = jnp.concatenate(outs, axis=1).astype(BF16)


def _dsa(zb, misc, tri_lt, topk):
    b, s, _ = zb.shape
    nq = s // TQ
    nk = s // TK
    return pl.pallas_call(
        functools.partial(_dsa_body, topk=topk),
        grid=(b, nq),
        in_specs=[
            pl.BlockSpec((1, TQ, W_IDX), lambda bi, i: (bi, i, C_IQ // W_IDX)),
            pl.BlockSpec((1, s, LANES), lambda bi, i: (bi, 0, C_MISC // LANES)),
            pl.BlockSpec((1, TQ, W_DSA), lambda bi, i: (bi, i, C_CQ // W_DSA)),
            pl.BlockSpec((1, s, W_DSA), lambda bi, i: (bi, 0, C_CK // W_DSA)),
            pl.BlockSpec((1, s, W_DSA), lambda bi, i: (bi, 0, C_CV // W_DSA)),
            pl.BlockSpec((1, TQ, LANES), lambda bi, i: (bi, i, 0)),
            pl.BlockSpec((TK, TK), lambda bi, i: (0, 0)),
        ],
        out_specs=pl.BlockSpec((1, TQ, W_DSA), lambda bi, i: (bi, i, 0)),
        out_shape=jax.ShapeDtypeStruct((b, s, W_DSA), BF16),
        scratch_shapes=[
            pltpu.VMEM((nk, TQ, TK), jnp.int32),
            pltpu.VMEM((nk, TQ, TK), F32),
            pltpu.VMEM((N_HEADS_DSA, TQ, 1), F32),
            pltpu.VMEM((N_HEADS_DSA, TQ, 1), F32),
            pltpu.VMEM((N_HEADS_DSA, TQ, LANES), F32),
        ],
        compiler_params=pltpu.CompilerParams(
            dimension_semantics=("parallel", "arbitrary"), vmem_limit_bytes=VMEM_LIMIT),
        name="dsa",
    )(zb, zb, zb, zb, zb, misc, tri_lt)


def _merge_body(x_ref, g_ref, of_ref, os_ref, oc_ref, wg_ref, bg_ref, wf_ref, ws_ref, wc_ref, wo_ref, o_ref):
    x = x_ref[...]
    d = x.shape[1]
    h = _rms_rows(x, g_ref[...]).astype(BF16)
    merged = jnp.zeros_like(x)
    for n, (o_br, w_br) in enumerate(((of_ref, wf_ref), (os_ref, ws_ref), (oc_ref, wc_ref))):
        gate = _dot(h, wg_ref[:, n * d:(n + 1) * d]) + bg_ref[n:n + 1, :]
        gate = 1.0 / (1.0 + jnp.exp(-gate))
        merged = merged + gate * _dot(o_br[...], w_br[...])
    o_ref[...] = x + _dot(merged.astype(BF16), wo_ref[...])


def _merge(x2, g, o_f, o_s, o_c, w_gates, b_gates, w_f, w_s, w_c, w_o):
    m, d = x2.shape
    tm = min(TM_MERGE, m)
    row = lambda w: pl.BlockSpec((tm, w), lambda i: (i, 0))
    const = lambda a: pl.BlockSpec(a.shape, lambda i: (0, 0))
    return pl.pallas_call(
        _merge_body,
        grid=(m // tm,),
        in_specs=[row(d), const(g), row(W_FOX), row(W_SB), row(W_DSA), const(w_gates), const(b_gates),
                  const(w_f), const(w_s), const(w_c), const(w_o)],
        out_specs=row(d),
        out_shape=jax.ShapeDtypeStruct((m, d), F32),
        compiler_params=pltpu.CompilerParams(
            dimension_semantics=("parallel",), vmem_limit_bytes=VMEM_LIMIT),
        name="merge",
    )(x2, g, o_f, o_s, o_c, w_gates, b_gates, w_f, w_s, w_c, w_o)


def _tile_heads(g, n):
    return jnp.tile(g.astype(F32), n)


def _constants(tm_proj):
    r = np.arange(W_FOX)
    bd = (r[:, None] // HEAD_DIM == r[None, :] // HEAD_DIM).astype(np.float32)
    rt = np.arange(tm_proj)
    tri_tok = (rt[None, :] <= rt[:, None]).astype(np.float32)
    rk = np.arange(TK)
    tri_ge = (rk[:, None] >= rk[None, :]).astype(np.float32)
    tri_lt = (rk[:, None] < rk[None, :]).astype(np.float32)
    return (jnp.asarray(bd, BF16), jnp.asarray(tri_tok, BF16), jnp.asarray(tri_ge, BF16),
            jnp.asarray(tri_lt, BF16))


def _rope_freq_row():
    half = ROT_DIM // 2
    inv_freq = jnp.power(ROPE_THETA, -jnp.arange(half, dtype=F32) * 2.0 / ROT_DIM)
    d = np.arange(LANES) % HEAD_DIM
    row = jnp.where(jnp.asarray(d < ROT_DIM), inv_freq[jnp.asarray(d % half)], 0.0)
    return row.reshape(1, LANES).astype(F32)


def kernel(x, positions, ffn1_norm, ffn1_w_gate, ffn1_w_up, ffn1_w_down, mix_norm, w_in, b_forget, b_gates, q_norm_fox, k_norm_fox, q_norm_sb, k_norm_sb, q_norm_dsa, k_norm_dsa, w_branch_fox, w_branch_sb, w_branch_dsa, w_out, ffn2_norm, ffn2_w_gate, ffn2_w_up, ffn2_w_down):
    b, s, d = x.shape
    depth = w_in.shape[0]
    topk = min(TOPK_MAX, s // 4)
    assert s % TQ == 0 and s % TK == 0 and TQ == TK and d % LANES == 0
    tm_proj = min(TM_PROJ, s)
    bd, tri_tok, tri_ge, tri_lt = _constants(tm_proj)
    freq = _rope_freq_row()
    pos3 = positions.reshape(b, s, 1)
    scale = HEAD_DIM ** -0.5
    idx_scale = IDX_DIM ** -0.5

    splits = (W_FOX, W_FOX, W_FOX, N_HEADS_FOX, W_SB, W_SB, W_SB, W_DSA, W_DSA, W_DSA,
              W_IDX, IDX_DIM, N_IDX_HEADS, d, d, d)
    offs = np.concatenate([[0], np.cumsum(splits)]).tolist()
    (o_qf, o_kf, o_vf, o_ff, o_qs, o_ks, o_vs, o_qc, o_kc, o_vc, o_qi, o_ki, o_wi, o_ga) = offs[:14]

    x2 = x.reshape(b * s, d)
    for l in range(depth):
        wl = w_in[l]
        pad = jnp.zeros((d, LANES - IDX_DIM - N_HEADS_FOX - N_IDX_HEADS), F32)
        w_proj = jnp.concatenate([
            wl[:, o_qf:o_ff], wl[:, o_qs:o_qi], wl[:, o_qi:o_ki],
            wl[:, o_ki:o_wi], wl[:, o_ff:o_qs], wl[:, o_wi:o_ga], pad], axis=1).astype(BF16)
        w_gates = wl[:, o_ga:].astype(BF16)
        ones = lambda n: jnp.ones((n,), F32)
        gains = jnp.concatenate([
            _tile_heads(q_norm_fox[l], N_HEADS_FOX) * scale, _tile_heads(k_norm_fox[l], N_HEADS_FOX), ones(W_FOX),
            _tile_heads(q_norm_sb[l], N_HEADS_SB) * scale, _tile_heads(k_norm_sb[l], N_HEADS_SB), ones(W_SB),
            _tile_heads(q_norm_dsa[l], N_HEADS_DSA) * scale, _tile_heads(k_norm_dsa[l], N_HEADS_DSA), ones(W_DSA),
            ones(W_IDX) * idx_scale, ones(LANES)]).reshape(1, N_PROJ)
        fbias = jnp.zeros((LANES,), F32).at[L_FORGET:L_FORGET + N_HEADS_FOX].set(b_forget[l].astype(F32))
        fbias = fbias.reshape(1, LANES)

        x2 = _ffn(x2, ffn1_norm[l].reshape(1, d), ffn1_w_gate[l].astype(BF16), ffn1_w_up[l].astype(BF16),
                  ffn1_w_down[l].astype(BF16))
        zb, misc, cumt = _proj(x2.reshape(b, s, d), pos3, mix_norm[l].reshape(1, d), w_proj, gains, bd, freq,
                               fbias, tri_tok)
        o_f = _fox(zb, misc, cumt)
        o_s = _sb(zb, tri_ge)
        o_c = _dsa(zb, misc, tri_lt, topk)
        x2 = _merge(x2, mix_norm[l].reshape(1, d), o_f.reshape(b * s, W_FOX), o_s.reshape(b * s, W_SB),
                    o_c.reshape(b * s, W_DSA), w_gates, b_gates[l].astype(F32),
                    w_branch_fox[l].astype(BF16), w_branch_sb[l].astype(BF16), w_branch_dsa[l].astype(BF16),
                    w_out[l].astype(BF16))
        x2 = _ffn(x2, ffn2_norm[l].reshape(1, d), ffn2_w_gate[l].astype(BF16), ffn2_w_up[l].astype(BF16),
                  ffn2_w_down[l].astype(BF16))
    return x2.reshape(b, s, d)
```

```python
import functools

import jax
import jax.numpy as jnp
import numpy as np
from jax import lax
from jax.experimental import pallas as pl
from jax.experimental.pallas import tpu as pltpu

F32 = jnp.float32
BF16 = jnp.bfloat16

HEAD_DIM = 64
N_HEADS_FOX = 6
N_HEADS_SB = 6
N_HEADS_DSA = 4
N_IDX_HEADS = 4
IDX_DIM = 64
TOPK_MAX = 256
ROPE_THETA = 500000.0
ROT_DIM = HEAD_DIM // 4
EPS = 1e-6

LANES = 128
SUBLANES = 8
W_FOX = N_HEADS_FOX * HEAD_DIM
W_SB = N_HEADS_SB * HEAD_DIM
W_DSA = N_HEADS_DSA * HEAD_DIM
W_IDX = N_IDX_HEADS * IDX_DIM

C_FQ, C_FK, C_FV = 0, W_FOX, 2 * W_FOX
C_SQ, C_SK, C_SV = 3 * W_FOX, 3 * W_FOX + W_SB, 3 * W_FOX + 2 * W_SB
C_CQ = 3 * W_FOX + 3 * W_SB
C_CK, C_CV = C_CQ + W_DSA, C_CQ + 2 * W_DSA
C_IQ = C_CQ + 3 * W_DSA
C_MISC = C_IQ + W_IDX
N_PROJ = C_MISC + LANES
L_FORGET = IDX_DIM
L_WIDX = IDX_DIM + N_HEADS_FOX
MISC_ROWS = 16

W_FOX_AUG = N_HEADS_FOX * LANES
Z_FQ, Z_FK = 0, W_FOX_AUG
Z_SQ, Z_SK = 2 * W_FOX_AUG, 2 * W_FOX_AUG + W_SB
Z_CQ = 2 * W_FOX_AUG + 2 * W_SB
Z_CK = Z_CQ + W_DSA
Z_IQ = Z_CK + W_DSA
Z_IK = Z_IQ + W_IDX
N_ZB = Z_IK + LANES
N_BIAS_FEATS = 3

TQ = 256
TK = 256
TM_PROJ = 512
TM_FFN = 512
TM_MERGE = 512
NEG_BIG = -1e30
VMEM_LIMIT = 56 * 1024 * 1024

_NT = (((1,), (1,)), ((), ()))


def _dot(a, b):
    return jnp.dot(a, b, preferred_element_type=F32)


def _dot_nt(a, b):
    return lax.dot_general(a, b, _NT, preferred_element_type=F32)


def _split2(x):
    hi = x.astype(BF16)
    lo = (x - hi.astype(F32)).astype(BF16)
    return hi, lo


def _split3(x):
    hi = x.astype(BF16)
    r = x - hi.astype(F32)
    mid = r.astype(BF16)
    lo = (r - mid.astype(F32)).astype(BF16)
    return hi, mid, lo


def _rms_rows(x, g):
    ms = jnp.mean(x * x, axis=-1, keepdims=True)
    return x * lax.rsqrt(ms + EPS) * g


def _softplus(z):
    return jnp.maximum(z, 0.0) + jnp.log(1.0 + jnp.exp(-jnp.abs(z)))


def _ffn_body(x_ref, g_ref, wg_ref, wu_ref, wd_ref, o_ref, h_scr, acc_scr):
    f = pl.program_id(1)

    @pl.when(f == 0)
    def _():
        h_scr[...] = _rms_rows(x_ref[...], g_ref[...]).astype(BF16)
        acc_scr[...] = jnp.zeros_like(acc_scr)

    h = h_scr[...]
    a = _dot(h, wg_ref[...])
    u = _dot(h, wu_ref[...])
    p = (a * (1.0 / (1.0 + jnp.exp(-a))) * u).astype(BF16)
    acc_scr[...] += _dot(p, wd_ref[...])

    @pl.when(f == pl.num_programs(1) - 1)
    def _():
        o_ref[...] = x_ref[...] + 0.5 * acc_scr[...]


def _ffn_tile_f(d_ff):
    best = LANES
    for t in range(LANES, d_ff + 1, LANES):
        if d_ff % t == 0 and t <= 1536:
            best = t
    return best


def _ffn(x2, g, wg, wu, wd):
    m, d = x2.shape
    d_ff = wg.shape[1]
    tm = min(TM_FFN, m)
    tf = _ffn_tile_f(d_ff)
    return pl.pallas_call(
        _ffn_body,
        grid=(m // tm, d_ff // tf),
        in_specs=[
            pl.BlockSpec((tm, d), lambda i, f: (i, 0)),
            pl.BlockSpec((1, d), lambda i, f: (0, 0)),
            pl.BlockSpec((d, tf), lambda i, f: (0, f)),
            pl.BlockSpec((d, tf), lambda i, f: (0, f)),
            pl.BlockSpec((tf, d), lambda i, f: (f, 0)),
        ],
        out_specs=pl.BlockSpec((tm, d), lambda i, f: (i, 0)),
        out_shape=jax.ShapeDtypeStruct((m, d), F32),
        scratch_shapes=[pltpu.VMEM((tm, d), BF16), pltpu.VMEM((tm, d), F32)],
        compiler_params=pltpu.CompilerParams(
            dimension_semantics=("parallel", "arbitrary"), vmem_limit_bytes=VMEM_LIMIT),
        name="ffn",
    )(x2, g, wg, wu, wd)


def _proj_body(x_ref, pos_ref, g_ref, w_ref, gains_ref, bd_ref, freq_ref, fbias_ref, tri_ref,
               pq_ref, pk_ref, ones_q_ref, ones_k_ref,
               zb_ref, vtf_ref, vts_ref, vtc_ref, misct_ref, carry_scr):
    t = pl.program_id(1)
    tm = x_ref.shape[1]
    h = _rms_rows(x_ref[0], g_ref[...]).astype(BF16)

    lane = lax.broadcasted_iota(jnp.int32, (1, LANES), 1)
    d_in_head = lane % HEAD_DIM
    ang = pos_ref[0].astype(F32) * freq_ref[...]
    cos = jnp.cos(ang)
    sin_signed = jnp.where(d_in_head < ROT_DIM // 2, -jnp.sin(ang), jnp.sin(ang))
    first_half = d_in_head < ROT_DIM // 2

    def rope(x):
        partner = jnp.where(first_half, pltpu.roll(x, LANES - ROT_DIM // 2, 1), pltpu.roll(x, ROT_DIM // 2, 1))
        return x * cos + partner * sin_signed

    def head_norm(z, c0, width):
        hi, lo = _split2(z * z)
        bd = bd_ref[:width, :width]
        ss = _dot(hi, bd) + _dot(lo, bd)
        return z * lax.rsqrt(ss * (1.0 / HEAD_DIM) + EPS) * gains_ref[:, c0:c0 + width]

    def group(c0, width, norm, rot):
        z = _dot(h, w_ref[:, c0:c0 + width])
        if norm:
            z = head_norm(z, c0, width)
        else:
            z = z * gains_ref[:, c0:c0 + width]
        if rot:
            z = jnp.concatenate([rope(z[:, c:c + LANES]) for c in range(0, width, LANES)], axis=1)
        return z

    def store_vt(vt_ref, c0, width):
        zt = _dot(h, w_ref[:, c0:c0 + width]).T
        for p in range(width // LANES):
            for c in range(tm // TK):
                vt_ref[0, p, c] = zt[p * LANES:(p + 1) * LANES, c * TK:(c + 1) * TK].astype(BF16)

    zm = _dot(h, w_ref[:, C_MISC:C_MISC + LANES])
    ki = rope(zm)
    zb_ref[0, :, Z_IK:Z_IK + LANES] = jnp.where(lane < IDX_DIM, ki, pltpu.roll(ki, IDX_DIM, 1)).astype(BF16)

    @pl.when(t == 0)
    def _():
        carry_scr[...] = jnp.zeros_like(carry_scr)

    logf = -_softplus(-(zm + fbias_ref[...]))
    hi, mid, lo = _split3(logf)
    tri = tri_ref[...]
    cum = _dot(tri, hi) + _dot(tri, mid) + _dot(tri, lo) + carry_scr[0:1, :]
    carry_scr[...] = jnp.broadcast_to(cum[tm - 1:tm, :], carry_scr.shape)

    is_forget = (lane >= L_FORGET) & (lane < L_FORGET + N_HEADS_FOX)
    is_widx = (lane >= L_WIDX) & (lane < L_WIDX + N_IDX_HEADS)
    misc = jnp.where(is_forget, cum, jnp.where(is_widx, zm * (N_IDX_HEADS ** -0.5), 0.0))
    misc_t = misc.T
    for c in range(tm // TQ):
        misct_ref[0, c] = misc_t[L_FORGET:L_FORGET + MISC_ROWS, c * TQ:(c + 1) * TQ]

    c_hi, c_mid, c_lo = _split3(cum)
    for c0, z0, p_ref, ones_ref in ((C_FQ, Z_FQ, pq_ref, ones_q_ref), (C_FK, Z_FK, pk_ref, ones_k_ref)):
        zn = group(c0, W_FOX, True, False).astype(BF16)
        src = jnp.concatenate([zn, c_hi, c_mid, c_lo], axis=1)
        zb_ref[0, :, z0:z0 + W_FOX_AUG] = (_dot(src, p_ref[...]) + ones_ref[...]).astype(BF16)

    zb_ref[0, :, Z_SQ:Z_SQ + W_SB] = group(C_SQ, W_SB, True, False).astype(BF16)
    zb_ref[0, :, Z_SK:Z_SK + W_SB] = group(C_SK, W_SB, True, False).astype(BF16)
    zb_ref[0, :, Z_CQ:Z_CQ + W_DSA] = group(C_CQ, W_DSA, True, True).astype(BF16)
    zb_ref[0, :, Z_CK:Z_CK + W_DSA] = group(C_CK, W_DSA, True, True).astype(BF16)
    zb_ref[0, :, Z_IQ:Z_IQ + W_IDX] = group(C_IQ, W_IDX, False, True).astype(BF16)
    store_vt(vtf_ref, C_FV, W_FOX)
    store_vt(vts_ref, C_SV, W_SB)
    store_vt(vtc_ref, C_CV, W_DSA)


def _proj(x, pos3, g, w, gains, consts, fbias):
    b, s, d = x.shape
    tm = min(TM_PROJ, s)
    const = lambda a: pl.BlockSpec(a.shape, lambda bi, ti: (0,) * a.ndim)
    vt_spec = lambda n: pl.BlockSpec((1, n, tm // TK, LANES, TK), lambda bi, ti: (bi, 0, ti, 0, 0))
    vt_shape = lambda n: jax.ShapeDtypeStruct((b, n, s // TK, LANES, TK), BF16)
    cs = (consts["bd"], consts["freq"], fbias, consts["tri_tok"], consts["pq"], consts["pk"],
          consts["ones_q"], consts["ones_k"])
    return pl.pallas_call(
        _proj_body,
        grid=(b, s // tm),
        in_specs=[
            pl.BlockSpec((1, tm, d), lambda bi, ti: (bi, ti, 0)),
            pl.BlockSpec((1, tm, 1), lambda bi, ti: (bi, ti, 0)),
            const(g), const(w), const(gains),
        ] + [const(a) for a in cs],
        out_specs=[
            pl.BlockSpec((1, tm, N_ZB), lambda bi, ti: (bi, ti, 0)),
            vt_spec(W_FOX // LANES), vt_spec(W_SB // LANES), vt_spec(W_DSA // LANES),
            pl.BlockSpec((1, tm // TQ, MISC_ROWS, TQ), lambda bi, ti: (bi, ti, 0, 0)),
        ],
        out_shape=[
            jax.ShapeDtypeStruct((b, s, N_ZB), BF16),
            vt_shape(W_FOX // LANES), vt_shape(W_SB // LANES), vt_shape(W_DSA // LANES),
            jax.ShapeDtypeStruct((b, s // TQ, MISC_ROWS, TQ), F32),
        ],
        scratch_shapes=[pltpu.VMEM((SUBLANES, LANES), F32)],
        compiler_params=pltpu.CompilerParams(
            dimension_semantics=("parallel", "arbitrary"), vmem_limit_bytes=VMEM_LIMIT),
        name="proj",
    )(x, pos3, g, w, gains, *cs)


def _head_masks():
    lane = lax.broadcasted_iota(jnp.int32, (1, LANES), 1)
    return lane < HEAD_DIM, lane >= HEAD_DIM


def _split_heads(q128):
    lo_half, hi_half = _head_masks()
    zero = jnp.zeros_like(q128)
    return jnp.where(lo_half, q128, zero), jnp.where(hi_half, q128, zero)


def _visible(tk, tq, strict):
    key = lax.broadcasted_iota(jnp.int32, (tk, tq), 0)
    qry = lax.broadcasted_iota(jnp.int32, (tk, tq), 1)
    return key < qry if strict else key <= qry


def _softmax_update(s, m, l):
    m_new = jnp.maximum(m, jnp.max(s, axis=0, keepdims=True))
    alpha = jnp.exp(m - m_new)
    p = jnp.exp(s - m_new)
    return p, alpha, m_new, alpha * l + jnp.sum(p, axis=0, keepdims=True)


def _head_rows(hd):
    return slice(hd * HEAD_DIM, (hd + 1) * HEAD_DIM)


def _fox_body(q_ref, k_ref, vt_ref, o_ref, acc_scr):
    i = pl.program_id(1)
    tq = q_ref.shape[1]
    nh = N_HEADS_FOX
    q_heads = [q_ref[0, :, hd * LANES:(hd + 1) * LANES] for hd in range(nh)]
    acc_scr[...] = jnp.zeros_like(acc_scr)

    def step(j, state, masked):
        kb = k_ref[0, pl.ds(pl.multiple_of(j * TK, TK), TK), :]
        scores = [_dot_nt(kb[:, hd * LANES:(hd + 1) * LANES], q_heads[hd]) for hd in range(nh)]
        if masked:
            vis = _visible(TK, tq, False)
            scores = [jnp.where(vis, s, -jnp.inf) for s in scores]
        upd = [_softmax_update(scores[hd], *state[hd]) for hd in range(nh)]
        for hd in range(nh):
            p, alpha = upd[hd][0], upd[hd][1]
            pr, rows = hd // 2, _head_rows(hd % 2)
            acc_scr[pr, rows, :] = alpha * acc_scr[pr, rows, :] + _dot(vt_ref[0, pr, j, rows, :], p.astype(BF16))
        return tuple((u[2], u[3]) for u in upd)

    init = tuple((jnp.full((1, tq), -jnp.inf, F32), jnp.zeros((1, tq), F32)) for _ in range(nh))
    state = lax.fori_loop(0, i, lambda j, st: step(j, st, False), init)
    state = step(i, state, True)
    for pr in range(nh // 2):
        inv = jnp.concatenate([jnp.broadcast_to(1.0 / state[2 * pr + hd][1], (HEAD_DIM, tq)) for hd in range(2)],
                              axis=0)
        o_ref[0, :, pr * LANES:(pr + 1) * LANES] = (acc_scr[pr] * inv).T.astype(BF16)


def _fox(zb, vt):
    b, s, _ = zb.shape
    nq, nk = s // TQ, s // TK
    n_pairs = W_FOX // LANES
    return pl.pallas_call(
        _fox_body,
        grid=(b, nq),
        in_specs=[
            pl.BlockSpec((1, TQ, W_FOX_AUG), lambda bi, i: (bi, i, Z_FQ // W_FOX_AUG)),
            pl.BlockSpec((1, s, W_FOX_AUG), lambda bi, i: (bi, 0, Z_FK // W_FOX_AUG)),
            pl.BlockSpec((1, n_pairs, nk, LANES, TK), lambda bi, i: (bi, 0, 0, 0, 0)),
        ],
        out_specs=pl.BlockSpec((1, TQ, W_FOX), lambda bi, i: (bi, i, 0)),
        out_shape=jax.ShapeDtypeStruct((b, s, W_FOX), BF16),
        scratch_shapes=[pltpu.VMEM((n_pairs, LANES, TQ), F32)],
        compiler_params=pltpu.CompilerParams(
            dimension_semantics=("parallel", "arbitrary"), vmem_limit_bytes=VMEM_LIMIT),
        name="fox",
    )(zb, zb, vt)


def _sb_body(q_ref, k_ref, vt_ref, tri_ref, o_ref, acc_scr):
    i = pl.program_id(1)
    tq = q_ref.shape[1]
    nh = N_HEADS_SB
    q_heads = []
    for pr in range(nh // 2):
        q_heads.extend(_split_heads(q_ref[0, :, pr * LANES:(pr + 1) * LANES]))
    acc_scr[...] = jnp.zeros_like(acc_scr)

    def step(j, carry, masked):
        kb = k_ref[0, pl.ds(pl.multiple_of(j * TK, TK), TK), :]
        tri = tri_ref[...]
        zs = [_dot_nt(kb[:, (hd // 2) * LANES:(hd // 2 + 1) * LANES], q_heads[hd]) for hd in range(nh)]
        loms = [-_softplus(z) for z in zs]
        if masked:
            strict = _visible(TK, tq, True)
            loms = [jnp.where(strict, lom, 0.0) for lom in loms]
        splits = [_split2(lom) for lom in loms]
        suffixes = [_dot(tri, hi) + _dot(tri, lo) for hi, lo in splits]
        new_carry = []
        for hd in range(nh):
            a = jnp.exp(zs[hd] + suffixes[hd] + carry[hd])
            if masked:
                a = jnp.where(strict, a, 0.0)
            pr, rows = hd // 2, _head_rows(hd % 2)
            acc_scr[pr, rows, :] = acc_scr[pr, rows, :] + _dot(vt_ref[0, pr, j, rows, :], a.astype(BF16))
            new_carry.append(carry[hd] + suffixes[hd][0:1, :])
        return tuple(new_carry)

    carry = step(i, tuple(jnp.zeros((1, tq), F32) for _ in range(nh)), True)
    lax.fori_loop(0, i, lambda n, c: step(i - 1 - n, c, False), carry)
    for pr in range(nh // 2):
        o_ref[0, :, pr * LANES:(pr + 1) * LANES] = acc_scr[pr].T.astype(BF16)


def _sb(zb, vt, tri_ge):
    b, s, _ = zb.shape
    nq, nk = s // TQ, s // TK
    n_pairs = W_SB // LANES
    return pl.pallas_call(
        _sb_body,
        grid=(b, nq),
        in_specs=[
            pl.BlockSpec((1, TQ, W_SB), lambda bi, i: (bi, i, Z_SQ // W_SB)),
            pl.BlockSpec((1, s, W_SB), lambda bi, i: (bi, 0, Z_SK // W_SB)),
            pl.BlockSpec((1, n_pairs, nk, LANES, TK), lambda bi, i: (bi, 0, 0, 0, 0)),
            pl.BlockSpec((TK, TK), lambda bi, i: (0, 0)),
        ],
        out_specs=pl.BlockSpec((1, TQ, W_SB), lambda bi, i: (bi, i, 0)),
        out_shape=jax.ShapeDtypeStruct((b, s, W_SB), BF16),
        scratch_shapes=[pltpu.VMEM((n_pairs, LANES, TQ), F32)],
        compiler_params=pltpu.CompilerParams(
            dimension_semantics=("parallel", "arbitrary"), vmem_limit_bytes=VMEM_LIMIT),
        name="sb",
    )(zb, zb, vt, tri_ge)


def _dsa_body(iq_ref, ik_ref, q_ref, k_ref, vt_ref, misct_ref, tri_ref, o_ref,
              keys_scr, bias_scr, acc_scr, *, topk):
    i = pl.program_id(1)
    tq = q_ref.shape[1]
    n_pairs = N_HEADS_DSA // 2
    int_min = jnp.int32(-2 ** 31)

    iq = iq_ref[0]
    iq_heads = _split_heads(iq[:, 0:LANES]) + _split_heads(iq[:, LANES:2 * LANES])
    w_rows = [misct_ref[0, 0, L_WIDX - L_FORGET + hh:L_WIDX - L_FORGET + hh + 1, :] for hh in range(N_IDX_HEADS)]

    def score_step(j, masked):
        kb = ik_ref[0, pl.ds(pl.multiple_of(j * TK, TK), TK), :]
        sc = jnp.zeros((TK, tq), F32)
        for hh in range(N_IDX_HEADS):
            sc = sc + w_rows[hh] * jnp.maximum(_dot_nt(kb, iq_heads[hh]), 0.0)
        if masked:
            sc = jnp.where(_visible(TK, tq, False), sc, -jnp.inf)
        bits = lax.bitcast_convert_type(sc, jnp.int32)
        keys_scr[j] = jnp.where(bits < 0, bits ^ jnp.int32(0x7FFFFFFF), bits)

    def score_body(j, c):
        score_step(j, False)
        return c

    lax.fori_loop(0, i, score_body, 0)
    score_step(i, True)

    def count_ge(cand):
        def body(j, acc):
            c = jnp.where(keys_scr[j] >= cand, 1.0, 0.0)
            return acc + jnp.sum(c.reshape(TK // SUBLANES, SUBLANES, tq), axis=0)
        acc = lax.fori_loop(0, i + 1, body, jnp.zeros((SUBLANES, tq), F32))
        return jnp.sum(acc, axis=0, keepdims=True)

    kf = jnp.float32(topk)

    def bisect(it, thr):
        cand = thr + lax.shift_left(jnp.int32(1), 31 - it)
        return jnp.where(count_ge(cand) >= kf, cand, thr)

    thr = lax.fori_loop(0, 32, bisect, jnp.full((1, tq), int_min, jnp.int32))
    need = kf - count_ge(thr + 1)

    def bias_step(j, run, masked):
        kblk = keys_scr[j]
        eq = kblk == thr
        eqf = jnp.where(eq, 1.0, 0.0)
        before = _dot(tri_ref[...], eqf.astype(BF16)) + run
        sel = (kblk > thr) | (eq & (before < need))
        if masked:
            sel = sel & _visible(TK, tq, False)
        bias_scr[j] = jnp.where(sel, 0.0, NEG_BIG)
        return run + jnp.sum(eqf, axis=0, keepdims=True)

    run = lax.fori_loop(0, i, lambda j, r: bias_step(j, r, False), jnp.zeros((1, tq), F32))
    bias_step(i, run, True)

    q = q_ref[0]
    q_heads = _split_heads(q[:, 0:LANES]) + _split_heads(q[:, LANES:2 * LANES])
    acc_scr[...] = jnp.zeros_like(acc_scr)

    def attn_step(j, state):
        kb = k_ref[0, pl.ds(pl.multiple_of(j * TK, TK), TK), :]
        bias = bias_scr[j]
        scores = [_dot_nt(kb[:, (hh // 2) * LANES:(hh // 2 + 1) * LANES], q_heads[hh]) + bias
                  for hh in range(N_HEADS_DSA)]
        upd = [_softmax_update(scores[hh], *state[hh]) for hh in range(N_HEADS_DSA)]
        for hh in range(N_HEADS_DSA):
            p, alpha = upd[hh][0], upd[hh][1]
            pr, rows = hh // 2, _head_rows(hh % 2)
            acc_scr[pr, rows, :] = alpha * acc_scr[pr, rows, :] + _dot(vt_ref[0, pr, j, rows, :], p.astype(BF16))
        return tuple((u[2], u[3]) for u in upd)

    init = tuple((jnp.full((1, tq), -jnp.inf, F32), jnp.zeros((1, tq), F32)) for _ in range(N_HEADS_DSA))
    state = lax.fori_loop(0, i + 1, attn_step, init)
    outs = []
    for pr in range(n_pairs):
        inv = jnp.concatenate([jnp.broadcast_to(1.0 / state[2 * pr + hd][1], (HEAD_DIM, tq)) for hd in range(2)],
                              axis=0)
        outs.append((acc_scr[pr] * inv).T)
    o_ref[0] = jnp.concatenate(outs, axis=1).astype(BF16)


def _dsa(zb, vt, misct, tri_lt, topk):
    b, s, _ = zb.shape
    nq, nk = s // TQ, s // TK
    n_pairs = W_DSA // LANES
    return pl.pallas_call(
        functools.partial(_dsa_body, topk=topk),
        grid=(b, nq),
        in_specs=[
            pl.BlockSpec((1, TQ, W_IDX), lambda bi, i: (bi, i, Z_IQ // W_IDX)),
            pl.BlockSpec((1, s, LANES), lambda bi, i: (bi, 0, Z_IK // LANES)),
            pl.BlockSpec((1, TQ, W_DSA), lambda bi, i: (bi, i, Z_CQ // W_DSA)),
            pl.BlockSpec((1, s, W_DSA), lambda bi, i: (bi, 0, Z_CK // W_DSA)),
            pl.BlockSpec((1, n_pairs, nk, LANES, TK), lambda bi, i: (bi, 0, 0, 0, 0)),
            pl.BlockSpec((1, 1, MISC_ROWS, TQ), lambda bi, i: (bi, i, 0, 0)),
            pl.BlockSpec((TK, TK), lambda bi, i: (0, 0)),
        ],
        out_specs=pl.BlockSpec((1, TQ, W_DSA), lambda bi, i: (bi, i, 0)),
        out_shape=jax.ShapeDtypeStruct((b, s, W_DSA), BF16),
        scratch_shapes=[
            pltpu.VMEM((nk, TK, TQ), jnp.int32),
            pltpu.VMEM((nk, TK, TQ), F32),
            pltpu.VMEM((n_pairs, LANES, TQ), F32),
        ],
        compiler_params=pltpu.CompilerParams(
            dimension_semantics=("parallel", "arbitrary"), vmem_limit_bytes=VMEM_LIMIT),
        name="dsa",
    )(zb, zb, zb, zb, vt, misct, tri_lt)


def _merge_body(x_ref, g_ref, of_ref, os_ref, oc_ref, wg_ref, bg_ref, wf_ref, ws_ref, wc_ref, wo_ref, o_ref):
    x = x_ref[...]
    d = x.shape[1]
    h = _rms_rows(x, g_ref[...]).astype(BF16)
    merged = jnp.zeros_like(x)
    for n, (o_br, w_br) in enumerate(((of_ref, wf_ref), (os_ref, ws_ref), (oc_ref, wc_ref))):
        gate = _dot(h, wg_ref[:, n * d:(n + 1) * d]) + bg_ref[n:n + 1, :]
        gate = 1.0 / (1.0 + jnp.exp(-gate))
        merged = merged + gate * _dot(o_br[...], w_br[...])
    o_ref[...] = x + _dot(merged.astype(BF16), wo_ref[...])


def _merge(x2, g, o_f, o_s, o_c, w_gates, b_gates, w_f, w_s, w_c, w_o):
    m, d = x2.shape
    tm = min(TM_MERGE, m)
    row = lambda w: pl.BlockSpec((tm, w), lambda i: (i, 0))
    const = lambda a: pl.BlockSpec(a.shape, lambda i: (0, 0))
    return pl.pallas_call(
        _merge_body,
        grid=(m // tm,),
        in_specs=[row(d), const(g), row(W_FOX), row(W_SB), row(W_DSA), const(w_gates), const(b_gates),
                  const(w_f), const(w_s), const(w_c), const(w_o)],
        out_specs=row(d),
        out_shape=jax.ShapeDtypeStruct((m, d), F32),
        compiler_params=pltpu.CompilerParams(
            dimension_semantics=("parallel",), vmem_limit_bytes=VMEM_LIMIT),
        name="merge",
    )(x2, g, o_f, o_s, o_c, w_gates, b_gates, w_f, w_s, w_c, w_o)


def _tile_heads(g, n):
    return jnp.tile(g.astype(F32), n)


def _constants(tm_proj):
    r = np.arange(W_FOX)
    bd = (r[:, None] // HEAD_DIM == r[None, :] // HEAD_DIM).astype(np.float32)
    rt = np.arange(tm_proj)
    tri_tok = (rt[None, :] <= rt[:, None]).astype(np.float32)
    rk = np.arange(TK)
    tri_ge = (rk[None, :] >= rk[:, None]).astype(np.float32)
    tri_lt = (rk[None, :] < rk[:, None]).astype(np.float32)

    n_src = W_FOX + N_BIAS_FEATS * LANES
    pq = np.zeros((n_src, W_FOX_AUG), np.float32)
    pk = np.zeros((n_src, W_FOX_AUG), np.float32)
    ones_q = np.zeros((1, W_FOX_AUG), np.float32)
    ones_k = np.zeros((1, W_FOX_AUG), np.float32)
    for hd in range(N_HEADS_FOX):
        for dd in range(HEAD_DIM):
            pq[hd * HEAD_DIM + dd, hd * LANES + dd] = 1.0
            pk[hd * HEAD_DIM + dd, hd * LANES + dd] = 1.0
        for part in range(N_BIAS_FEATS):
            src_row = W_FOX + part * LANES + L_FORGET + hd
            pq[src_row, hd * LANES + HEAD_DIM + part] = 1.0
            ones_q[0, hd * LANES + HEAD_DIM + N_BIAS_FEATS + part] = 1.0
            ones_k[0, hd * LANES + HEAD_DIM + part] = 1.0
            pk[src_row, hd * LANES + HEAD_DIM + N_BIAS_FEATS + part] = -1.0
    half = ROT_DIM // 2
    inv_freq = jnp.power(ROPE_THETA, -jnp.arange(half, dtype=F32) * 2.0 / ROT_DIM)
    dlane = np.arange(LANES) % HEAD_DIM
    freq = jnp.where(jnp.asarray(dlane < ROT_DIM), inv_freq[jnp.asarray(dlane % half)], 0.0)
    return dict(bd=jnp.asarray(bd, BF16), tri_tok=jnp.asarray(tri_tok, BF16), tri_ge=jnp.asarray(tri_ge, BF16),
                tri_lt=jnp.asarray(tri_lt, BF16), pq=jnp.asarray(pq, BF16), pk=jnp.asarray(pk, BF16),
                ones_q=jnp.asarray(ones_q, F32), ones_k=jnp.asarray(ones_k, F32),
                freq=freq.reshape(1, LANES).astype(F32))


def kernel(x, positions, ffn1_norm, ffn1_w_gate, ffn1_w_up, ffn1_w_down, mix_norm, w_in, b_forget, b_gates, q_norm_fox, k_norm_fox, q_norm_sb, k_norm_sb, q_norm_dsa, k_norm_dsa, w_branch_fox, w_branch_sb, w_branch_dsa, w_out, ffn2_norm, ffn2_w_gate, ffn2_w_up, ffn2_w_down):
    b, s, d = x.shape
    depth = w_in.shape[0]
    topk = min(TOPK_MAX, s // 4)
    assert s % TQ == 0 and s % TK == 0 and TQ == TK and d % LANES == 0
    consts = _constants(min(TM_PROJ, s))
    pos3 = positions.reshape(b, s, 1)
    scale = HEAD_DIM ** -0.5
    idx_scale = IDX_DIM ** -0.5

    splits = (W_FOX, W_FOX, W_FOX, N_HEADS_FOX, W_SB, W_SB, W_SB, W_DSA, W_DSA, W_DSA,
              W_IDX, IDX_DIM, N_IDX_HEADS, d, d, d)
    offs = np.concatenate([[0], np.cumsum(splits)]).tolist()
    (o_qf, o_kf, o_vf, o_ff, o_qs, o_ks, o_vs, o_qc, o_kc, o_vc, o_qi, o_ki, o_wi, o_ga) = offs[:14]

    x2 = x.reshape(b * s, d)
    for l in range(depth):
        wl = w_in[l]
        pad = jnp.zeros((d, LANES - IDX_DIM - N_HEADS_FOX - N_IDX_HEADS), F32)
        w_proj = jnp.concatenate([
            wl[:, o_qf:o_ff], wl[:, o_qs:o_qi], wl[:, o_qi:o_ki],
            wl[:, o_ki:o_wi], wl[:, o_ff:o_qs], wl[:, o_wi:o_ga], pad], axis=1).astype(BF16)
        w_gates = wl[:, o_ga:].astype(BF16)
        ones = lambda n: jnp.ones((n,), F32)
        gains = jnp.concatenate([
            _tile_heads(q_norm_fox[l], N_HEADS_FOX) * scale, _tile_heads(k_norm_fox[l], N_HEADS_FOX), ones(W_FOX),
            _tile_heads(q_norm_sb[l], N_HEADS_SB) * scale, _tile_heads(k_norm_sb[l], N_HEADS_SB), ones(W_SB),
            _tile_heads(q_norm_dsa[l], N_HEADS_DSA) * scale, _tile_heads(k_norm_dsa[l], N_HEADS_DSA), ones(W_DSA),
            ones(W_IDX) * idx_scale, ones(LANES)]).reshape(1, N_PROJ)
        fbias = jnp.zeros((LANES,), F32).at[L_FORGET:L_FORGET + N_HEADS_FOX].set(b_forget[l].astype(F32))
        fbias = fbias.reshape(1, LANES)

        x2 = _ffn(x2, ffn1_norm[l].reshape(1, d), ffn1_w_gate[l].astype(BF16), ffn1_w_up[l].astype(BF16),
                  ffn1_w_down[l].astype(BF16))
        zb, vt_f, vt_s, vt_c, misct = _proj(x2.reshape(b, s, d), pos3, mix_norm[l].reshape(1, d), w_proj, gains,
                                            consts, fbias)
        o_f = _fox(zb, vt_f)
        o_s = _sb(zb, vt_s, consts["tri_ge"])
        o_c = _dsa(zb, vt_c, misct, consts["tri_lt"], topk)
        x2 = _merge(x2, mix_norm[l].reshape(1, d), o_f.reshape(b * s, W_FOX), o_s.reshape(b * s, W_SB),
                    o_c.reshape(b * s, W_DSA), w_gates, b_gates[l].astype(F32),
                    w_branch_fox[l].astype(BF16), w_branch_sb[l].astype(BF16), w_branch_dsa[l].astype(BF16),
                    w_out[l].astype(BF16))
        x2 = _ffn(x2, ffn2_norm[l].reshape(1, d), ffn2_w_gate[l].astype(BF16), ffn2_w_up[l].astype(BF16),
                  ffn2_w_down[l].astype(BF16))
    return x2.reshape(b, s, d)
```

```python
import functools

import jax
import jax.numpy as jnp
import numpy as np
from jax import lax
from jax.experimental import pallas as pl
from jax.experimental.pallas import tpu as pltpu

F32 = jnp.float32
BF16 = jnp.bfloat16

HEAD_DIM = 64
N_HEADS_FOX = 6
N_HEADS_SB = 6
N_HEADS_DSA = 4
N_IDX_HEADS = 4
IDX_DIM = 64
TOPK_MAX = 256
ROPE_THETA = 500000.0
ROT_DIM = HEAD_DIM // 4
EPS = 1e-6

LANES = 128
SUBLANES = 8
W_FOX = N_HEADS_FOX * HEAD_DIM
W_SB = N_HEADS_SB * HEAD_DIM
W_DSA = N_HEADS_DSA * HEAD_DIM
W_IDX = N_IDX_HEADS * IDX_DIM

C_FQ, C_FK, C_FV = 0, W_FOX, 2 * W_FOX
C_SQ, C_SK, C_SV = 3 * W_FOX, 3 * W_FOX + W_SB, 3 * W_FOX + 2 * W_SB
C_CQ = 3 * W_FOX + 3 * W_SB
C_CK, C_CV = C_CQ + W_DSA, C_CQ + 2 * W_DSA
C_IQ = C_CQ + 3 * W_DSA
C_MISC = C_IQ + W_IDX
N_PROJ = C_MISC + LANES
L_FORGET = IDX_DIM
L_WIDX = IDX_DIM + N_HEADS_FOX
MISC_ROWS = 16

W_FOX_AUG = N_HEADS_FOX * LANES
Z_FQ, Z_FK = 0, W_FOX_AUG
Z_SQ, Z_SK = 2 * W_FOX_AUG, 2 * W_FOX_AUG + W_SB
Z_CQ = 2 * W_FOX_AUG + 2 * W_SB
Z_CK = Z_CQ + W_DSA
Z_IQ = Z_CK + W_DSA
Z_IK = Z_IQ + W_IDX
N_ZB = Z_IK + LANES
N_BIAS_FEATS = 3

TQ = 256
TK = 256
TM_PROJ = 512
TM_FFN = 512
TM_MERGE = 512
NEG_BIG = -1e30
EXP_UNDERFLOW = -110.0
VMEM_LIMIT = 56 * 1024 * 1024

_NT = (((1,), (1,)), ((), ()))


def _dot(a, b):
    return jnp.dot(a, b, preferred_element_type=F32)


def _dot_nt(a, b):
    return lax.dot_general(a, b, _NT, preferred_element_type=F32)


def _split2(x):
    hi = x.astype(BF16)
    lo = (x - hi.astype(F32)).astype(BF16)
    return hi, lo


def _split3(x):
    hi = x.astype(BF16)
    r = x - hi.astype(F32)
    mid = r.astype(BF16)
    lo = (r - mid.astype(F32)).astype(BF16)
    return hi, mid, lo


def _rms_rows(x, g):
    ms = jnp.mean(x * x, axis=-1, keepdims=True)
    return x * lax.rsqrt(ms + EPS) * g


def _softplus(z):
    return jnp.maximum(z, 0.0) + jnp.log(1.0 + jnp.exp(-jnp.abs(z)))


def _ffn_body(x_ref, g_ref, wg_ref, wu_ref, wd_ref, o_ref, h_scr, acc_scr):
    f = pl.program_id(1)

    @pl.when(f == 0)
    def _():
        h_scr[...] = _rms_rows(x_ref[...], g_ref[...]).astype(BF16)
        acc_scr[...] = jnp.zeros_like(acc_scr)

    h = h_scr[...]
    a = _dot(h, wg_ref[...])
    u = _dot(h, wu_ref[...])
    p = (a * (1.0 / (1.0 + jnp.exp(-a))) * u).astype(BF16)
    acc_scr[...] += _dot(p, wd_ref[...])

    @pl.when(f == pl.num_programs(1) - 1)
    def _():
        o_ref[...] = x_ref[...] + 0.5 * acc_scr[...]


def _ffn_tile_f(d_ff):
    best = LANES
    for t in range(LANES, d_ff + 1, LANES):
        if d_ff % t == 0 and t <= 1536:
            best = t
    return best


def _ffn(x2, g, wg, wu, wd):
    m, d = x2.shape
    d_ff = wg.shape[1]
    tm = min(TM_FFN, m)
    tf = _ffn_tile_f(d_ff)
    return pl.pallas_call(
        _ffn_body,
        grid=(m // tm, d_ff // tf),
        in_specs=[
            pl.BlockSpec((tm, d), lambda i, f: (i, 0)),
            pl.BlockSpec((1, d), lambda i, f: (0, 0)),
            pl.BlockSpec((d, tf), lambda i, f: (0, f)),
            pl.BlockSpec((d, tf), lambda i, f: (0, f)),
            pl.BlockSpec((tf, d), lambda i, f: (f, 0)),
        ],
        out_specs=pl.BlockSpec((tm, d), lambda i, f: (i, 0)),
        out_shape=jax.ShapeDtypeStruct((m, d), F32),
        scratch_shapes=[pltpu.VMEM((tm, d), BF16), pltpu.VMEM((tm, d), F32)],
        compiler_params=pltpu.CompilerParams(
            dimension_semantics=("parallel", "arbitrary"), vmem_limit_bytes=VMEM_LIMIT),
        name="ffn",
    )(x2, g, wg, wu, wd)


def _proj_body(x_ref, pos_ref, g_ref, w_ref, gains_ref, bd_ref, freq_ref, fbias_ref, tri_ref,
               pq_ref, pk_ref, ones_q_ref, ones_k_ref,
               zb_ref, vtf_ref, vts_ref, vtc_ref, misct_ref, carry_scr):
    t = pl.program_id(1)
    tm = x_ref.shape[1]
    h = _rms_rows(x_ref[0], g_ref[...]).astype(BF16)

    lane = lax.broadcasted_iota(jnp.int32, (1, LANES), 1)
    d_in_head = lane % HEAD_DIM
    pos_cols = jnp.broadcast_to(pos_ref[0, 0].astype(F32), (LANES, tm)).T
    ang = pos_cols * freq_ref[...]
    cos = jnp.cos(ang)
    sin_signed = jnp.where(d_in_head < ROT_DIM // 2, -jnp.sin(ang), jnp.sin(ang))
    first_half = d_in_head < ROT_DIM // 2

    def rope(x):
        partner = jnp.where(first_half, pltpu.roll(x, LANES - ROT_DIM // 2, 1), pltpu.roll(x, ROT_DIM // 2, 1))
        return x * cos + partner * sin_signed

    def head_norm(z, c0, width):
        hi, lo = _split2(z * z)
        bd = bd_ref[:width, :width]
        ss = _dot(hi, bd) + _dot(lo, bd)
        return z * lax.rsqrt(ss * (1.0 / HEAD_DIM) + EPS) * gains_ref[:, c0:c0 + width]

    def group(c0, width, norm, rot):
        z = _dot(h, w_ref[:, c0:c0 + width])
        if norm:
            z = head_norm(z, c0, width)
        else:
            z = z * gains_ref[:, c0:c0 + width]
        if rot:
            z = jnp.concatenate([rope(z[:, c:c + LANES]) for c in range(0, width, LANES)], axis=1)
        return z

    def store_vt(vt_ref, c0, width):
        zt = _dot(h, w_ref[:, c0:c0 + width]).T
        for p in range(width // LANES):
            for c in range(tm // TK):
                vt_ref[0, p, c] = zt[p * LANES:(p + 1) * LANES, c * TK:(c + 1) * TK].astype(BF16)

    zm = _dot(h, w_ref[:, C_MISC:C_MISC + LANES])
    ki = rope(zm)
    zb_ref[0, :, Z_IK:Z_IK + LANES] = jnp.where(lane < IDX_DIM, ki, pltpu.roll(ki, IDX_DIM, 1)).astype(BF16)

    @pl.when(t == 0)
    def _():
        carry_scr[...] = jnp.zeros_like(carry_scr)

    logf = -_softplus(-(zm + fbias_ref[...]))
    hi, mid, lo = _split3(logf)
    tri = tri_ref[...]
    cum = _dot(tri, hi) + _dot(tri, mid) + _dot(tri, lo) + carry_scr[0:1, :]
    carry_scr[...] = jnp.broadcast_to(cum[tm - 1:tm, :], carry_scr.shape)

    is_forget = (lane >= L_FORGET) & (lane < L_FORGET + N_HEADS_FOX)
    is_widx = (lane >= L_WIDX) & (lane < L_WIDX + N_IDX_HEADS)
    misc = jnp.where(is_forget, cum, jnp.where(is_widx, zm * (N_IDX_HEADS ** -0.5), 0.0))
    misc_t = misc.T
    for c in range(tm // TQ):
        misct_ref[0, c] = misc_t[L_FORGET:L_FORGET + MISC_ROWS, c * TQ:(c + 1) * TQ]

    c_hi, c_mid, c_lo = _split3(cum)
    for c0, z0, p_ref, ones_ref in ((C_FQ, Z_FQ, pq_ref, ones_q_ref), (C_FK, Z_FK, pk_ref, ones_k_ref)):
        zn = group(c0, W_FOX, True, False).astype(BF16)
        src = jnp.concatenate([zn, c_hi, c_mid, c_lo], axis=1)
        zb_ref[0, :, z0:z0 + W_FOX_AUG] = (_dot(src, p_ref[...]) + ones_ref[...]).astype(BF16)

    zb_ref[0, :, Z_SQ:Z_SQ + W_SB] = group(C_SQ, W_SB, True, False).astype(BF16)
    zb_ref[0, :, Z_SK:Z_SK + W_SB] = group(C_SK, W_SB, True, False).astype(BF16)
    zb_ref[0, :, Z_CQ:Z_CQ + W_DSA] = group(C_CQ, W_DSA, True, True).astype(BF16)
    zb_ref[0, :, Z_CK:Z_CK + W_DSA] = group(C_CK, W_DSA, True, True).astype(BF16)
    zb_ref[0, :, Z_IQ:Z_IQ + W_IDX] = group(C_IQ, W_IDX, False, True).astype(BF16)
    store_vt(vtf_ref, C_FV, W_FOX)
    store_vt(vts_ref, C_SV, W_SB)
    store_vt(vtc_ref, C_CV, W_DSA)


def _proj(x, pos3, g, w, gains, consts, fbias):
    b, s, d = x.shape
    tm = min(TM_PROJ, s)
    const = lambda a: pl.BlockSpec(a.shape, lambda bi, ti: (0,) * a.ndim)
    vt_spec = lambda n: pl.BlockSpec((1, n, tm // TK, LANES, TK), lambda bi, ti: (bi, 0, ti, 0, 0))
    vt_shape = lambda n: jax.ShapeDtypeStruct((b, n, s // TK, LANES, TK), BF16)
    cs = (consts["bd"], consts["freq"], fbias, consts["tri_tok"], consts["pq"], consts["pk"],
          consts["ones_q"], consts["ones_k"])
    return pl.pallas_call(
        _proj_body,
        grid=(b, s // tm),
        in_specs=[
            pl.BlockSpec((1, tm, d), lambda bi, ti: (bi, ti, 0)),
            pl.BlockSpec((1, 1, 1, tm), lambda bi, ti: (bi, ti, 0, 0)),
            const(g), const(w), const(gains),
        ] + [const(a) for a in cs],
        out_specs=[
            pl.BlockSpec((1, tm, N_ZB), lambda bi, ti: (bi, ti, 0)),
            vt_spec(W_FOX // LANES), vt_spec(W_SB // LANES), vt_spec(W_DSA // LANES),
            pl.BlockSpec((1, tm // TQ, MISC_ROWS, TQ), lambda bi, ti: (bi, ti, 0, 0)),
        ],
        out_shape=[
            jax.ShapeDtypeStruct((b, s, N_ZB), BF16),
            vt_shape(W_FOX // LANES), vt_shape(W_SB // LANES), vt_shape(W_DSA // LANES),
            jax.ShapeDtypeStruct((b, s // TQ, MISC_ROWS, TQ), F32),
        ],
        scratch_shapes=[pltpu.VMEM((SUBLANES, LANES), F32)],
        compiler_params=pltpu.CompilerParams(
            dimension_semantics=("parallel", "arbitrary"), vmem_limit_bytes=VMEM_LIMIT),
        name="proj",
    )(x, pos3, g, w, gains, *cs)


def _head_masks():
    lane = lax.broadcasted_iota(jnp.int32, (1, LANES), 1)
    return lane < HEAD_DIM, lane >= HEAD_DIM


def _split_heads(q128):
    lo_half, hi_half = _head_masks()
    zero = jnp.zeros_like(q128)
    return jnp.where(lo_half, q128, zero), jnp.where(hi_half, q128, zero)


def _visible(tk, tq, strict):
    key = lax.broadcasted_iota(jnp.int32, (tk, tq), 0)
    qry = lax.broadcasted_iota(jnp.int32, (tk, tq), 1)
    return key < qry if strict else key <= qry


def _softmax_update(s, m, l):
    m_new = jnp.maximum(m, jnp.max(s, axis=0, keepdims=True))
    alpha = jnp.exp(m - m_new)
    p = jnp.exp(s - m_new)
    return p, alpha, m_new, alpha * l + jnp.sum(p, axis=0, keepdims=True)


def _head_rows(hd):
    return slice(hd * HEAD_DIM, (hd + 1) * HEAD_DIM)


def _fox_body(q_ref, k_ref, vt_ref, o_ref, acc_scr):
    i = pl.program_id(1)
    tq = q_ref.shape[1]
    nh = N_HEADS_FOX
    q_heads = [q_ref[0, :, hd * LANES:(hd + 1) * LANES] for hd in range(nh)]
    acc_scr[...] = jnp.zeros_like(acc_scr)

    def step(j, state, masked):
        kb = k_ref[0, pl.ds(pl.multiple_of(j * TK, TK), TK), :]
        scores = [_dot_nt(kb[:, hd * LANES:(hd + 1) * LANES], q_heads[hd]) for hd in range(nh)]
        if masked:
            vis = _visible(TK, tq, False)
            scores = [jnp.where(vis, s, -jnp.inf) for s in scores]
        upd = [_softmax_update(scores[hd], *state[hd]) for hd in range(nh)]
        for hd in range(nh):
            p, alpha = upd[hd][0], upd[hd][1]
            pr, rows = hd // 2, _head_rows(hd % 2)
            acc_scr[pr, rows, :] = alpha * acc_scr[pr, rows, :] + _dot(vt_ref[0, pr, j, rows, :], p.astype(BF16))
        return tuple((u[2], u[3]) for u in upd)

    init = tuple((jnp.full((1, tq), -jnp.inf, F32), jnp.zeros((1, tq), F32)) for _ in range(nh))
    state = lax.fori_loop(0, i, lambda j, st: step(j, st, False), init)
    state = step(i, state, True)
    for pr in range(nh // 2):
        inv = jnp.concatenate([jnp.broadcast_to(1.0 / state[2 * pr + hd][1], (HEAD_DIM, tq)) for hd in range(2)],
                              axis=0)
        o_ref[0, :, pr * LANES:(pr + 1) * LANES] = (acc_scr[pr] * inv).T.astype(BF16)


def _fox(zb, vt):
    b, s, _ = zb.shape
    nq, nk = s // TQ, s // TK
    n_pairs = W_FOX // LANES
    return pl.pallas_call(
        _fox_body,
        grid=(b, nq),
        in_specs=[
            pl.BlockSpec((1, TQ, W_FOX_AUG), lambda bi, i: (bi, i, Z_FQ // W_FOX_AUG)),
            pl.BlockSpec((1, s, W_FOX_AUG), lambda bi, i: (bi, 0, Z_FK // W_FOX_AUG)),
            pl.BlockSpec((1, n_pairs, nk, LANES, TK), lambda bi, i: (bi, 0, 0, 0, 0)),
        ],
        out_specs=pl.BlockSpec((1, TQ, W_FOX), lambda bi, i: (bi, i, 0)),
        out_shape=jax.ShapeDtypeStruct((b, s, W_FOX), BF16),
        scratch_shapes=[pltpu.VMEM((n_pairs, LANES, TQ), F32)],
        compiler_params=pltpu.CompilerParams(
            dimension_semantics=("parallel", "arbitrary"), vmem_limit_bytes=VMEM_LIMIT),
        name="fox",
    )(zb, zb, vt)


def _sb_body(q_ref, k_ref, vt_ref, tri_ref, o_ref, acc_scr):
    i = pl.program_id(1)
    tq = q_ref.shape[1]
    nh = N_HEADS_SB
    q_heads = []
    for pr in range(nh // 2):
        q_heads.extend(_split_heads(q_ref[0, :, pr * LANES:(pr + 1) * LANES]))
    acc_scr[...] = jnp.zeros_like(acc_scr)

    def step(j, carry, masked):
        kb = k_ref[0, pl.ds(pl.multiple_of(j * TK, TK), TK), :]
        tri = tri_ref[...]
        zs = [_dot_nt(kb[:, (hd // 2) * LANES:(hd // 2 + 1) * LANES], q_heads[hd]) for hd in range(nh)]
        loms = [-_softplus(z) for z in zs]
        if masked:
            strict = _visible(TK, tq, True)
            loms = [jnp.where(strict, lom, 0.0) for lom in loms]
        splits = [_split2(lom) for lom in loms]
        suffixes = [_dot(tri, hi) + _dot(tri, lo) for hi, lo in splits]
        new_carry = []
        for hd in range(nh):
            a = jnp.exp(zs[hd] + suffixes[hd] + carry[hd])
            if masked:
                a = jnp.where(strict, a, 0.0)
            pr, rows = hd // 2, _head_rows(hd % 2)
            acc_scr[pr, rows, :] = acc_scr[pr, rows, :] + _dot(vt_ref[0, pr, j, rows, :], a.astype(BF16))
            new_carry.append(carry[hd] + suffixes[hd][0:1, :])
        return tuple(new_carry)

    carry = step(i, tuple(jnp.zeros((1, tq), F32) for _ in range(nh)), True)

    def live(state):
        n, carry = state
        worst = functools.reduce(jnp.maximum, carry)
        return jnp.logical_and(n < i, jnp.max(worst) > EXP_UNDERFLOW)

    def body(state):
        n, carry = state
        return n + 1, step(i - 1 - n, carry, False)

    lax.while_loop(live, body, (jnp.int32(0), carry))
    for pr in range(nh // 2):
        o_ref[0, :, pr * LANES:(pr + 1) * LANES] = acc_scr[pr].T.astype(BF16)


def _sb(zb, vt, tri_ge):
    b, s, _ = zb.shape
    nq, nk = s // TQ, s // TK
    n_pairs = W_SB // LANES
    return pl.pallas_call(
        _sb_body,
        grid=(b, nq),
        in_specs=[
            pl.BlockSpec((1, TQ, W_SB), lambda bi, i: (bi, i, Z_SQ // W_SB)),
            pl.BlockSpec((1, s, W_SB), lambda bi, i: (bi, 0, Z_SK // W_SB)),
            pl.BlockSpec((1, n_pairs, nk, LANES, TK), lambda bi, i: (bi, 0, 0, 0, 0)),
            pl.BlockSpec((TK, TK), lambda bi, i: (0, 0)),
        ],
        out_specs=pl.BlockSpec((1, TQ, W_SB), lambda bi, i: (bi, i, 0)),
        out_shape=jax.ShapeDtypeStruct((b, s, W_SB), BF16),
        scratch_shapes=[pltpu.VMEM((n_pairs, LANES, TQ), F32)],
        compiler_params=pltpu.CompilerParams(
            dimension_semantics=("parallel", "arbitrary"), vmem_limit_bytes=VMEM_LIMIT),
        name="sb",
    )(zb, zb, vt, tri_ge)


def _dsa_body(iq_ref, ik_ref, q_ref, k_ref, vt_ref, misct_ref, tri_ref, o_ref,
              keys_scr, bias_scr, acc_scr, *, topk):
    i = pl.program_id(1)
    tq = q_ref.shape[1]
    n_pairs = N_HEADS_DSA // 2
    int_min = jnp.int32(-2 ** 31)

    iq = iq_ref[0]
    iq_heads = _split_heads(iq[:, 0:LANES]) + _split_heads(iq[:, LANES:2 * LANES])
    w_rows = [misct_ref[0, 0, L_WIDX - L_FORGET + hh:L_WIDX - L_FORGET + hh + 1, :] for hh in range(N_IDX_HEADS)]

    def score_step(j, masked):
        kb = ik_ref[0, pl.ds(pl.multiple_of(j * TK, TK), TK), :]
        sc = jnp.zeros((TK, tq), F32)
        for hh in range(N_IDX_HEADS):
            sc = sc + w_rows[hh] * jnp.maximum(_dot_nt(kb, iq_heads[hh]), 0.0)
        if masked:
            sc = jnp.where(_visible(TK, tq, False), sc, -jnp.inf)
        bits = lax.bitcast_convert_type(sc, jnp.int32)
        keys_scr[j] = jnp.where(bits < 0, bits ^ jnp.int32(0x7FFFFFFF), bits)

    def score_body(j, c):
        score_step(j, False)
        return c

    lax.fori_loop(0, i, score_body, 0)
    score_step(i, True)

    def count_ge(cand):
        def body(j, acc):
            c = jnp.where(keys_scr[j] >= cand, 1.0, 0.0)
            return acc + jnp.sum(c.reshape(TK // SUBLANES, SUBLANES, tq), axis=0)
        acc = lax.fori_loop(0, i + 1, body, jnp.zeros((SUBLANES, tq), F32))
        return jnp.sum(acc, axis=0, keepdims=True)

    kf = jnp.float32(topk)

    def bisect(it, thr):
        cand = thr + lax.shift_left(jnp.int32(1), 31 - it)
        return jnp.where(count_ge(cand) >= kf, cand, thr)

    thr = lax.fori_loop(0, 32, bisect, jnp.full((1, tq), int_min, jnp.int32))
    need = kf - count_ge(thr + 1)

    def bias_step(j, run, masked):
        kblk = keys_scr[j]
        eq = kblk == thr
        eqf = jnp.where(eq, 1.0, 0.0)
        before = _dot(tri_ref[...], eqf.astype(BF16)) + run
        sel = (kblk > thr) | (eq & (before < need))
        if masked:
            sel = sel & _visible(TK, tq, False)
        bias_scr[j] = jnp.where(sel, 0.0, NEG_BIG)
        return run + jnp.sum(eqf, axis=0, keepdims=True)

    run = lax.fori_loop(0, i, lambda j, r: bias_step(j, r, False), jnp.zeros((1, tq), F32))
    bias_step(i, run, True)

    q = q_ref[0]
    q_heads = _split_heads(q[:, 0:LANES]) + _split_heads(q[:, LANES:2 * LANES])
    acc_scr[...] = jnp.zeros_like(acc_scr)

    def attn_step(j, state):
        kb = k_ref[0, pl.ds(pl.multiple_of(j * TK, TK), TK), :]
        bias = bias_scr[j]
        scores = [_dot_nt(kb[:, (hh // 2) * LANES:(hh // 2 + 1) * LANES], q_heads[hh]) + bias
                  for hh in range(N_HEADS_DSA)]
        upd = [_softmax_update(scores[hh], *state[hh]) for hh in range(N_HEADS_DSA)]
        for hh in range(N_HEADS_DSA):
            p, alpha = upd[hh][0], upd[hh][1]
            pr, rows = hh // 2, _head_rows(hh % 2)
            acc_scr[pr, rows, :] = alpha * acc_scr[pr, rows, :] + _dot(vt_ref[0, pr, j, rows, :], p.astype(BF16))
        return tuple((u[2], u[3]) for u in upd)

    init = tuple((jnp.full((1, tq), -jnp.inf, F32), jnp.zeros((1, tq), F32)) for _ in range(N_HEADS_DSA))
    state = lax.fori_loop(0, i + 1, attn_step, init)
    outs = []
    for pr in range(n_pairs):
        inv = jnp.concatenate([jnp.broadcast_to(1.0 / state[2 * pr + hd][1], (HEAD_DIM, tq)) for hd in range(2)],
                              axis=0)
        outs.append((acc_scr[pr] * inv).T)
    o_ref[0] = jnp.concatenate(outs, axis=1).astype(BF16)


def _dsa(zb, vt, misct, tri_lt, topk):
    b, s, _ = zb.shape
    nq, nk = s // TQ, s // TK
    n_pairs = W_DSA // LANES
    return pl.pallas_call(
        functools.partial(_dsa_body, topk=topk),
        grid=(b, nq),
        in_specs=[
            pl.BlockSpec((1, TQ, W_IDX), lambda bi, i: (bi, i, Z_IQ // W_IDX)),
            pl.BlockSpec((1, s, LANES), lambda bi, i: (bi, 0, Z_IK // LANES)),
            pl.BlockSpec((1, TQ, W_DSA), lambda bi, i: (bi, i, Z_CQ // W_DSA)),
            pl.BlockSpec((1, s, W_DSA), lambda bi, i: (bi, 0, Z_CK // W_DSA)),
            pl.BlockSpec((1, n_pairs, nk, LANES, TK), lambda bi, i: (bi, 0, 0, 0, 0)),
            pl.BlockSpec((1, 1, MISC_ROWS, TQ), lambda bi, i: (bi, i, 0, 0)),
            pl.BlockSpec((TK, TK), lambda bi, i: (0, 0)),
        ],
        out_specs=pl.BlockSpec((1, TQ, W_DSA), lambda bi, i: (bi, i, 0)),
        out_shape=jax.ShapeDtypeStruct((b, s, W_DSA), BF16),
        scratch_shapes=[
            pltpu.VMEM((nk, TK, TQ), jnp.int32),
            pltpu.VMEM((nk, TK, TQ), F32),
            pltpu.VMEM((n_pairs, LANES, TQ), F32),
        ],
        compiler_params=pltpu.CompilerParams(
            dimension_semantics=("parallel", "arbitrary"), vmem_limit_bytes=VMEM_LIMIT),
        name="dsa",
    )(zb, zb, zb, zb, vt, misct, tri_lt)


def _merge_body(x_ref, g_ref, of_ref, os_ref, oc_ref, wg_ref, bg_ref, wf_ref, ws_ref, wc_ref, wo_ref, o_ref):
    x = x_ref[...]
    d = x.shape[1]
    h = _rms_rows(x, g_ref[...]).astype(BF16)
    merged = jnp.zeros_like(x)
    for n, (o_br, w_br) in enumerate(((of_ref, wf_ref), (os_ref, ws_ref), (oc_ref, wc_ref))):
        gate = _dot(h, wg_ref[:, n * d:(n + 1) * d]) + bg_ref[n:n + 1, :]
        gate = 1.0 / (1.0 + jnp.exp(-gate))
        merged = merged + gate * _dot(o_br[...], w_br[...])
    o_ref[...] = x + _dot(merged.astype(BF16), wo_ref[...])


def _merge(x2, g, o_f, o_s, o_c, w_gates, b_gates, w_f, w_s, w_c, w_o):
    m, d = x2.shape
    tm = min(TM_MERGE, m)
    row = lambda w: pl.BlockSpec((tm, w), lambda i: (i, 0))
    const = lambda a: pl.BlockSpec(a.shape, lambda i: (0, 0))
    return pl.pallas_call(
        _merge_body,
        grid=(m // tm,),
        in_specs=[row(d), const(g), row(W_FOX), row(W_SB), row(W_DSA), const(w_gates), const(b_gates),
                  const(w_f), const(w_s), const(w_c), const(w_o)],
        out_specs=row(d),
        out_shape=jax.ShapeDtypeStruct((m, d), F32),
        compiler_params=pltpu.CompilerParams(
            dimension_semantics=("parallel",), vmem_limit_bytes=VMEM_LIMIT),
        name="merge",
    )(x2, g, o_f, o_s, o_c, w_gates, b_gates, w_f, w_s, w_c, w_o)


def _tile_heads(g, n):
    return jnp.tile(g.astype(F32), n)


def _constants(tm_proj):
    r = np.arange(W_FOX)
    bd = (r[:, None] // HEAD_DIM == r[None, :] // HEAD_DIM).astype(np.float32)
    rt = np.arange(tm_proj)
    tri_tok = (rt[None, :] <= rt[:, None]).astype(np.float32)
    rk = np.arange(TK)
    tri_ge = (rk[None, :] >= rk[:, None]).astype(np.float32)
    tri_lt = (rk[None, :] < rk[:, None]).astype(np.float32)

    n_src = W_FOX + N_BIAS_FEATS * LANES
    pq = np.zeros((n_src, W_FOX_AUG), np.float32)
    pk = np.zeros((n_src, W_FOX_AUG), np.float32)
    ones_q = np.zeros((1, W_FOX_AUG), np.float32)
    ones_k = np.zeros((1, W_FOX_AUG), np.float32)
    for hd in range(N_HEADS_FOX):
        for dd in range(HEAD_DIM):
            pq[hd * HEAD_DIM + dd, hd * LANES + dd] = 1.0
            pk[hd * HEAD_DIM + dd, hd * LANES + dd] = 1.0
        for part in range(N_BIAS_FEATS):
            src_row = W_FOX + part * LANES + L_FORGET + hd
            pq[src_row, hd * LANES + HEAD_DIM + part] = 1.0
            ones_q[0, hd * LANES + HEAD_DIM + N_BIAS_FEATS + part] = 1.0
            ones_k[0, hd * LANES + HEAD_DIM + part] = 1.0
            pk[src_row, hd * LANES + HEAD_DIM + N_BIAS_FEATS + part] = -1.0
    half = ROT_DIM // 2
    inv_freq = jnp.power(ROPE_THETA, -jnp.arange(half, dtype=F32) * 2.0 / ROT_DIM)
    dlane = np.arange(LANES) % HEAD_DIM
    freq = jnp.where(jnp.asarray(dlane < ROT_DIM), inv_freq[jnp.asarray(dlane % half)], 0.0)
    return dict(bd=jnp.asarray(bd, BF16), tri_tok=jnp.asarray(tri_tok, BF16), tri_ge=jnp.asarray(tri_ge, BF16),
                tri_lt=jnp.asarray(tri_lt, BF16), pq=jnp.asarray(pq, BF16), pk=jnp.asarray(pk, BF16),
                ones_q=jnp.asarray(ones_q, F32), ones_k=jnp.asarray(ones_k, F32),
                freq=freq.reshape(1, LANES).astype(F32))


def kernel(x, positions, ffn1_norm, ffn1_w_gate, ffn1_w_up, ffn1_w_down, mix_norm, w_in, b_forget, b_gates, q_norm_fox, k_norm_fox, q_norm_sb, k_norm_sb, q_norm_dsa, k_norm_dsa, w_branch_fox, w_branch_sb, w_branch_dsa, w_out, ffn2_norm, ffn2_w_gate, ffn2_w_up, ffn2_w_down):
    b, s, d = x.shape
    depth = w_in.shape[0]
    topk = min(TOPK_MAX, s // 4)
    assert s % TQ == 0 and s % TK == 0 and TQ == TK and d % LANES == 0
    consts = _constants(min(TM_PROJ, s))
    tm_proj = min(TM_PROJ, s)
    pos3 = positions.reshape(b, s // tm_proj, 1, tm_proj)
    scale = HEAD_DIM ** -0.5
    idx_scale = IDX_DIM ** -0.5

    splits = (W_FOX, W_FOX, W_FOX, N_HEADS_FOX, W_SB, W_SB, W_SB, W_DSA, W_DSA, W_DSA,
              W_IDX, IDX_DIM, N_IDX_HEADS, d, d, d)
    offs = np.concatenate([[0], np.cumsum(splits)]).tolist()
    (o_qf, o_kf, o_vf, o_ff, o_qs, o_ks, o_vs, o_qc, o_kc, o_vc, o_qi, o_ki, o_wi, o_ga) = offs[:14]

    x2 = x.reshape(b * s, d)
    for l in range(depth):
        wl = w_in[l]
        pad = jnp.zeros((d, LANES - IDX_DIM - N_HEADS_FOX - N_IDX_HEADS), F32)
        w_proj = jnp.concatenate([
            wl[:, o_qf:o_ff], wl[:, o_qs:o_qi], wl[:, o_qi:o_ki],
            wl[:, o_ki:o_wi], wl[:, o_ff:o_qs], wl[:, o_wi:o_ga], pad], axis=1).astype(BF16)
        w_gates = wl[:, o_ga:].astype(BF16)
        ones = lambda n: jnp.ones((n,), F32)
        gains = jnp.concatenate([
            _tile_heads(q_norm_fox[l], N_HEADS_FOX) * scale, _tile_heads(k_norm_fox[l], N_HEADS_FOX), ones(W_FOX),
            _tile_heads(q_norm_sb[l], N_HEADS_SB) * scale, _tile_heads(k_norm_sb[l], N_HEADS_SB), ones(W_SB),
            _tile_heads(q_norm_dsa[l], N_HEADS_DSA) * scale, _tile_heads(k_norm_dsa[l], N_HEADS_DSA), ones(W_DSA),
            ones(W_IDX) * idx_scale, ones(LANES)]).reshape(1, N_PROJ)
        fbias = jnp.zeros((LANES,), F32).at[L_FORGET:L_FORGET + N_HEADS_FOX].set(b_forget[l].astype(F32))
        fbias = fbias.reshape(1, LANES)

        x2 = _ffn(x2, ffn1_norm[l].reshape(1, d), ffn1_w_gate[l].astype(BF16), ffn1_w_up[l].astype(BF16),
                  ffn1_w_down[l].astype(BF16))
        zb, vt_f, vt_s, vt_c, misct = _proj(x2.reshape(b, s, d), pos3, mix_norm[l].reshape(1, d), w_proj, gains,
                                            consts, fbias)
        o_f = _fox(zb, vt_f)
        o_s = _sb(zb, vt_s, consts["tri_ge"])
        o_c = _dsa(zb, vt_c, misct, consts["tri_lt"], topk)
        x2 = _merge(x2, mix_norm[l].reshape(1, d), o_f.reshape(b * s, W_FOX), o_s.reshape(b * s, W_SB),
                    o_c.reshape(b * s, W_DSA), w_gates, b_gates[l].astype(F32),
                    w_branch_fox[l].astype(BF16), w_branch_sb[l].astype(BF16), w_branch_dsa[l].astype(BF16),
                    w_out[l].astype(BF16))
        x2 = _ffn(x2, ffn2_norm[l].reshape(1, d), ffn2_w_gate[l].astype(BF16), ffn2_w_up[l].astype(BF16),
                  ffn2_w_down[l].astype(BF16))
    return x2.reshape(b, s, d)
```

```python
import functools

import jax
import jax.numpy as jnp
import numpy as np
from jax import lax
from jax.experimental import pallas as pl
from jax.experimental.pallas import tpu as pltpu

F32 = jnp.float32
BF16 = jnp.bfloat16

HEAD_DIM = 64
N_HEADS_FOX = 6
N_HEADS_SB = 6
N_HEADS_DSA = 4
N_IDX_HEADS = 4
IDX_DIM = 64
TOPK_MAX = 256
ROPE_THETA = 500000.0
ROT_DIM = HEAD_DIM // 4
EPS = 1e-6

LANES = 128
SUBLANES = 8
PACK16 = 2 * SUBLANES
W_FOX = N_HEADS_FOX * HEAD_DIM
W_SB = N_HEADS_SB * HEAD_DIM
W_DSA = N_HEADS_DSA * HEAD_DIM
W_IDX = N_IDX_HEADS * IDX_DIM

C_FQ, C_FK, C_FV = 0, W_FOX, 2 * W_FOX
C_SQ, C_SK, C_SV = 3 * W_FOX, 3 * W_FOX + W_SB, 3 * W_FOX + 2 * W_SB
C_CQ = 3 * W_FOX + 3 * W_SB
C_CK, C_CV = C_CQ + W_DSA, C_CQ + 2 * W_DSA
C_IQ = C_CQ + 3 * W_DSA
C_MISC = C_IQ + W_IDX
N_PROJ = C_MISC + LANES
L_FORGET = IDX_DIM
L_WIDX = IDX_DIM + N_HEADS_FOX
MISC_ROWS = 16

W_FOX_AUG = N_HEADS_FOX * LANES
Z_FQ, Z_FK = 0, W_FOX_AUG
Z_SQ, Z_SK = 2 * W_FOX_AUG, 2 * W_FOX_AUG + W_SB
Z_CQ = 2 * W_FOX_AUG + 2 * W_SB
Z_CK = Z_CQ + W_DSA
Z_IQ = Z_CK + W_DSA
Z_IK = Z_IQ + W_IDX
N_ZB = Z_IK + LANES
N_BIAS_FEATS = 3

TQ = 256
TK = 256
TM_PROJ = 512
TM_FFN = 512
TM_MERGE = 512
NEG_BIG = -1e30
EXP_UNDERFLOW = -110.0
VMEM_LIMIT = 56 * 1024 * 1024

_NT = (((1,), (1,)), ((), ()))


def _dot(a, b):
    return jnp.dot(a, b, preferred_element_type=F32)


def _dot_nt(a, b):
    return lax.dot_general(a, b, _NT, preferred_element_type=F32)


def _split2(x):
    hi = x.astype(BF16)
    lo = (x - hi.astype(F32)).astype(BF16)
    return hi, lo


def _split3(x):
    hi = x.astype(BF16)
    r = x - hi.astype(F32)
    mid = r.astype(BF16)
    lo = (r - mid.astype(F32)).astype(BF16)
    return hi, mid, lo


def _rms_rows(x, g):
    ms = jnp.mean(x * x, axis=-1, keepdims=True)
    return x * lax.rsqrt(ms + EPS) * g


def _softplus(z):
    return jnp.maximum(z, 0.0) + jnp.log(1.0 + jnp.exp(-jnp.abs(z)))


def _ffn_body(x_ref, g_ref, wg_ref, wu_ref, wd_ref, o_ref, h_scr, acc_scr):
    f = pl.program_id(1)

    @pl.when(f == 0)
    def _():
        h_scr[...] = _rms_rows(x_ref[...], g_ref[...]).astype(BF16)
        acc_scr[...] = jnp.zeros_like(acc_scr)

    h = h_scr[...]
    a = _dot(h, wg_ref[...])
    u = _dot(h, wu_ref[...])
    p = (a * (1.0 / (1.0 + jnp.exp(-a))) * u).astype(BF16)
    acc_scr[...] += _dot(p, wd_ref[...])

    @pl.when(f == pl.num_programs(1) - 1)
    def _():
        o_ref[...] = x_ref[...] + 0.5 * acc_scr[...]


def _ffn_tile_f(d_ff):
    best = LANES
    for t in range(LANES, d_ff + 1, LANES):
        if d_ff % t == 0 and t <= 1536:
            best = t
    return best


def _ffn(x2, g, wg, wu, wd):
    m, d = x2.shape
    d_ff = wg.shape[1]
    tm = min(TM_FFN, m)
    tf = _ffn_tile_f(d_ff)
    return pl.pallas_call(
        _ffn_body,
        grid=(m // tm, d_ff // tf),
        in_specs=[
            pl.BlockSpec((tm, d), lambda i, f: (i, 0)),
            pl.BlockSpec((1, d), lambda i, f: (0, 0)),
            pl.BlockSpec((d, tf), lambda i, f: (0, f)),
            pl.BlockSpec((d, tf), lambda i, f: (0, f)),
            pl.BlockSpec((tf, d), lambda i, f: (f, 0)),
        ],
        out_specs=pl.BlockSpec((tm, d), lambda i, f: (i, 0)),
        out_shape=jax.ShapeDtypeStruct((m, d), F32),
        scratch_shapes=[pltpu.VMEM((tm, d), BF16), pltpu.VMEM((tm, d), F32)],
        compiler_params=pltpu.CompilerParams(
            dimension_semantics=("parallel", "arbitrary"), vmem_limit_bytes=VMEM_LIMIT),
        name="ffn",
    )(x2, g, wg, wu, wd)


def _proj_body(x_ref, pos_ref, g_ref, w_ref, gains_ref, bd_ref, freq_ref, fbias_ref, tri_ref,
               pq_ref, pk_ref, ones_q_ref, ones_k_ref,
               zb_ref, vtf_ref, vts_ref, vtc_ref, misct_ref, carry_scr):
    t = pl.program_id(1)
    tm = x_ref.shape[1]
    h = _rms_rows(x_ref[0], g_ref[...]).astype(BF16)

    lane = lax.broadcasted_iota(jnp.int32, (1, LANES), 1)
    d_in_head = lane % HEAD_DIM
    pos_cols = jnp.broadcast_to(pos_ref[0, 0].astype(F32), (LANES, tm)).T
    ang = pos_cols * freq_ref[...]
    cos = jnp.cos(ang)
    sin_signed = jnp.where(d_in_head < ROT_DIM // 2, -jnp.sin(ang), jnp.sin(ang))
    first_half = d_in_head < ROT_DIM // 2

    def rope(x):
        partner = jnp.where(first_half, pltpu.roll(x, LANES - ROT_DIM // 2, 1), pltpu.roll(x, ROT_DIM // 2, 1))
        return x * cos + partner * sin_signed

    def head_norm(z, c0, width):
        hi, lo = _split2(z * z)
        bd = bd_ref[:width, :width]
        ss = _dot(hi, bd) + _dot(lo, bd)
        return z * lax.rsqrt(ss * (1.0 / HEAD_DIM) + EPS) * gains_ref[:, c0:c0 + width]

    def group(c0, width, norm, rot):
        z = _dot(h, w_ref[:, c0:c0 + width])
        if norm:
            z = head_norm(z, c0, width)
        else:
            z = z * gains_ref[:, c0:c0 + width]
        if rot:
            z = jnp.concatenate([rope(z[:, c:c + LANES]) for c in range(0, width, LANES)], axis=1)
        return z

    def store_vt(vt_ref, c0, width):
        zt = _dot(h, w_ref[:, c0:c0 + width]).T
        for p in range(width // LANES):
            for c in range(tm // TK):
                vt_ref[0, p, c] = zt[p * LANES:(p + 1) * LANES, c * TK:(c + 1) * TK].astype(BF16)

    zm = _dot(h, w_ref[:, C_MISC:C_MISC + LANES])
    ki = rope(zm)
    zb_ref[0, :, Z_IK:Z_IK + LANES] = jnp.where(lane < IDX_DIM, ki, pltpu.roll(ki, IDX_DIM, 1)).astype(BF16)

    @pl.when(t == 0)
    def _():
        carry_scr[...] = jnp.zeros_like(carry_scr)

    logf = -_softplus(-(zm + fbias_ref[...]))
    hi, mid, lo = _split3(logf)
    tri = tri_ref[...]
    cum = _dot(tri, hi) + _dot(tri, mid) + _dot(tri, lo) + carry_scr[0:1, :]
    carry_scr[...] = jnp.broadcast_to(cum[tm - 1:tm, :], carry_scr.shape)

    is_forget = (lane >= L_FORGET) & (lane < L_FORGET + N_HEADS_FOX)
    is_widx = (lane >= L_WIDX) & (lane < L_WIDX + N_IDX_HEADS)
    misc = jnp.where(is_forget, cum, jnp.where(is_widx, zm * (N_IDX_HEADS ** -0.5), 0.0))
    misc_t = misc.T
    for c in range(tm // TQ):
        misct_ref[0, c] = misc_t[L_FORGET:L_FORGET + MISC_ROWS, c * TQ:(c + 1) * TQ]

    c_hi, c_mid, c_lo = _split3(cum)
    for c0, z0, p_ref, ones_ref in ((C_FQ, Z_FQ, pq_ref, ones_q_ref), (C_FK, Z_FK, pk_ref, ones_k_ref)):
        zn = group(c0, W_FOX, True, False).astype(BF16)
        src = jnp.concatenate([zn, c_hi, c_mid, c_lo], axis=1)
        zb_ref[0, :, z0:z0 + W_FOX_AUG] = (_dot(src, p_ref[...]) + ones_ref[...]).astype(BF16)

    zb_ref[0, :, Z_SQ:Z_SQ + W_SB] = group(C_SQ, W_SB, True, False).astype(BF16)
    zb_ref[0, :, Z_SK:Z_SK + W_SB] = group(C_SK, W_SB, True, False).astype(BF16)
    zb_ref[0, :, Z_CQ:Z_CQ + W_DSA] = group(C_CQ, W_DSA, True, True).astype(BF16)
    zb_ref[0, :, Z_CK:Z_CK + W_DSA] = group(C_CK, W_DSA, True, True).astype(BF16)
    zb_ref[0, :, Z_IQ:Z_IQ + W_IDX] = group(C_IQ, W_IDX, False, True).astype(BF16)
    store_vt(vtf_ref, C_FV, W_FOX)
    store_vt(vts_ref, C_SV, W_SB)
    store_vt(vtc_ref, C_CV, W_DSA)


def _proj(x, pos3, g, w, gains, consts, fbias):
    b, s, d = x.shape
    tm = min(TM_PROJ, s)
    const = lambda a: pl.BlockSpec(a.shape, lambda bi, ti: (0,) * a.ndim)
    vt_spec = lambda n: pl.BlockSpec((1, n, tm // TK, LANES, TK), lambda bi, ti: (bi, 0, ti, 0, 0))
    vt_shape = lambda n: jax.ShapeDtypeStruct((b, n, s // TK, LANES, TK), BF16)
    cs = (consts["bd"], consts["freq"], fbias, consts["tri_tok"], consts["pq"], consts["pk"],
          consts["ones_q"], consts["ones_k"])
    return pl.pallas_call(
        _proj_body,
        grid=(b, s // tm),
        in_specs=[
            pl.BlockSpec((1, tm, d), lambda bi, ti: (bi, ti, 0)),
            pl.BlockSpec((1, 1, 1, tm), lambda bi, ti: (bi, ti, 0, 0)),
            const(g), const(w), const(gains),
        ] + [const(a) for a in cs],
        out_specs=[
            pl.BlockSpec((1, tm, N_ZB), lambda bi, ti: (bi, ti, 0)),
            vt_spec(W_FOX // LANES), vt_spec(W_SB // LANES), vt_spec(W_DSA // LANES),
            pl.BlockSpec((1, tm // TQ, MISC_ROWS, TQ), lambda bi, ti: (bi, ti, 0, 0)),
        ],
        out_shape=[
            jax.ShapeDtypeStruct((b, s, N_ZB), BF16),
            vt_shape(W_FOX // LANES), vt_shape(W_SB // LANES), vt_shape(W_DSA // LANES),
            jax.ShapeDtypeStruct((b, s // TQ, MISC_ROWS, TQ), F32),
        ],
        scratch_shapes=[pltpu.VMEM((SUBLANES, LANES), F32)],
        compiler_params=pltpu.CompilerParams(
            dimension_semantics=("parallel", "arbitrary"), vmem_limit_bytes=VMEM_LIMIT),
        name="proj",
    )(x, pos3, g, w, gains, *cs)


def _head_masks():
    lane = lax.broadcasted_iota(jnp.int32, (1, LANES), 1)
    return lane < HEAD_DIM, lane >= HEAD_DIM


def _split_heads(q128):
    lo_half, hi_half = _head_masks()
    zero = jnp.zeros_like(q128)
    return jnp.where(lo_half, q128, zero), jnp.where(hi_half, q128, zero)


def _visible(tk, tq, strict):
    key = lax.broadcasted_iota(jnp.int32, (tk, tq), 0)
    qry = lax.broadcasted_iota(jnp.int32, (tk, tq), 1)
    return key < qry if strict else key <= qry


def _softmax_update(s, m, l):
    m_new = jnp.maximum(m, jnp.max(s, axis=0, keepdims=True))
    alpha = jnp.exp(m - m_new)
    p = jnp.exp(s - m_new)
    return p, alpha, m_new, alpha * l + jnp.sum(p, axis=0, keepdims=True)


def _head_rows(hd):
    return slice(hd * HEAD_DIM, (hd + 1) * HEAD_DIM)


def _fox_body(q_ref, k_ref, vt_ref, o_ref, acc_scr):
    i = pl.program_id(1)
    tq = q_ref.shape[1]
    nh = N_HEADS_FOX
    q_heads = [q_ref[0, :, hd * LANES:(hd + 1) * LANES] for hd in range(nh)]
    acc_scr[...] = jnp.zeros_like(acc_scr)

    def step(j, state, masked):
        kb = k_ref[0, pl.ds(pl.multiple_of(j * TK, TK), TK), :]
        scores = [_dot_nt(kb[:, hd * LANES:(hd + 1) * LANES], q_heads[hd]) for hd in range(nh)]
        if masked:
            vis = _visible(TK, tq, False)
            scores = [jnp.where(vis, s, -jnp.inf) for s in scores]
        upd = [_softmax_update(scores[hd], *state[hd]) for hd in range(nh)]
        for hd in range(nh):
            p, alpha = upd[hd][0], upd[hd][1]
            pr, rows = hd // 2, _head_rows(hd % 2)
            acc_scr[pr, rows, :] = alpha * acc_scr[pr, rows, :] + _dot(vt_ref[0, pr, j, rows, :], p.astype(BF16))
        return tuple((u[2], u[3]) for u in upd)

    init = tuple((jnp.full((1, tq), -jnp.inf, F32), jnp.zeros((1, tq), F32)) for _ in range(nh))
    state = lax.fori_loop(0, i, lambda j, st: step(j, st, False), init)
    state = step(i, state, True)
    for pr in range(nh // 2):
        inv = jnp.concatenate([jnp.broadcast_to(1.0 / state[2 * pr + hd][1], (HEAD_DIM, tq)) for hd in range(2)],
                              axis=0)
        o_ref[0, :, pr * LANES:(pr + 1) * LANES] = (acc_scr[pr] * inv).T.astype(BF16)


def _fox(zb, vt):
    b, s, _ = zb.shape
    nq, nk = s // TQ, s // TK
    n_pairs = W_FOX // LANES
    return pl.pallas_call(
        _fox_body,
        grid=(b, nq),
        in_specs=[
            pl.BlockSpec((1, TQ, W_FOX_AUG), lambda bi, i: (bi, i, Z_FQ // W_FOX_AUG)),
            pl.BlockSpec((1, s, W_FOX_AUG), lambda bi, i: (bi, 0, Z_FK // W_FOX_AUG)),
            pl.BlockSpec((1, n_pairs, nk, LANES, TK), lambda bi, i: (bi, 0, 0, 0, 0)),
        ],
        out_specs=pl.BlockSpec((1, TQ, W_FOX), lambda bi, i: (bi, i, 0)),
        out_shape=jax.ShapeDtypeStruct((b, s, W_FOX), BF16),
        scratch_shapes=[pltpu.VMEM((n_pairs, LANES, TQ), F32)],
        compiler_params=pltpu.CompilerParams(
            dimension_semantics=("parallel", "arbitrary"), vmem_limit_bytes=VMEM_LIMIT),
        name="fox",
    )(zb, zb, vt)


def _sb_body(q_ref, k_ref, vt_ref, tri_ref, o_ref, acc_scr):
    i = pl.program_id(1)
    tq = q_ref.shape[1]
    nh = N_HEADS_SB
    q_heads = []
    for pr in range(nh // 2):
        q_heads.extend(_split_heads(q_ref[0, :, pr * LANES:(pr + 1) * LANES]))
    acc_scr[...] = jnp.zeros_like(acc_scr)

    def step(j, carry, masked):
        kb = k_ref[0, pl.ds(pl.multiple_of(j * TK, TK), TK), :]
        tri = tri_ref[...]
        zs = [_dot_nt(kb[:, (hd // 2) * LANES:(hd // 2 + 1) * LANES], q_heads[hd]) for hd in range(nh)]
        loms = [-_softplus(z) for z in zs]
        if masked:
            strict = _visible(TK, tq, True)
            loms = [jnp.where(strict, lom, 0.0) for lom in loms]
        splits = [_split2(lom) for lom in loms]
        suffixes = [_dot(tri, hi) + _dot(tri, lo) for hi, lo in splits]
        new_carry = []
        for hd in range(nh):
            a = jnp.exp(zs[hd] + suffixes[hd] + carry[hd])
            if masked:
                a = jnp.where(strict, a, 0.0)
            pr, rows = hd // 2, _head_rows(hd % 2)
            acc_scr[pr, rows, :] = acc_scr[pr, rows, :] + _dot(vt_ref[0, pr, j, rows, :], a.astype(BF16))
            new_carry.append(carry[hd] + suffixes[hd][0:1, :])
        return tuple(new_carry)

    carry = step(i, tuple(jnp.zeros((1, tq), F32) for _ in range(nh)), True)

    def live(state):
        n, carry = state
        worst = functools.reduce(jnp.maximum, carry)
        return jnp.logical_and(n < i, jnp.max(worst) > EXP_UNDERFLOW)

    def body(state):
        n, carry = state
        return n + 1, step(i - 1 - n, carry, False)

    lax.while_loop(live, body, (jnp.int32(0), carry))
    for pr in range(nh // 2):
        o_ref[0, :, pr * LANES:(pr + 1) * LANES] = acc_scr[pr].T.astype(BF16)


def _sb(zb, vt, tri_ge):
    b, s, _ = zb.shape
    nq, nk = s // TQ, s // TK
    n_pairs = W_SB // LANES
    return pl.pallas_call(
        _sb_body,
        grid=(b, nq),
        in_specs=[
            pl.BlockSpec((1, TQ, W_SB), lambda bi, i: (bi, i, Z_SQ // W_SB)),
            pl.BlockSpec((1, s, W_SB), lambda bi, i: (bi, 0, Z_SK // W_SB)),
            pl.BlockSpec((1, n_pairs, nk, LANES, TK), lambda bi, i: (bi, 0, 0, 0, 0)),
            pl.BlockSpec((TK, TK), lambda bi, i: (0, 0)),
        ],
        out_specs=pl.BlockSpec((1, TQ, W_SB), lambda bi, i: (bi, i, 0)),
        out_shape=jax.ShapeDtypeStruct((b, s, W_SB), BF16),
        scratch_shapes=[pltpu.VMEM((n_pairs, LANES, TQ), F32)],
        compiler_params=pltpu.CompilerParams(
            dimension_semantics=("parallel", "arbitrary"), vmem_limit_bytes=VMEM_LIMIT),
        name="sb",
    )(zb, zb, vt, tri_ge)


def _dsa_body(iq_ref, ik_ref, q_ref, k_ref, vt_ref, misct_ref, tri_ref, o_ref,
              keys_scr, hi_scr, lo_scr, bias_scr, acc_scr, *, topk):
    i = pl.program_id(1)
    tq = q_ref.shape[1]
    n_pairs = N_HEADS_DSA // 2
    i16_min, i16_max = -2 ** 15, 2 ** 15 - 1

    iq = iq_ref[0]
    iq_heads = _split_heads(iq[:, 0:LANES]) + _split_heads(iq[:, LANES:2 * LANES])
    w_rows = [misct_ref[0, 0, L_WIDX - L_FORGET + hh:L_WIDX - L_FORGET + hh + 1, :] for hh in range(N_IDX_HEADS)]

    def score_step(j, masked):
        kb = ik_ref[0, pl.ds(pl.multiple_of(j * TK, TK), TK), :]
        dots = [_dot_nt(kb, iq_heads[hh]) for hh in range(N_IDX_HEADS)]
        sc = jnp.zeros((TK, tq), F32)
        for hh in range(N_IDX_HEADS):
            sc = sc + w_rows[hh] * jnp.maximum(dots[hh], 0.0)
        if masked:
            sc = jnp.where(_visible(TK, tq, False), sc, -jnp.inf)
        bits = lax.bitcast_convert_type(sc, jnp.int32)
        key = jnp.where(bits < 0, bits ^ jnp.int32(0x7FFFFFFF), bits)
        keys_scr[j] = key
        hi_scr[j] = lax.shift_right_arithmetic(key, 16).astype(jnp.int16)
        lo_scr[j] = ((key & 0xFFFF) - 2 ** 15).astype(jnp.int16)

    def score_body(j, c):
        score_step(j, False)
        return c

    lax.fori_loop(0, i, score_body, 0)
    score_step(i, True)

    def count16(ref, cand):
        cand_b = jnp.broadcast_to(cand.astype(jnp.int16), (PACK16, tq))
        one, zero = jnp.ones((), jnp.int16), jnp.zeros((), jnp.int16)

        def body(j, acc):
            c = jnp.where(ref[j].reshape(TK // PACK16, PACK16, tq) >= cand_b, one, zero)
            parts = [c[r] for r in range(TK // PACK16)]
            while len(parts) > 1:
                parts = [parts[a] + parts[a + 1] for a in range(0, len(parts), 2)]
            return acc + parts[0]

        acc = lax.fori_loop(0, i + 1, body, jnp.zeros((PACK16, tq), jnp.int16))
        return jnp.sum(acc.astype(jnp.int32).astype(F32), axis=0, keepdims=True)

    def count16_gt(ref, thr16):
        return jnp.where(thr16 == i16_max, 0.0, count16(ref, jnp.minimum(thr16 + 1, i16_max)))

    def bisect16(ref, want):
        def it(n, thr16):
            cand = thr16 + lax.shift_left(jnp.int32(1), 15 - n)
            return jnp.where(count16(ref, cand) >= want, cand, thr16)
        return lax.fori_loop(0, 16, it, jnp.full((1, tq), i16_min, jnp.int32))

    kf = jnp.float32(topk)
    t_hi = bisect16(hi_scr, kf)
    n_gt_hi = count16_gt(hi_scr, t_hi)
    t_hi_b = jnp.broadcast_to(t_hi.astype(jnp.int16), (PACK16, tq))

    def low_body(j, c):
        shape = (TK // PACK16, PACK16, tq)
        lo = jnp.where(hi_scr[j].reshape(shape) == t_hi_b, lo_scr[j].reshape(shape), jnp.int16(i16_min))
        lo_scr[j] = lo.reshape(TK, tq)
        return c

    lax.fori_loop(0, i + 1, low_body, 0)
    t_lo = bisect16(lo_scr, kf - n_gt_hi)
    thr = t_hi * 2 ** 16 + t_lo + 2 ** 15
    n_gt_lo = count16_gt(lo_scr, t_lo)
    n_eq = count16(lo_scr, t_lo) - n_gt_lo
    need = kf - n_gt_hi - n_gt_lo

    def plain_step(j, masked):
        sel = keys_scr[j] >= thr
        if masked:
            sel = sel & _visible(TK, tq, False)
        bias_scr[j] = jnp.where(sel, 0.0, NEG_BIG)

    def tie_step(j, run, masked):
        kblk = keys_scr[j]
        eq = kblk == thr
        eqf = jnp.where(eq, 1.0, 0.0)
        before = _dot(tri_ref[...], eqf.astype(BF16)) + run
        sel = (kblk > thr) | (eq & (before < need))
        if masked:
            sel = sel & _visible(TK, tq, False)
        bias_scr[j] = jnp.where(sel, 0.0, NEG_BIG)
        return run + jnp.sum(eqf, axis=0, keepdims=True)

    def plain_bias():
        def body(j, c):
            plain_step(j, False)
            return c
        lax.fori_loop(0, i, body, 0)
        plain_step(i, True)

    def tie_bias():
        run = lax.fori_loop(0, i, lambda j, r: tie_step(j, r, False), jnp.zeros((1, tq), F32))
        tie_step(i, run, True)

    lax.cond(jnp.max(n_eq - need) > 0.0, tie_bias, plain_bias)

    q = q_ref[0]
    q_heads = _split_heads(q[:, 0:LANES]) + _split_heads(q[:, LANES:2 * LANES])
    acc_scr[...] = jnp.zeros_like(acc_scr)

    def attn_step(j, state):
        kb = k_ref[0, pl.ds(pl.multiple_of(j * TK, TK), TK), :]
        bias = bias_scr[j]
        scores = [_dot_nt(kb[:, (hh // 2) * LANES:(hh // 2 + 1) * LANES], q_heads[hh]) + bias
                  for hh in range(N_HEADS_DSA)]
        upd = [_softmax_update(scores[hh], *state[hh]) for hh in range(N_HEADS_DSA)]
        for hh in range(N_HEADS_DSA):
            p, alpha = upd[hh][0], upd[hh][1]
            pr, rows = hh // 2, _head_rows(hh % 2)
            acc_scr[pr, rows, :] = alpha * acc_scr[pr, rows, :] + _dot(vt_ref[0, pr, j, rows, :], p.astype(BF16))
        return tuple((u[2], u[3]) for u in upd)

    init = tuple((jnp.full((1, tq), -jnp.inf, F32), jnp.zeros((1, tq), F32)) for _ in range(N_HEADS_DSA))
    state = lax.fori_loop(0, i + 1, attn_step, init)
    outs = []
    for pr in range(n_pairs):
        inv = jnp.concatenate([jnp.broadcast_to(1.0 / state[2 * pr + hd][1], (HEAD_DIM, tq)) for hd in range(2)],
                              axis=0)
        outs.append((acc_scr[pr] * inv).T)
    o_ref[0] = jnp.concatenate(outs, axis=1).astype(BF16)


def _dsa(zb, vt, misct, tri_lt, topk):
    b, s, _ = zb.shape
    nq, nk = s // TQ, s // TK
    n_pairs = W_DSA // LANES
    return pl.pallas_call(
        functools.partial(_dsa_body, topk=topk),
        grid=(b, nq),
        in_specs=[
            pl.BlockSpec((1, TQ, W_IDX), lambda bi, i: (bi, i, Z_IQ // W_IDX)),
            pl.BlockSpec((1, s, LANES), lambda bi, i: (bi, 0, Z_IK // LANES)),
            pl.BlockSpec((1, TQ, W_DSA), lambda bi, i: (bi, i, Z_CQ // W_DSA)),
            pl.BlockSpec((1, s, W_DSA), lambda bi, i: (bi, 0, Z_CK // W_DSA)),
            pl.BlockSpec((1, n_pairs, nk, LANES, TK), lambda bi, i: (bi, 0, 0, 0, 0)),
            pl.BlockSpec((1, 1, MISC_ROWS, TQ), lambda bi, i: (bi, i, 0, 0)),
            pl.BlockSpec((TK, TK), lambda bi, i: (0, 0)),
        ],
        out_specs=pl.BlockSpec((1, TQ, W_DSA), lambda bi, i: (bi, i, 0)),
        out_shape=jax.ShapeDtypeStruct((b, s, W_DSA), BF16),
        scratch_shapes=[
            pltpu.VMEM((nk, TK, TQ), jnp.int32),
            pltpu.VMEM((nk, TK, TQ), jnp.int16),
            pltpu.VMEM((nk, TK, TQ), jnp.int16),
            pltpu.VMEM((nk, TK, TQ), F32),
            pltpu.VMEM((n_pairs, LANES, TQ), F32),
        ],
        compiler_params=pltpu.CompilerParams(
            dimension_semantics=("parallel", "arbitrary"), vmem_limit_bytes=VMEM_LIMIT),
        name="dsa",
    )(zb, zb, zb, zb, vt, misct, tri_lt)


def _merge_body(x_ref, g_ref, of_ref, os_ref, oc_ref, wg_ref, bg_ref, wf_ref, ws_ref, wc_ref, wo_ref, o_ref):
    x = x_ref[...]
    d = x.shape[1]
    h = _rms_rows(x, g_ref[...]).astype(BF16)
    merged = jnp.zeros_like(x)
    for n, (o_br, w_br) in enumerate(((of_ref, wf_ref), (os_ref, ws_ref), (oc_ref, wc_ref))):
        gate = _dot(h, wg_ref[:, n * d:(n + 1) * d]) + bg_ref[n:n + 1, :]
        gate = 1.0 / (1.0 + jnp.exp(-gate))
        merged = merged + gate * _dot(o_br[...], w_br[...])
    o_ref[...] = x + _dot(merged.astype(BF16), wo_ref[...])


def _merge(x2, g, o_f, o_s, o_c, w_gates, b_gates, w_f, w_s, w_c, w_o):
    m, d = x2.shape
    tm = min(TM_MERGE, m)
    row = lambda w: pl.BlockSpec((tm, w), lambda i: (i, 0))
    const = lambda a: pl.BlockSpec(a.shape, lambda i: (0, 0))
    return pl.pallas_call(
        _merge_body,
        grid=(m // tm,),
        in_specs=[row(d), const(g), row(W_FOX), row(W_SB), row(W_DSA), const(w_gates), const(b_gates),
                  const(w_f), const(w_s), const(w_c), const(w_o)],
        out_specs=row(d),
        out_shape=jax.ShapeDtypeStruct((m, d), F32),
        compiler_params=pltpu.CompilerParams(
            dimension_semantics=("parallel",), vmem_limit_bytes=VMEM_LIMIT),
        name="merge",
    )(x2, g, o_f, o_s, o_c, w_gates, b_gates, w_f, w_s, w_c, w_o)


def _tile_heads(g, n):
    return jnp.tile(g.astype(F32), n)


def _constants(tm_proj):
    r = np.arange(W_FOX)
    bd = (r[:, None] // HEAD_DIM == r[None, :] // HEAD_DIM).astype(np.float32)
    rt = np.arange(tm_proj)
    tri_tok = (rt[None, :] <= rt[:, None]).astype(np.float32)
    rk = np.arange(TK)
    tri_ge = (rk[None, :] >= rk[:, None]).astype(np.float32)
    tri_lt = (rk[None, :] < rk[:, None]).astype(np.float32)

    n_src = W_FOX + N_BIAS_FEATS * LANES
    pq = np.zeros((n_src, W_FOX_AUG), np.float32)
    pk = np.zeros((n_src, W_FOX_AUG), np.float32)
    ones_q = np.zeros((1, W_FOX_AUG), np.float32)
    ones_k = np.zeros((1, W_FOX_AUG), np.float32)
    for hd in range(N_HEADS_FOX):
        for dd in range(HEAD_DIM):
            pq[hd * HEAD_DIM + dd, hd * LANES + dd] = 1.0
            pk[hd * HEAD_DIM + dd, hd * LANES + dd] = 1.0
        for part in range(N_BIAS_FEATS):
            src_row = W_FOX + part * LANES + L_FORGET + hd
            pq[src_row, hd * LANES + HEAD_DIM + part] = 1.0
            ones_q[0, hd * LANES + HEAD_DIM + N_BIAS_FEATS + part] = 1.0
            ones_k[0, hd * LANES + HEAD_DIM + part] = 1.0
            pk[src_row, hd * LANES + HEAD_DIM + N_BIAS_FEATS + part] = -1.0
    half = ROT_DIM // 2
    inv_freq = jnp.power(ROPE_THETA, -jnp.arange(half, dtype=F32) * 2.0 / ROT_DIM)
    dlane = np.arange(LANES) % HEAD_DIM
    freq = jnp.where(jnp.asarray(dlane < ROT_DIM), inv_freq[jnp.asarray(dlane % half)], 0.0)
    return dict(bd=jnp.asarray(bd, BF16), tri_tok=jnp.asarray(tri_tok, BF16), tri_ge=jnp.asarray(tri_ge, BF16),
                tri_lt=jnp.asarray(tri_lt, BF16), pq=jnp.asarray(pq, BF16), pk=jnp.asarray(pk, BF16),
                ones_q=jnp.asarray(ones_q, F32), ones_k=jnp.asarray(ones_k, F32),
                freq=freq.reshape(1, LANES).astype(F32))


def kernel(x, positions, ffn1_norm, ffn1_w_gate, ffn1_w_up, ffn1_w_down, mix_norm, w_in, b_forget, b_gates, q_norm_fox, k_norm_fox, q_norm_sb, k_norm_sb, q_norm_dsa, k_norm_dsa, w_branch_fox, w_branch_sb, w_branch_dsa, w_out, ffn2_norm, ffn2_w_gate, ffn2_w_up, ffn2_w_down):
    b, s, d = x.shape
    depth = w_in.shape[0]
    topk = min(TOPK_MAX, s // 4)
    assert s % TQ == 0 and s % TK == 0 and TQ == TK and d % LANES == 0
    consts = _constants(min(TM_PROJ, s))
    tm_proj = min(TM_PROJ, s)
    pos3 = positions.reshape(b, s // tm_proj, 1, tm_proj)
    scale = HEAD_DIM ** -0.5
    idx_scale = IDX_DIM ** -0.5

    splits = (W_FOX, W_FOX, W_FOX, N_HEADS_FOX, W_SB, W_SB, W_SB, W_DSA, W_DSA, W_DSA,
              W_IDX, IDX_DIM, N_IDX_HEADS, d, d, d)
    offs = np.concatenate([[0], np.cumsum(splits)]).tolist()
    (o_qf, o_kf, o_vf, o_ff, o_qs, o_ks, o_vs, o_qc, o_kc, o_vc, o_qi, o_ki, o_wi, o_ga) = offs[:14]

    x2 = x.reshape(b * s, d)
    for l in range(depth):
        wl = w_in[l]
        pad = jnp.zeros((d, LANES - IDX_DIM - N_HEADS_FOX - N_IDX_HEADS), F32)
        w_proj = jnp.concatenate([
            wl[:, o_qf:o_ff], wl[:, o_qs:o_qi], wl[:, o_qi:o_ki],
            wl[:, o_ki:o_wi], wl[:, o_ff:o_qs], wl[:, o_wi:o_ga], pad], axis=1).astype(BF16)
        w_gates = wl[:, o_ga:].astype(BF16)
        ones = lambda n: jnp.ones((n,), F32)
        gains = jnp.concatenate([
            _tile_heads(q_norm_fox[l], N_HEADS_FOX) * scale, _tile_heads(k_norm_fox[l], N_HEADS_FOX), ones(W_FOX),
            _tile_heads(q_norm_sb[l], N_HEADS_SB) * scale, _tile_heads(k_norm_sb[l], N_HEADS_SB), ones(W_SB),
            _tile_heads(q_norm_dsa[l], N_HEADS_DSA) * scale, _tile_heads(k_norm_dsa[l], N_HEADS_DSA), ones(W_DSA),
            ones(W_IDX) * idx_scale, ones(LANES)]).reshape(1, N_PROJ)
        fbias = jnp.zeros((LANES,), F32).at[L_FORGET:L_FORGET + N_HEADS_FOX].set(b_forget[l].astype(F32))
        fbias = fbias.reshape(1, LANES)

        x2 = _ffn(x2, ffn1_norm[l].reshape(1, d), ffn1_w_gate[l].astype(BF16), ffn1_w_up[l].astype(BF16),
                  ffn1_w_down[l].astype(BF16))
        zb, vt_f, vt_s, vt_c, misct = _proj(x2.reshape(b, s, d), pos3, mix_norm[l].reshape(1, d), w_proj, gains,
                                            consts, fbias)
        o_f = _fox(zb, vt_f)
        o_s = _sb(zb, vt_s, consts["tri_ge"])
        o_c = _dsa(zb, vt_c, misct, consts["tri_lt"], topk)
        x2 = _merge(x2, mix_norm[l].reshape(1, d), o_f.reshape(b * s, W_FOX), o_s.reshape(b * s, W_SB),
                    o_c.reshape(b * s, W_DSA), w_gates, b_gates[l].astype(F32),
                    w_branch_fox[l].astype(BF16), w_branch_sb[l].astype(BF16), w_branch_dsa[l].astype(BF16),
                    w_out[l].astype(BF16))
        x2 = _ffn(x2, ffn2_norm[l].reshape(1, d), ffn2_w_gate[l].astype(BF16), ffn2_w_up[l].astype(BF16),
                  ffn2_w_down[l].astype(BF16))
    return x2.reshape(b, s, d)
```

```python
import functools

import jax
import jax.numpy as jnp
import numpy as np
from jax import lax
from jax.experimental import pallas as pl
from jax.experimental.pallas import tpu as pltpu

F32 = jnp.float32
BF16 = jnp.bfloat16

HEAD_DIM = 64
N_HEADS_FOX = 6
N_HEADS_SB = 6
N_HEADS_DSA = 4
N_IDX_HEADS = 4
IDX_DIM = 64
TOPK_MAX = 256
ROPE_THETA = 500000.0
ROT_DIM = HEAD_DIM // 4
EPS = 1e-6

LANES = 128
SUBLANES = 8
PACK16 = 2 * SUBLANES
W_FOX = N_HEADS_FOX * HEAD_DIM
W_SB = N_HEADS_SB * HEAD_DIM
W_DSA = N_HEADS_DSA * HEAD_DIM
W_IDX = N_IDX_HEADS * IDX_DIM

C_FQ, C_FK, C_FV = 0, W_FOX, 2 * W_FOX
C_SQ, C_SK, C_SV = 3 * W_FOX, 3 * W_FOX + W_SB, 3 * W_FOX + 2 * W_SB
C_CQ = 3 * W_FOX + 3 * W_SB
C_CK, C_CV = C_CQ + W_DSA, C_CQ + 2 * W_DSA
C_IQ = C_CQ + 3 * W_DSA
C_MISC = C_IQ + W_IDX
N_PROJ = C_MISC + LANES
L_FORGET = IDX_DIM
L_WIDX = IDX_DIM + N_HEADS_FOX
MISC_ROWS = 16

W_FOX_AUG = N_HEADS_FOX * LANES
Z_FQ, Z_FK = 0, W_FOX_AUG
Z_SQ, Z_SK = 2 * W_FOX_AUG, 2 * W_FOX_AUG + W_SB
Z_CQ = 2 * W_FOX_AUG + 2 * W_SB
Z_CK = Z_CQ + W_DSA
Z_IQ = Z_CK + W_DSA
Z_IK = Z_IQ + W_IDX
N_ZB = Z_IK + LANES
N_BIAS_FEATS = 3

TQ = 256
TK = 256
TM_PROJ = 512
TM_FFN = 512
TM_MERGE = 512
NEG_BIG = -1e30
EXP_UNDERFLOW = -110.0
VMEM_LIMIT = 56 * 1024 * 1024

_NT = (((1,), (1,)), ((), ()))


def _dot(a, b):
    return jnp.dot(a, b, preferred_element_type=F32)


def _dot_nt(a, b):
    return lax.dot_general(a, b, _NT, preferred_element_type=F32)


def _split2(x):
    hi = x.astype(BF16)
    lo = (x - hi.astype(F32)).astype(BF16)
    return hi, lo


def _split3(x):
    hi = x.astype(BF16)
    r = x - hi.astype(F32)
    mid = r.astype(BF16)
    lo = (r - mid.astype(F32)).astype(BF16)
    return hi, mid, lo


def _rms_rows(x, g):
    ms = jnp.mean(x * x, axis=-1, keepdims=True)
    return x * lax.rsqrt(ms + EPS) * g


def _softplus(z):
    return jnp.maximum(z, 0.0) + jnp.log(1.0 + jnp.exp(-jnp.abs(z)))


def _ffn_body(x_ref, g_ref, wg_ref, wu_ref, wd_ref, o_ref, h_scr, acc_scr):
    f = pl.program_id(1)

    @pl.when(f == 0)
    def _():
        h_scr[...] = _rms_rows(x_ref[...], g_ref[...]).astype(BF16)
        acc_scr[...] = jnp.zeros_like(acc_scr)

    h = h_scr[...]
    a = _dot(h, wg_ref[...])
    u = _dot(h, wu_ref[...])
    p = (a * (1.0 / (1.0 + jnp.exp(-a))) * u).astype(BF16)
    acc_scr[...] += _dot(p, wd_ref[...])

    @pl.when(f == pl.num_programs(1) - 1)
    def _():
        o_ref[...] = x_ref[...] + 0.5 * acc_scr[...]


def _ffn_tile_f(d_ff):
    best = LANES
    for t in range(LANES, d_ff + 1, LANES):
        if d_ff % t == 0 and t <= 1536:
            best = t
    return best


def _ffn(x2, g, wg, wu, wd):
    m, d = x2.shape
    d_ff = wg.shape[1]
    tm = min(TM_FFN, m)
    tf = _ffn_tile_f(d_ff)
    return pl.pallas_call(
        _ffn_body,
        grid=(m // tm, d_ff // tf),
        in_specs=[
            pl.BlockSpec((tm, d), lambda i, f: (i, 0)),
            pl.BlockSpec((1, d), lambda i, f: (0, 0)),
            pl.BlockSpec((d, tf), lambda i, f: (0, f)),
            pl.BlockSpec((d, tf), lambda i, f: (0, f)),
            pl.BlockSpec((tf, d), lambda i, f: (f, 0)),
        ],
        out_specs=pl.BlockSpec((tm, d), lambda i, f: (i, 0)),
        out_shape=jax.ShapeDtypeStruct((m, d), F32),
        scratch_shapes=[pltpu.VMEM((tm, d), BF16), pltpu.VMEM((tm, d), F32)],
        compiler_params=pltpu.CompilerParams(
            dimension_semantics=("parallel", "arbitrary"), vmem_limit_bytes=VMEM_LIMIT),
        name="ffn",
    )(x2, g, wg, wu, wd)


def _proj_body(x_ref, pos_ref, g_ref, w_ref, gains_ref, bd_ref, freq_ref, fbias_ref, tri_ref,
               pq_ref, pk_ref, ones_q_ref, ones_k_ref,
               zb_ref, vtf_ref, vts_ref, vtc_ref, misct_ref, carry_scr):
    t = pl.program_id(1)
    tm = x_ref.shape[1]
    h = _rms_rows(x_ref[0], g_ref[...]).astype(BF16)

    lane = lax.broadcasted_iota(jnp.int32, (1, LANES), 1)
    d_in_head = lane % HEAD_DIM
    pos_cols = jnp.broadcast_to(pos_ref[0, 0].astype(F32), (LANES, tm)).T
    ang = pos_cols * freq_ref[...]
    cos = jnp.cos(ang)
    sin_signed = jnp.where(d_in_head < ROT_DIM // 2, -jnp.sin(ang), jnp.sin(ang))
    first_half = d_in_head < ROT_DIM // 2

    def rope(x):
        partner = jnp.where(first_half, pltpu.roll(x, LANES - ROT_DIM // 2, 1), pltpu.roll(x, ROT_DIM // 2, 1))
        return x * cos + partner * sin_signed

    def head_norm(z, c0, width):
        hi, lo = _split2(z * z)
        bd = bd_ref[:width, :width]
        ss = _dot(hi, bd) + _dot(lo, bd)
        return z * lax.rsqrt(ss * (1.0 / HEAD_DIM) + EPS) * gains_ref[:, c0:c0 + width]

    def group(c0, width, norm, rot):
        z = _dot(h, w_ref[:, c0:c0 + width])
        if norm:
            z = head_norm(z, c0, width)
        else:
            z = z * gains_ref[:, c0:c0 + width]
        if rot:
            z = jnp.concatenate([rope(z[:, c:c + LANES]) for c in range(0, width, LANES)], axis=1)
        return z

    def store_vt(vt_ref, c0, width):
        zt = _dot(h, w_ref[:, c0:c0 + width]).T
        for p in range(width // LANES):
            for c in range(tm // TK):
                vt_ref[0, p, c] = zt[p * LANES:(p + 1) * LANES, c * TK:(c + 1) * TK].astype(BF16)

    zm = _dot(h, w_ref[:, C_MISC:C_MISC + LANES])
    ki = rope(zm)
    zb_ref[0, :, Z_IK:Z_IK + LANES] = jnp.where(lane < IDX_DIM, ki, pltpu.roll(ki, IDX_DIM, 1)).astype(BF16)

    @pl.when(t == 0)
    def _():
        carry_scr[...] = jnp.zeros_like(carry_scr)

    logf = -_softplus(-(zm + fbias_ref[...]))
    hi, mid, lo = _split3(logf)
    tri = tri_ref[...]
    cum = _dot(tri, hi) + _dot(tri, mid) + _dot(tri, lo) + carry_scr[0:1, :]
    carry_scr[...] = jnp.broadcast_to(cum[tm - 1:tm, :], carry_scr.shape)

    is_forget = (lane >= L_FORGET) & (lane < L_FORGET + N_HEADS_FOX)
    is_widx = (lane >= L_WIDX) & (lane < L_WIDX + N_IDX_HEADS)
    misc = jnp.where(is_forget, cum, jnp.where(is_widx, zm * (N_IDX_HEADS ** -0.5), 0.0))
    misc_t = misc.T
    for c in range(tm // TQ):
        misct_ref[0, c] = misc_t[L_FORGET:L_FORGET + MISC_ROWS, c * TQ:(c + 1) * TQ]

    c_hi, c_mid, c_lo = _split3(cum)
    for c0, z0, p_ref, ones_ref in ((C_FQ, Z_FQ, pq_ref, ones_q_ref), (C_FK, Z_FK, pk_ref, ones_k_ref)):
        zn = group(c0, W_FOX, True, False).astype(BF16)
        src = jnp.concatenate([zn, c_hi, c_mid, c_lo], axis=1)
        zb_ref[0, :, z0:z0 + W_FOX_AUG] = (_dot(src, p_ref[...]) + ones_ref[...]).astype(BF16)

    zb_ref[0, :, Z_SQ:Z_SQ + W_SB] = group(C_SQ, W_SB, True, False).astype(BF16)
    zb_ref[0, :, Z_SK:Z_SK + W_SB] = group(C_SK, W_SB, True, False).astype(BF16)
    zb_ref[0, :, Z_CQ:Z_CQ + W_DSA] = group(C_CQ, W_DSA, True, True).astype(BF16)
    zb_ref[0, :, Z_CK:Z_CK + W_DSA] = group(C_CK, W_DSA, True, True).astype(BF16)
    zb_ref[0, :, Z_IQ:Z_IQ + W_IDX] = group(C_IQ, W_IDX, False, True).astype(BF16)
    store_vt(vtf_ref, C_FV, W_FOX)
    store_vt(vts_ref, C_SV, W_SB)
    store_vt(vtc_ref, C_CV, W_DSA)


def _proj(x, pos3, g, w, gains, consts, fbias):
    b, s, d = x.shape
    tm = min(TM_PROJ, s)
    const = lambda a: pl.BlockSpec(a.shape, lambda bi, ti: (0,) * a.ndim)
    vt_spec = lambda n: pl.BlockSpec((1, n, tm // TK, LANES, TK), lambda bi, ti: (bi, 0, ti, 0, 0))
    vt_shape = lambda n: jax.ShapeDtypeStruct((b, n, s // TK, LANES, TK), BF16)
    cs = (consts["bd"], consts["freq"], fbias, consts["tri_tok"], consts["pq"], consts["pk"],
          consts["ones_q"], consts["ones_k"])
    return pl.pallas_call(
        _proj_body,
        grid=(b, s // tm),
        in_specs=[
            pl.BlockSpec((1, tm, d), lambda bi, ti: (bi, ti, 0)),
            pl.BlockSpec((1, 1, 1, tm), lambda bi, ti: (bi, ti, 0, 0)),
            const(g), const(w), const(gains),
        ] + [const(a) for a in cs],
        out_specs=[
            pl.BlockSpec((1, tm, N_ZB), lambda bi, ti: (bi, ti, 0)),
            vt_spec(W_FOX // LANES), vt_spec(W_SB // LANES), vt_spec(W_DSA // LANES),
            pl.BlockSpec((1, tm // TQ, MISC_ROWS, TQ), lambda bi, ti: (bi, ti, 0, 0)),
        ],
        out_shape=[
            jax.ShapeDtypeStruct((b, s, N_ZB), BF16),
            vt_shape(W_FOX // LANES), vt_shape(W_SB // LANES), vt_shape(W_DSA // LANES),
            jax.ShapeDtypeStruct((b, s // TQ, MISC_ROWS, TQ), F32),
        ],
        scratch_shapes=[pltpu.VMEM((SUBLANES, LANES), F32)],
        compiler_params=pltpu.CompilerParams(
            dimension_semantics=("parallel", "arbitrary"), vmem_limit_bytes=VMEM_LIMIT),
        name="proj",
    )(x, pos3, g, w, gains, *cs)


def _head_masks():
    lane = lax.broadcasted_iota(jnp.int32, (1, LANES), 1)
    return lane < HEAD_DIM, lane >= HEAD_DIM


def _split_heads(q128):
    lo_half, hi_half = _head_masks()
    zero = jnp.zeros_like(q128)
    return jnp.where(lo_half, q128, zero), jnp.where(hi_half, q128, zero)


def _visible(tk, tq, strict):
    key = lax.broadcasted_iota(jnp.int32, (tk, tq), 0)
    qry = lax.broadcasted_iota(jnp.int32, (tk, tq), 1)
    return key < qry if strict else key <= qry


def _softmax_update(s, m, l):
    m_new = jnp.maximum(m, jnp.max(s, axis=0, keepdims=True))
    alpha = jnp.exp(m - m_new)
    p = jnp.exp(s - m_new)
    return p, alpha, m_new, alpha * l + jnp.sum(p, axis=0, keepdims=True)


def _head_rows(hd):
    return slice(hd * HEAD_DIM, (hd + 1) * HEAD_DIM)


def _walk_blocks(n_blocks, step, state):
    last = n_blocks - 1
    state = lax.fori_loop(0, last // 2, lambda n, st: step(2 * n, 2, False, st), state)
    return lax.cond(last % 2 == 1,
                    lambda st: step(last - 1, 2, True, st),
                    lambda st: step(last, 1, True, st),
                    state)


def _key_rows(ref, j0, count):
    return ref[0, pl.ds(pl.multiple_of(j0 * TK, TK), count * TK), :]


def _fox_body(q_ref, k_ref, vt_ref, o_ref, acc_scr):
    i = pl.program_id(1)
    tq = q_ref.shape[1]
    nh = N_HEADS_FOX
    q_heads = [q_ref[0, :, hd * LANES:(hd + 1) * LANES] for hd in range(nh)]
    acc_scr[...] = jnp.zeros_like(acc_scr)

    def step(j0, count, ends, state):
        kb = _key_rows(k_ref, j0, count)
        scores = [_dot_nt(kb[:, hd * LANES:(hd + 1) * LANES], q_heads[hd]) for hd in range(nh)]
        weights = []
        for b in range(count):
            s = [sh[b * TK:(b + 1) * TK, :] for sh in scores]
            if ends and b == count - 1:
                vis = _visible(TK, tq, False)
                s = [jnp.where(vis, sh, -jnp.inf) for sh in s]
            upd = [_softmax_update(s[hd], *state[hd]) for hd in range(nh)]
            state = tuple((u[2], u[3]) for u in upd)
            weights.append([(u[0], u[1]) for u in upd])
        accs = [acc_scr[hd // 2, _head_rows(hd % 2), :] for hd in range(nh)]
        for b in range(count):
            for hd in range(nh):
                p, alpha = weights[b][hd]
                accs[hd] = alpha * accs[hd] + _dot(vt_ref[0, hd // 2, j0 + b, _head_rows(hd % 2), :], p.astype(BF16))
        for hd in range(nh):
            acc_scr[hd // 2, _head_rows(hd % 2), :] = accs[hd]
        return state

    init = tuple((jnp.full((1, tq), -jnp.inf, F32), jnp.zeros((1, tq), F32)) for _ in range(nh))
    state = _walk_blocks(i + 1, step, init)
    for pr in range(nh // 2):
        inv = jnp.concatenate([jnp.broadcast_to(1.0 / state[2 * pr + hd][1], (HEAD_DIM, tq)) for hd in range(2)],
                              axis=0)
        o_ref[0, :, pr * LANES:(pr + 1) * LANES] = (acc_scr[pr] * inv).T.astype(BF16)


def _fox(zb, vt):
    b, s, _ = zb.shape
    nq, nk = s // TQ, s // TK
    n_pairs = W_FOX // LANES
    return pl.pallas_call(
        _fox_body,
        grid=(b, nq),
        in_specs=[
            pl.BlockSpec((1, TQ, W_FOX_AUG), lambda bi, i: (bi, i, Z_FQ // W_FOX_AUG)),
            pl.BlockSpec((1, s, W_FOX_AUG), lambda bi, i: (bi, 0, Z_FK // W_FOX_AUG)),
            pl.BlockSpec((1, n_pairs, nk, LANES, TK), lambda bi, i: (bi, 0, 0, 0, 0)),
        ],
        out_specs=pl.BlockSpec((1, TQ, W_FOX), lambda bi, i: (bi, i, 0)),
        out_shape=jax.ShapeDtypeStruct((b, s, W_FOX), BF16),
        scratch_shapes=[pltpu.VMEM((n_pairs, LANES, TQ), F32)],
        compiler_params=pltpu.CompilerParams(
            dimension_semantics=("parallel", "arbitrary"), vmem_limit_bytes=VMEM_LIMIT),
        name="fox",
    )(zb, zb, vt)


def _sb_body(q_ref, k_ref, vt_ref, tri_ref, o_ref, acc_scr):
    i = pl.program_id(1)
    tq = q_ref.shape[1]
    nh = N_HEADS_SB
    q_heads = []
    for pr in range(nh // 2):
        q_heads.extend(_split_heads(q_ref[0, :, pr * LANES:(pr + 1) * LANES]))
    acc_scr[...] = jnp.zeros_like(acc_scr)

    def step(j, carry, masked):
        kb = k_ref[0, pl.ds(pl.multiple_of(j * TK, TK), TK), :]
        tri = tri_ref[...]
        zs = [_dot_nt(kb[:, (hd // 2) * LANES:(hd // 2 + 1) * LANES], q_heads[hd]) for hd in range(nh)]
        loms = [-_softplus(z) for z in zs]
        if masked:
            strict = _visible(TK, tq, True)
            loms = [jnp.where(strict, lom, 0.0) for lom in loms]
        splits = [_split2(lom) for lom in loms]
        suffixes = [_dot(tri, hi) + _dot(tri, lo) for hi, lo in splits]
        new_carry = []
        for hd in range(nh):
            a = jnp.exp(zs[hd] + suffixes[hd] + carry[hd])
            if masked:
                a = jnp.where(strict, a, 0.0)
            pr, rows = hd // 2, _head_rows(hd % 2)
            acc_scr[pr, rows, :] = acc_scr[pr, rows, :] + _dot(vt_ref[0, pr, j, rows, :], a.astype(BF16))
            new_carry.append(carry[hd] + suffixes[hd][0:1, :])
        return tuple(new_carry)

    carry = step(i, tuple(jnp.zeros((1, tq), F32) for _ in range(nh)), True)

    def live(state):
        n, carry = state
        worst = functools.reduce(jnp.maximum, carry)
        return jnp.logical_and(n < i, jnp.max(worst) > EXP_UNDERFLOW)

    def body(state):
        n, carry = state
        return n + 1, step(i - 1 - n, carry, False)

    lax.while_loop(live, body, (jnp.int32(0), carry))
    for pr in range(nh // 2):
        o_ref[0, :, pr * LANES:(pr + 1) * LANES] = acc_scr[pr].T.astype(BF16)


def _sb(zb, vt, tri_ge):
    b, s, _ = zb.shape
    nq, nk = s // TQ, s // TK
    n_pairs = W_SB // LANES
    return pl.pallas_call(
        _sb_body,
        grid=(b, nq),
        in_specs=[
            pl.BlockSpec((1, TQ, W_SB), lambda bi, i: (bi, i, Z_SQ // W_SB)),
            pl.BlockSpec((1, s, W_SB), lambda bi, i: (bi, 0, Z_SK // W_SB)),
            pl.BlockSpec((1, n_pairs, nk, LANES, TK), lambda bi, i: (bi, 0, 0, 0, 0)),
            pl.BlockSpec((TK, TK), lambda bi, i: (0, 0)),
        ],
        out_specs=pl.BlockSpec((1, TQ, W_SB), lambda bi, i: (bi, i, 0)),
        out_shape=jax.ShapeDtypeStruct((b, s, W_SB), BF16),
        scratch_shapes=[pltpu.VMEM((n_pairs, LANES, TQ), F32)],
        compiler_params=pltpu.CompilerParams(
            dimension_semantics=("parallel", "arbitrary"), vmem_limit_bytes=VMEM_LIMIT),
        name="sb",
    )(zb, zb, vt, tri_ge)


def _dsa_body(iq_ref, ik_ref, q_ref, k_ref, vt_ref, misct_ref, tri_ref, o_ref,
              keys_scr, hi_scr, lo_scr, bias_scr, acc_scr, *, topk):
    i = pl.program_id(1)
    tq = q_ref.shape[1]
    n_pairs = N_HEADS_DSA // 2
    i16_min, i16_max = -2 ** 15, 2 ** 15 - 1

    iq = iq_ref[0]
    iq_heads = _split_heads(iq[:, 0:LANES]) + _split_heads(iq[:, LANES:2 * LANES])
    w_rows = [misct_ref[0, 0, L_WIDX - L_FORGET + hh:L_WIDX - L_FORGET + hh + 1, :] for hh in range(N_IDX_HEADS)]

    def score_step(j0, count, ends, carry):
        kb = _key_rows(ik_ref, j0, count)
        dots = [_dot_nt(kb, iq_heads[hh]) for hh in range(N_IDX_HEADS)]
        for b in range(count):
            rows = slice(b * TK, (b + 1) * TK)
            sc = jnp.zeros((TK, tq), F32)
            for hh in range(N_IDX_HEADS):
                sc = sc + w_rows[hh] * jnp.maximum(dots[hh][rows, :], 0.0)
            if ends and b == count - 1:
                sc = jnp.where(_visible(TK, tq, False), sc, -jnp.inf)
            bits = lax.bitcast_convert_type(sc, jnp.int32)
            key = jnp.where(bits < 0, bits ^ jnp.int32(0x7FFFFFFF), bits)
            keys_scr[j0 + b] = key
            hi_scr[j0 + b] = lax.shift_right_arithmetic(key, 16).astype(jnp.int16)
            lo_scr[j0 + b] = ((key & 0xFFFF) - 2 ** 15).astype(jnp.int16)
        return carry

    _walk_blocks(i + 1, score_step, 0)

    def count16(ref, cand):
        cand_b = jnp.broadcast_to(cand.astype(jnp.int16), (PACK16, tq))
        one, zero = jnp.ones((), jnp.int16), jnp.zeros((), jnp.int16)

        def body(j, acc):
            c = jnp.where(ref[j].reshape(TK // PACK16, PACK16, tq) >= cand_b, one, zero)
            parts = [c[r] for r in range(TK // PACK16)]
            while len(parts) > 1:
                parts = [parts[a] + parts[a + 1] for a in range(0, len(parts), 2)]
            return acc + parts[0]

        acc = lax.fori_loop(0, i + 1, body, jnp.zeros((PACK16, tq), jnp.int16))
        return jnp.sum(acc.astype(jnp.int32).astype(F32), axis=0, keepdims=True)

    def count16_gt(ref, thr16):
        return jnp.where(thr16 == i16_max, 0.0, count16(ref, jnp.minimum(thr16 + 1, i16_max)))

    def bisect16(ref, want):
        def it(n, thr16):
            cand = thr16 + lax.shift_left(jnp.int32(1), 15 - n)
            return jnp.where(count16(ref, cand) >= want, cand, thr16)
        return lax.fori_loop(0, 16, it, jnp.full((1, tq), i16_min, jnp.int32))

    kf = jnp.float32(topk)
    t_hi = bisect16(hi_scr, kf)
    n_gt_hi = count16_gt(hi_scr, t_hi)
    t_hi_b = jnp.broadcast_to(t_hi.astype(jnp.int16), (PACK16, tq))

    def low_body(j, c):
        shape = (TK // PACK16, PACK16, tq)
        lo = jnp.where(hi_scr[j].reshape(shape) == t_hi_b, lo_scr[j].reshape(shape), jnp.int16(i16_min))
        lo_scr[j] = lo.reshape(TK, tq)
        return c

    lax.fori_loop(0, i + 1, low_body, 0)
    t_lo = bisect16(lo_scr, kf - n_gt_hi)
    thr = t_hi * 2 ** 16 + t_lo + 2 ** 15
    n_gt_lo = count16_gt(lo_scr, t_lo)
    n_eq = count16(lo_scr, t_lo) - n_gt_lo
    need = kf - n_gt_hi - n_gt_lo

    def plain_step(j, masked):
        sel = keys_scr[j] >= thr
        if masked:
            sel = sel & _visible(TK, tq, False)
        bias_scr[j] = jnp.where(sel, 0.0, NEG_BIG)

    def tie_step(j, run, masked):
        kblk = keys_scr[j]
        eq = kblk == thr
        eqf = jnp.where(eq, 1.0, 0.0)
        before = _dot(tri_ref[...], eqf.astype(BF16)) + run
        sel = (kblk > thr) | (eq & (before < need))
        if masked:
            sel = sel & _visible(TK, tq, False)
        bias_scr[j] = jnp.where(sel, 0.0, NEG_BIG)
        return run + jnp.sum(eqf, axis=0, keepdims=True)

    def plain_bias():
        def body(j, c):
            plain_step(j, False)
            return c
        lax.fori_loop(0, i, body, 0)
        plain_step(i, True)

    def tie_bias():
        run = lax.fori_loop(0, i, lambda j, r: tie_step(j, r, False), jnp.zeros((1, tq), F32))
        tie_step(i, run, True)

    lax.cond(jnp.max(n_eq - need) > 0.0, tie_bias, plain_bias)

    q = q_ref[0]
    q_heads = _split_heads(q[:, 0:LANES]) + _split_heads(q[:, LANES:2 * LANES])
    acc_scr[...] = jnp.zeros_like(acc_scr)

    def attn_step(j0, count, ends, state):
        kb = _key_rows(k_ref, j0, count)
        scores = [_dot_nt(kb[:, (hh // 2) * LANES:(hh // 2 + 1) * LANES], q_heads[hh]) for hh in range(N_HEADS_DSA)]
        weights = []
        for b in range(count):
            bias = bias_scr[j0 + b]
            upd = [_softmax_update(scores[hh][b * TK:(b + 1) * TK, :] + bias, *state[hh])
                   for hh in range(N_HEADS_DSA)]
            state = tuple((u[2], u[3]) for u in upd)
            weights.append([(u[0], u[1]) for u in upd])
        accs = [acc_scr[hh // 2, _head_rows(hh % 2), :] for hh in range(N_HEADS_DSA)]
        for b in range(count):
            for hh in range(N_HEADS_DSA):
                p, alpha = weights[b][hh]
                accs[hh] = alpha * accs[hh] + _dot(vt_ref[0, hh // 2, j0 + b, _head_rows(hh % 2), :],
                                                   p.astype(BF16))
        for hh in range(N_HEADS_DSA):
            acc_scr[hh // 2, _head_rows(hh % 2), :] = accs[hh]
        return state

    init = tuple((jnp.full((1, tq), -jnp.inf, F32), jnp.zeros((1, tq), F32)) for _ in range(N_HEADS_DSA))
    state = _walk_blocks(i + 1, attn_step, init)
    outs = []
    for pr in range(n_pairs):
        inv = jnp.concatenate([jnp.broadcast_to(1.0 / state[2 * pr + hd][1], (HEAD_DIM, tq)) for hd in range(2)],
                              axis=0)
        outs.append((acc_scr[pr] * inv).T)
    o_ref[0] = jnp.concatenate(outs, axis=1).astype(BF16)


def _dsa(zb, vt, misct, tri_lt, topk):
    b, s, _ = zb.shape
    nq, nk = s // TQ, s // TK
    n_pairs = W_DSA // LANES
    return pl.pallas_call(
        functools.partial(_dsa_body, topk=topk),
        grid=(b, nq),
        in_specs=[
            pl.BlockSpec((1, TQ, W_IDX), lambda bi, i: (bi, i, Z_IQ // W_IDX)),
            pl.BlockSpec((1, s, LANES), lambda bi, i: (bi, 0, Z_IK // LANES)),
            pl.BlockSpec((1, TQ, W_DSA), lambda bi, i: (bi, i, Z_CQ // W_DSA)),
            pl.BlockSpec((1, s, W_DSA), lambda bi, i: (bi, 0, Z_CK // W_DSA)),
            pl.BlockSpec((1, n_pairs, nk, LANES, TK), lambda bi, i: (bi, 0, 0, 0, 0)),
            pl.BlockSpec((1, 1, MISC_ROWS, TQ), lambda bi, i: (bi, i, 0, 0)),
            pl.BlockSpec((TK, TK), lambda bi, i: (0, 0)),
        ],
        out_specs=pl.BlockSpec((1, TQ, W_DSA), lambda bi, i: (bi, i, 0)),
        out_shape=jax.ShapeDtypeStruct((b, s, W_DSA), BF16),
        scratch_shapes=[
            pltpu.VMEM((nk, TK, TQ), jnp.int32),
            pltpu.VMEM((nk, TK, TQ), jnp.int16),
            pltpu.VMEM((nk, TK, TQ), jnp.int16),
            pltpu.VMEM((nk, TK, TQ), F32),
            pltpu.VMEM((n_pairs, LANES, TQ), F32),
        ],
        compiler_params=pltpu.CompilerParams(
            dimension_semantics=("parallel", "arbitrary"), vmem_limit_bytes=VMEM_LIMIT),
        name="dsa",
    )(zb, zb, zb, zb, vt, misct, tri_lt)


def _merge_body(x_ref, g_ref, of_ref, os_ref, oc_ref, wg_ref, bg_ref, wf_ref, ws_ref, wc_ref, wo_ref, o_ref):
    x = x_ref[...]
    d = x.shape[1]
    h = _rms_rows(x, g_ref[...]).astype(BF16)
    merged = jnp.zeros_like(x)
    for n, (o_br, w_br) in enumerate(((of_ref, wf_ref), (os_ref, ws_ref), (oc_ref, wc_ref))):
        gate = _dot(h, wg_ref[:, n * d:(n + 1) * d]) + bg_ref[n:n + 1, :]
        gate = 1.0 / (1.0 + jnp.exp(-gate))
        merged = merged + gate * _dot(o_br[...], w_br[...])
    o_ref[...] = x + _dot(merged.astype(BF16), wo_ref[...])


def _merge(x2, g, o_f, o_s, o_c, w_gates, b_gates, w_f, w_s, w_c, w_o):
    m, d = x2.shape
    tm = min(TM_MERGE, m)
    row = lambda w: pl.BlockSpec((tm, w), lambda i: (i, 0))
    const = lambda a: pl.BlockSpec(a.shape, lambda i: (0, 0))
    return pl.pallas_call(
        _merge_body,
        grid=(m // tm,),
        in_specs=[row(d), const(g), row(W_FOX), row(W_SB), row(W_DSA), const(w_gates), const(b_gates),
                  const(w_f), const(w_s), const(w_c), const(w_o)],
        out_specs=row(d),
        out_shape=jax.ShapeDtypeStruct((m, d), F32),
        compiler_params=pltpu.CompilerParams(
            dimension_semantics=("parallel",), vmem_limit_bytes=VMEM_LIMIT),
        name="merge",
    )(x2, g, o_f, o_s, o_c, w_gates, b_gates, w_f, w_s, w_c, w_o)


def _tile_heads(g, n):
    return jnp.tile(g.astype(F32), n)


def _constants(tm_proj):
    r = np.arange(W_FOX)
    bd = (r[:, None] // HEAD_DIM == r[None, :] // HEAD_DIM).astype(np.float32)
    rt = np.arange(tm_proj)
    tri_tok = (rt[None, :] <= rt[:, None]).astype(np.float32)
    rk = np.arange(TK)
    tri_ge = (rk[None, :] >= rk[:, None]).astype(np.float32)
    tri_lt = (rk[None, :] < rk[:, None]).astype(np.float32)

    n_src = W_FOX + N_BIAS_FEATS * LANES
    pq = np.zeros((n_src, W_FOX_AUG), np.float32)
    pk = np.zeros((n_src, W_FOX_AUG), np.float32)
    ones_q = np.zeros((1, W_FOX_AUG), np.float32)
    ones_k = np.zeros((1, W_FOX_AUG), np.float32)
    for hd in range(N_HEADS_FOX):
        for dd in range(HEAD_DIM):
            pq[hd * HEAD_DIM + dd, hd * LANES + dd] = 1.0
            pk[hd * HEAD_DIM + dd, hd * LANES + dd] = 1.0
        for part in range(N_BIAS_FEATS):
            src_row = W_FOX + part * LANES + L_FORGET + hd
            pq[src_row, hd * LANES + HEAD_DIM + part] = 1.0
            ones_q[0, hd * LANES + HEAD_DIM + N_BIAS_FEATS + part] = 1.0
            ones_k[0, hd * LANES + HEAD_DIM + part] = 1.0
            pk[src_row, hd * LANES + HEAD_DIM + N_BIAS_FEATS + part] = -1.0
    half = ROT_DIM // 2
    inv_freq = jnp.power(ROPE_THETA, -jnp.arange(half, dtype=F32) * 2.0 / ROT_DIM)
    dlane = np.arange(LANES) % HEAD_DIM
    freq = jnp.where(jnp.asarray(dlane < ROT_DIM), inv_freq[jnp.asarray(dlane % half)], 0.0)
    return dict(bd=jnp.asarray(bd, BF16), tri_tok=jnp.asarray(tri_tok, BF16), tri_ge=jnp.asarray(tri_ge, BF16),
                tri_lt=jnp.asarray(tri_lt, BF16), pq=jnp.asarray(pq, BF16), pk=jnp.asarray(pk, BF16),
                ones_q=jnp.asarray(ones_q, F32), ones_k=jnp.asarray(ones_k, F32),
                freq=freq.reshape(1, LANES).astype(F32))


def kernel(x, positions, ffn1_norm, ffn1_w_gate, ffn1_w_up, ffn1_w_down, mix_norm, w_in, b_forget, b_gates, q_norm_fox, k_norm_fox, q_norm_sb, k_norm_sb, q_norm_dsa, k_norm_dsa, w_branch_fox, w_branch_sb, w_branch_dsa, w_out, ffn2_norm, ffn2_w_gate, ffn2_w_up, ffn2_w_down):
    b, s, d = x.shape
    depth = w_in.shape[0]
    topk = min(TOPK_MAX, s // 4)
    assert s % TQ == 0 and s % TK == 0 and TQ == TK and d % LANES == 0
    consts = _constants(min(TM_PROJ, s))
    tm_proj = min(TM_PROJ, s)
    pos3 = positions.reshape(b, s // tm_proj, 1, tm_proj)
    scale = HEAD_DIM ** -0.5
    idx_scale = IDX_DIM ** -0.5

    splits = (W_FOX, W_FOX, W_FOX, N_HEADS_FOX, W_SB, W_SB, W_SB, W_DSA, W_DSA, W_DSA,
              W_IDX, IDX_DIM, N_IDX_HEADS, d, d, d)
    offs = np.concatenate([[0], np.cumsum(splits)]).tolist()
    (o_qf, o_kf, o_vf, o_ff, o_qs, o_ks, o_vs, o_qc, o_kc, o_vc, o_qi, o_ki, o_wi, o_ga) = offs[:14]

    x2 = x.reshape(b * s, d)
    for l in range(depth):
        wl = w_in[l].astype(BF16)
        pad = jnp.zeros((d, LANES - IDX_DIM - N_HEADS_FOX - N_IDX_HEADS), BF16)
        w_proj = jnp.concatenate([
            wl[:, o_qf:o_ff], wl[:, o_qs:o_qi], wl[:, o_qi:o_ki],
            wl[:, o_ki:o_wi], wl[:, o_ff:o_qs], wl[:, o_wi:o_ga], pad], axis=1)
        w_gates = wl[:, o_ga:]
        ones = lambda n: jnp.ones((n,), F32)
        gains = jnp.concatenate([
            _tile_heads(q_norm_fox[l], N_HEADS_FOX) * scale, _tile_heads(k_norm_fox[l], N_HEADS_FOX), ones(W_FOX),
            _tile_heads(q_norm_sb[l], N_HEADS_SB) * scale, _tile_heads(k_norm_sb[l], N_HEADS_SB), ones(W_SB),
            _tile_heads(q_norm_dsa[l], N_HEADS_DSA) * scale, _tile_heads(k_norm_dsa[l], N_HEADS_DSA), ones(W_DSA),
            ones(W_IDX) * idx_scale, ones(LANES)]).reshape(1, N_PROJ)
        fbias = jnp.zeros((LANES,), F32).at[L_FORGET:L_FORGET + N_HEADS_FOX].set(b_forget[l].astype(F32))
        fbias = fbias.reshape(1, LANES)

        x2 = _ffn(x2, ffn1_norm[l].reshape(1, d), ffn1_w_gate[l].astype(BF16), ffn1_w_up[l].astype(BF16),
                  ffn1_w_down[l].astype(BF16))
        zb, vt_f, vt_s, vt_c, misct = _proj(x2.reshape(b, s, d), pos3, mix_norm[l].reshape(1, d), w_proj, gains,
                                            consts, fbias)
        o_f = _fox(zb, vt_f)
        o_s = _sb(zb, vt_s, consts["tri_ge"])
        o_c = _dsa(zb, vt_c, misct, consts["tri_lt"], topk)
        x2 = _merge(x2, mix_norm[l].reshape(1, d), o_f.reshape(b * s, W_FOX), o_s.reshape(b * s, W_SB),
                    o_c.reshape(b * s, W_DSA), w_gates, b_gates[l].astype(F32),
                    w_branch_fox[l].astype(BF16), w_branch_sb[l].astype(BF16), w_branch_dsa[l].astype(BF16),
                    w_out[l].astype(BF16))
        x2 = _ffn(x2, ffn2_norm[l].reshape(1, d), ffn2_w_gate[l].astype(BF16), ffn2_w_up[l].astype(BF16),
                  ffn2_w_down[l].astype(BF16))
    return x2.reshape(b, s, d)
```

```python
import functools

import jax
import jax.numpy as jnp
import numpy as np
from jax import lax
from jax.experimental import pallas as pl
from jax.experimental.pallas import tpu as pltpu

F32 = jnp.float32
BF16 = jnp.bfloat16

HEAD_DIM = 64
N_HEADS_FOX = 6
N_HEADS_SB = 6
N_HEADS_DSA = 4
N_IDX_HEADS = 4
IDX_DIM = 64
TOPK_MAX = 256
ROPE_THETA = 500000.0
ROT_DIM = HEAD_DIM // 4
EPS = 1e-6

LANES = 128
SUBLANES = 8
PACK16 = 2 * SUBLANES
N_COUNT_ACC = 4
W_FOX = N_HEADS_FOX * HEAD_DIM
W_SB = N_HEADS_SB * HEAD_DIM
W_DSA = N_HEADS_DSA * HEAD_DIM
W_IDX = N_IDX_HEADS * IDX_DIM

C_FQ, C_FK, C_FV = 0, W_FOX, 2 * W_FOX
C_SQ, C_SK, C_SV = 3 * W_FOX, 3 * W_FOX + W_SB, 3 * W_FOX + 2 * W_SB
C_CQ = 3 * W_FOX + 3 * W_SB
C_CK, C_CV = C_CQ + W_DSA, C_CQ + 2 * W_DSA
C_IQ = C_CQ + 3 * W_DSA
C_MISC = C_IQ + W_IDX
N_PROJ = C_MISC + LANES
L_FORGET = IDX_DIM
L_WIDX = IDX_DIM + N_HEADS_FOX
MISC_ROWS = 16

W_FOX_AUG = N_HEADS_FOX * LANES
Z_FQ, Z_FK = 0, W_FOX_AUG
Z_SQ, Z_SK = 2 * W_FOX_AUG, 2 * W_FOX_AUG + W_SB
Z_CQ = 2 * W_FOX_AUG + 2 * W_SB
Z_CK = Z_CQ + W_DSA
Z_IQ = Z_CK + W_DSA
Z_IK = Z_IQ + W_IDX
N_ZB = Z_IK + LANES
N_BIAS_FEATS = 3

TQ = 256
TK = 256
TM_PROJ = 512
TM_FFN = 512
TM_MERGE = 512
NEG_BIG = -1e30
EXP_UNDERFLOW = -110.0
VMEM_LIMIT = 56 * 1024 * 1024

_NT = (((1,), (1,)), ((), ()))


def _dot(a, b):
    return jnp.dot(a, b, preferred_element_type=F32)


def _dot_nt(a, b):
    return lax.dot_general(a, b, _NT, preferred_element_type=F32)


def _split2(x):
    hi = x.astype(BF16)
    lo = (x - hi.astype(F32)).astype(BF16)
    return hi, lo


def _split3(x):
    hi = x.astype(BF16)
    r = x - hi.astype(F32)
    mid = r.astype(BF16)
    lo = (r - mid.astype(F32)).astype(BF16)
    return hi, mid, lo


def _rms_rows(x, g):
    ms = jnp.mean(x * x, axis=-1, keepdims=True)
    return x * lax.rsqrt(ms + EPS) * g


def _softplus(z):
    return jnp.maximum(z, 0.0) + jnp.log(1.0 + jnp.exp(-jnp.abs(z)))


def _ffn_body(x_ref, g_ref, wg_ref, wu_ref, wd_ref, o_ref, *, tf):
    x = x_ref[...]
    h = _rms_rows(x, g_ref[...]).astype(BF16)
    acc = None
    for c in range(0, wg_ref.shape[1], tf):
        a = _dot(h, wg_ref[:, c:c + tf])
        u = _dot(h, wu_ref[:, c:c + tf])
        p = (a * (1.0 / (1.0 + jnp.exp(-a))) * u).astype(BF16)
        part = _dot(p, wd_ref[c:c + tf, :])
        acc = part if acc is None else acc + part
    o_ref[...] = x + 0.5 * acc


def _ffn_tile_f(d_ff):
    best = LANES
    for t in range(LANES, d_ff + 1, LANES):
        if d_ff % t == 0 and t <= 1536:
            best = t
    return best


def _resident(shape):
    return pl.BlockSpec(shape, lambda *_: (0,) * len(shape), pipeline_mode=pl.Buffered(1))


def _ffn(x2, g, wg, wu, wd):
    m, d = x2.shape
    d_ff = wg.shape[1]
    tm = min(TM_FFN, m)
    return pl.pallas_call(
        functools.partial(_ffn_body, tf=_ffn_tile_f(d_ff)),
        grid=(m // tm,),
        in_specs=[
            pl.BlockSpec((tm, d), lambda i: (i, 0)),
            _resident((1, d)), _resident((d, d_ff)), _resident((d, d_ff)), _resident((d_ff, d)),
        ],
        out_specs=pl.BlockSpec((tm, d), lambda i: (i, 0)),
        out_shape=jax.ShapeDtypeStruct((m, d), F32),
        compiler_params=pltpu.CompilerParams(
            dimension_semantics=("parallel",), vmem_limit_bytes=VMEM_LIMIT),
        name="ffn",
    )(x2, g, wg, wu, wd)


def _proj_body(x_ref, pos_ref, g_ref, w_ref, gains_ref, bd_ref, freq_ref, fbias_ref, tri_ref,
               pq_ref, pk_ref, ones_q_ref, ones_k_ref,
               zb_ref, vtf_ref, vts_ref, vtc_ref, misct_ref, carry_scr):
    t = pl.program_id(1)
    tm = x_ref.shape[1]
    h = _rms_rows(x_ref[0], g_ref[...]).astype(BF16)

    lane = lax.broadcasted_iota(jnp.int32, (1, LANES), 1)
    d_in_head = lane % HEAD_DIM
    pos_cols = jnp.broadcast_to(pos_ref[0, 0].astype(F32), (LANES, tm)).T
    ang = pos_cols * freq_ref[...]
    cos = jnp.cos(ang)
    sin_signed = jnp.where(d_in_head < ROT_DIM // 2, -jnp.sin(ang), jnp.sin(ang))
    first_half = d_in_head < ROT_DIM // 2

    def rope(x):
        partner = jnp.where(first_half, pltpu.roll(x, LANES - ROT_DIM // 2, 1), pltpu.roll(x, ROT_DIM // 2, 1))
        return x * cos + partner * sin_signed

    def head_norm(z, c0, width):
        hi, lo = _split2(z * z)
        bd = bd_ref[:width, :width]
        ss = _dot(hi, bd) + _dot(lo, bd)
        return z * lax.rsqrt(ss * (1.0 / HEAD_DIM) + EPS) * gains_ref[:, c0:c0 + width]

    def group(c0, width, norm, rot):
        z = _dot(h, w_ref[:, c0:c0 + width])
        if norm:
            z = head_norm(z, c0, width)
        else:
            z = z * gains_ref[:, c0:c0 + width]
        if rot:
            z = jnp.concatenate([rope(z[:, c:c + LANES]) for c in range(0, width, LANES)], axis=1)
        return z

    def store_vt(vt_ref, c0, width):
        zt = _dot(h, w_ref[:, c0:c0 + width]).T
        for p in range(width // LANES):
            for c in range(tm // TK):
                vt_ref[0, p, c] = zt[p * LANES:(p + 1) * LANES, c * TK:(c + 1) * TK].astype(BF16)

    zm = _dot(h, w_ref[:, C_MISC:C_MISC + LANES])
    ki = rope(zm)
    zb_ref[0, :, Z_IK:Z_IK + LANES] = jnp.where(lane < IDX_DIM, ki, pltpu.roll(ki, IDX_DIM, 1)).astype(BF16)

    @pl.when(t == 0)
    def _():
        carry_scr[...] = jnp.zeros_like(carry_scr)

    logf = -_softplus(-(zm + fbias_ref[...]))
    hi, mid, lo = _split3(logf)
    tri = tri_ref[...]
    cum = _dot(tri, hi) + _dot(tri, mid) + _dot(tri, lo) + carry_scr[0:1, :]
    carry_scr[...] = jnp.broadcast_to(cum[tm - 1:tm, :], carry_scr.shape)

    is_forget = (lane >= L_FORGET) & (lane < L_FORGET + N_HEADS_FOX)
    is_widx = (lane >= L_WIDX) & (lane < L_WIDX + N_IDX_HEADS)
    misc = jnp.where(is_forget, cum, jnp.where(is_widx, zm * (N_IDX_HEADS ** -0.5), 0.0))
    misc_t = misc.T
    for c in range(tm // TQ):
        misct_ref[0, c] = misc_t[L_FORGET:L_FORGET + MISC_ROWS, c * TQ:(c + 1) * TQ]

    c_hi, c_mid, c_lo = _split3(cum)
    for c0, z0, p_ref, ones_ref in ((C_FQ, Z_FQ, pq_ref, ones_q_ref), (C_FK, Z_FK, pk_ref, ones_k_ref)):
        zn = group(c0, W_FOX, True, False).astype(BF16)
        src = jnp.concatenate([zn, c_hi, c_mid, c_lo], axis=1)
        zb_ref[0, :, z0:z0 + W_FOX_AUG] = (_dot(src, p_ref[...]) + ones_ref[...]).astype(BF16)

    zb_ref[0, :, Z_SQ:Z_SQ + W_SB] = group(C_SQ, W_SB, True, False).astype(BF16)
    zb_ref[0, :, Z_SK:Z_SK + W_SB] = group(C_SK, W_SB, True, False).astype(BF16)
    zb_ref[0, :, Z_CQ:Z_CQ + W_DSA] = group(C_CQ, W_DSA, True, True).astype(BF16)
    zb_ref[0, :, Z_CK:Z_CK + W_DSA] = group(C_CK, W_DSA, True, True).astype(BF16)
    zb_ref[0, :, Z_IQ:Z_IQ + W_IDX] = group(C_IQ, W_IDX, False, True).astype(BF16)
    store_vt(vtf_ref, C_FV, W_FOX)
    store_vt(vts_ref, C_SV, W_SB)
    store_vt(vtc_ref, C_CV, W_DSA)


def _proj(x, pos3, g, w, gains, consts, fbias):
    b, s, d = x.shape
    tm = min(TM_PROJ, s)
    const = lambda a: pl.BlockSpec(a.shape, lambda bi, ti: (0,) * a.ndim)
    vt_spec = lambda n: pl.BlockSpec((1, n, tm // TK, LANES, TK), lambda bi, ti: (bi, 0, ti, 0, 0))
    vt_shape = lambda n: jax.ShapeDtypeStruct((b, n, s // TK, LANES, TK), BF16)
    cs = (consts["bd"], consts["freq"], fbias, consts["tri_tok"], consts["pq"], consts["pk"],
          consts["ones_q"], consts["ones_k"])
    return pl.pallas_call(
        _proj_body,
        grid=(b, s // tm),
        in_specs=[
            pl.BlockSpec((1, tm, d), lambda bi, ti: (bi, ti, 0)),
            pl.BlockSpec((1, 1, 1, tm), lambda bi, ti: (bi, ti, 0, 0)),
            const(g), const(w), const(gains),
        ] + [const(a) for a in cs],
        out_specs=[
            pl.BlockSpec((1, tm, N_ZB), lambda bi, ti: (bi, ti, 0)),
            vt_spec(W_FOX // LANES), vt_spec(W_SB // LANES), vt_spec(W_DSA // LANES),
            pl.BlockSpec((1, tm // TQ, MISC_ROWS, TQ), lambda bi, ti: (bi, ti, 0, 0)),
        ],
        out_shape=[
            jax.ShapeDtypeStruct((b, s, N_ZB), BF16),
            vt_shape(W_FOX // LANES), vt_shape(W_SB // LANES), vt_shape(W_DSA // LANES),
            jax.ShapeDtypeStruct((b, s // TQ, MISC_ROWS, TQ), F32),
        ],
        scratch_shapes=[pltpu.VMEM((SUBLANES, LANES), F32)],
        compiler_params=pltpu.CompilerParams(
            dimension_semantics=("parallel", "arbitrary"), vmem_limit_bytes=VMEM_LIMIT),
        name="proj",
    )(x, pos3, g, w, gains, *cs)


def _head_masks():
    lane = lax.broadcasted_iota(jnp.int32, (1, LANES), 1)
    return lane < HEAD_DIM, lane >= HEAD_DIM


def _split_heads(q128):
    lo_half, hi_half = _head_masks()
    zero = jnp.zeros_like(q128)
    return jnp.where(lo_half, q128, zero), jnp.where(hi_half, q128, zero)


def _visible(tk, tq, strict):
    key = lax.broadcasted_iota(jnp.int32, (tk, tq), 0)
    qry = lax.broadcasted_iota(jnp.int32, (tk, tq), 1)
    return key < qry if strict else key <= qry


def _softmax_update(s, m, l):
    m_new = jnp.maximum(m, jnp.max(s, axis=0, keepdims=True))
    alpha = jnp.exp(m - m_new)
    p = jnp.exp(s - m_new)
    return p, alpha, m_new, alpha * l + jnp.sum(p, axis=0, keepdims=True)


def _head_rows(hd):
    return slice(hd * HEAD_DIM, (hd + 1) * HEAD_DIM)


def _walk_blocks(n_blocks, step, state):
    last = n_blocks - 1
    state = lax.fori_loop(0, last // 2, lambda n, st: step(2 * n, 2, False, st), state)
    return lax.cond(last % 2 == 1,
                    lambda st: step(last - 1, 2, True, st),
                    lambda st: step(last, 1, True, st),
                    state)


def _key_rows(ref, j0, count):
    return ref[0, pl.ds(pl.multiple_of(j0 * TK, TK), count * TK), :]


def _fox_body(q_ref, k_ref, vt_ref, o_ref, acc_scr):
    i = pl.program_id(1)
    tq = q_ref.shape[1]
    nh = N_HEADS_FOX
    q_heads = [q_ref[0, :, hd * LANES:(hd + 1) * LANES] for hd in range(nh)]
    acc_scr[...] = jnp.zeros_like(acc_scr)

    def step(j0, count, ends, state):
        kb = _key_rows(k_ref, j0, count)
        scores = [_dot_nt(kb[:, hd * LANES:(hd + 1) * LANES], q_heads[hd]) for hd in range(nh)]
        weights = []
        for b in range(count):
            s = [sh[b * TK:(b + 1) * TK, :] for sh in scores]
            if ends and b == count - 1:
                vis = _visible(TK, tq, False)
                s = [jnp.where(vis, sh, -jnp.inf) for sh in s]
            upd = [_softmax_update(s[hd], *state[hd]) for hd in range(nh)]
            state = tuple((u[2], u[3]) for u in upd)
            weights.append([(u[0], u[1]) for u in upd])
        accs = [acc_scr[hd // 2, _head_rows(hd % 2), :] for hd in range(nh)]
        for b in range(count):
            for hd in range(nh):
                p, alpha = weights[b][hd]
                accs[hd] = alpha * accs[hd] + _dot(vt_ref[0, hd // 2, j0 + b, _head_rows(hd % 2), :], p.astype(BF16))
        for hd in range(nh):
            acc_scr[hd // 2, _head_rows(hd % 2), :] = accs[hd]
        return state

    init = tuple((jnp.full((1, tq), -jnp.inf, F32), jnp.zeros((1, tq), F32)) for _ in range(nh))
    state = _walk_blocks(i + 1, step, init)
    for pr in range(nh // 2):
        inv = jnp.concatenate([jnp.broadcast_to(1.0 / state[2 * pr + hd][1], (HEAD_DIM, tq)) for hd in range(2)],
                              axis=0)
        o_ref[0, :, pr * LANES:(pr + 1) * LANES] = (acc_scr[pr] * inv).T.astype(BF16)


def _fox(zb, vt):
    b, s, _ = zb.shape
    nq, nk = s // TQ, s // TK
    n_pairs = W_FOX // LANES
    return pl.pallas_call(
        _fox_body,
        grid=(b, nq),
        in_specs=[
            pl.BlockSpec((1, TQ, W_FOX_AUG), lambda bi, i: (bi, i, Z_FQ // W_FOX_AUG)),
            pl.BlockSpec((1, s, W_FOX_AUG), lambda bi, i: (bi, 0, Z_FK // W_FOX_AUG)),
            pl.BlockSpec((1, n_pairs, nk, LANES, TK), lambda bi, i: (bi, 0, 0, 0, 0)),
        ],
        out_specs=pl.BlockSpec((1, TQ, W_FOX), lambda bi, i: (bi, i, 0)),
        out_shape=jax.ShapeDtypeStruct((b, s, W_FOX), BF16),
        scratch_shapes=[pltpu.VMEM((n_pairs, LANES, TQ), F32)],
        compiler_params=pltpu.CompilerParams(
            dimension_semantics=("parallel", "arbitrary"), vmem_limit_bytes=VMEM_LIMIT),
        name="fox",
    )(zb, zb, vt)


def _sb_body(q_ref, k_ref, vt_ref, tri_ref, o_ref, acc_scr):
    i = pl.program_id(1)
    tq = q_ref.shape[1]
    nh = N_HEADS_SB
    q_heads = []
    for pr in range(nh // 2):
        q_heads.extend(_split_heads(q_ref[0, :, pr * LANES:(pr + 1) * LANES]))
    acc_scr[...] = jnp.zeros_like(acc_scr)

    def step(j, carry, masked):
        kb = k_ref[0, pl.ds(pl.multiple_of(j * TK, TK), TK), :]
        tri = tri_ref[...]
        zs = [_dot_nt(kb[:, (hd // 2) * LANES:(hd // 2 + 1) * LANES], q_heads[hd]) for hd in range(nh)]
        loms = [-_softplus(z) for z in zs]
        if masked:
            strict = _visible(TK, tq, True)
            loms = [jnp.where(strict, lom, 0.0) for lom in loms]
        splits = [_split2(lom) for lom in loms]
        suffixes = [_dot(tri, hi) + _dot(tri, lo) for hi, lo in splits]
        new_carry = []
        for hd in range(nh):
            a = jnp.exp(zs[hd] + suffixes[hd] + carry[hd])
            if masked:
                a = jnp.where(strict, a, 0.0)
            pr, rows = hd // 2, _head_rows(hd % 2)
            acc_scr[pr, rows, :] = acc_scr[pr, rows, :] + _dot(vt_ref[0, pr, j, rows, :], a.astype(BF16))
            new_carry.append(carry[hd] + suffixes[hd][0:1, :])
        return tuple(new_carry)

    carry = step(i, tuple(jnp.zeros((1, tq), F32) for _ in range(nh)), True)

    def live(state):
        n, carry = state
        worst = functools.reduce(jnp.maximum, carry)
        return jnp.logical_and(n < i, jnp.max(worst) > EXP_UNDERFLOW)

    def body(state):
        n, carry = state
        return n + 1, step(i - 1 - n, carry, False)

    lax.while_loop(live, body, (jnp.int32(0), carry))
    for pr in range(nh // 2):
        o_ref[0, :, pr * LANES:(pr + 1) * LANES] = acc_scr[pr].T.astype(BF16)


def _sb(zb, vt, tri_ge):
    b, s, _ = zb.shape
    nq, nk = s // TQ, s // TK
    n_pairs = W_SB // LANES
    return pl.pallas_call(
        _sb_body,
        grid=(b, nq),
        in_specs=[
            pl.BlockSpec((1, TQ, W_SB), lambda bi, i: (bi, i, Z_SQ // W_SB)),
            pl.BlockSpec((1, s, W_SB), lambda bi, i: (bi, 0, Z_SK // W_SB)),
            pl.BlockSpec((1, n_pairs, nk, LANES, TK), lambda bi, i: (bi, 0, 0, 0, 0)),
            pl.BlockSpec((TK, TK), lambda bi, i: (0, 0)),
        ],
        out_specs=pl.BlockSpec((1, TQ, W_SB), lambda bi, i: (bi, i, 0)),
        out_shape=jax.ShapeDtypeStruct((b, s, W_SB), BF16),
        scratch_shapes=[pltpu.VMEM((n_pairs, LANES, TQ), F32)],
        compiler_params=pltpu.CompilerParams(
            dimension_semantics=("parallel", "arbitrary"), vmem_limit_bytes=VMEM_LIMIT),
        name="sb",
    )(zb, zb, vt, tri_ge)


def _dsa_body(iq_ref, ik_ref, q_ref, k_ref, vt_ref, misct_ref, tri_ref, o_ref,
              keys_scr, hi_scr, lo_scr, bias_scr, acc_scr, *, topk):
    i = pl.program_id(1)
    tq = q_ref.shape[1]
    n_pairs = N_HEADS_DSA // 2
    i16_min, i16_max = -2 ** 15, 2 ** 15 - 1

    iq = iq_ref[0]
    iq_heads = _split_heads(iq[:, 0:LANES]) + _split_heads(iq[:, LANES:2 * LANES])
    w_rows = [misct_ref[0, 0, L_WIDX - L_FORGET + hh:L_WIDX - L_FORGET + hh + 1, :] for hh in range(N_IDX_HEADS)]

    def score_step(j0, count, ends, carry):
        kb = _key_rows(ik_ref, j0, count)
        dots = [_dot_nt(kb, iq_heads[hh]) for hh in range(N_IDX_HEADS)]
        for b in range(count):
            rows = slice(b * TK, (b + 1) * TK)
            sc = jnp.zeros((TK, tq), F32)
            for hh in range(N_IDX_HEADS):
                sc = sc + w_rows[hh] * jnp.maximum(dots[hh][rows, :], 0.0)
            if ends and b == count - 1:
                sc = jnp.where(_visible(TK, tq, False), sc, -jnp.inf)
            bits = lax.bitcast_convert_type(sc, jnp.int32)
            key = jnp.where(bits < 0, bits ^ jnp.int32(0x7FFFFFFF), bits)
            keys_scr[j0 + b] = key
            hi_scr[j0 + b] = lax.shift_right_arithmetic(key, 16).astype(jnp.int16)
            lo_scr[j0 + b] = ((key & 0xFFFF) - 2 ** 15).astype(jnp.int16)
        return carry

    _walk_blocks(i + 1, score_step, 0)

    def count16(ref, cand):
        cand_b = jnp.broadcast_to(cand.astype(jnp.int16), (PACK16, tq))
        one, zero = jnp.ones((), jnp.int16), jnp.zeros((), jnp.int16)

        def body(j, accs):
            c = jnp.where(ref[j].reshape(TK // PACK16, PACK16, tq) >= cand_b, one, zero)
            out = []
            for a in range(N_COUNT_ACC):
                g = [c[r] for r in range(a, TK // PACK16, N_COUNT_ACC)]
                while len(g) > 1:
                    g = [g[r] + g[r + 1] for r in range(0, len(g), 2)]
                out.append(accs[a] + g[0])
            return tuple(out)

        zeros = jnp.zeros((PACK16, tq), jnp.int16)
        accs = lax.fori_loop(0, i + 1, body, (zeros,) * N_COUNT_ACC)
        acc = functools.reduce(lambda x, y: x + y, accs)
        return jnp.sum(acc.astype(jnp.int32).astype(F32), axis=0, keepdims=True)

    def count16_gt(ref, thr16):
        return jnp.where(thr16 == i16_max, 0.0, count16(ref, jnp.minimum(thr16 + 1, i16_max)))

    def bisect16(ref, want):
        def it(n, thr16):
            cand = thr16 + lax.shift_left(jnp.int32(1), 15 - n)
            return jnp.where(count16(ref, cand) >= want, cand, thr16)
        return lax.fori_loop(0, 16, it, jnp.full((1, tq), i16_min, jnp.int32))

    kf = jnp.float32(topk)
    t_hi = bisect16(hi_scr, kf)
    n_gt_hi = count16_gt(hi_scr, t_hi)
    t_hi_b = jnp.broadcast_to(t_hi.astype(jnp.int16), (PACK16, tq))

    def low_body(j, c):
        shape = (TK // PACK16, PACK16, tq)
        lo = jnp.where(hi_scr[j].reshape(shape) == t_hi_b, lo_scr[j].reshape(shape), jnp.int16(i16_min))
        lo_scr[j] = lo.reshape(TK, tq)
        return c

    lax.fori_loop(0, i + 1, low_body, 0)
    t_lo = bisect16(lo_scr, kf - n_gt_hi)
    thr = t_hi * 2 ** 16 + t_lo + 2 ** 15
    n_gt_lo = count16_gt(lo_scr, t_lo)
    n_eq = count16(lo_scr, t_lo) - n_gt_lo
    need = kf - n_gt_hi - n_gt_lo

    def plain_step(j, masked):
        sel = keys_scr[j] >= thr
        if masked:
            sel = sel & _visible(TK, tq, False)
        bias_scr[j] = jnp.where(sel, 0.0, NEG_BIG)

    def tie_step(j, run, masked):
        kblk = keys_scr[j]
        eq = kblk == thr
        eqf = jnp.where(eq, 1.0, 0.0)
        before = _dot(tri_ref[...], eqf.astype(BF16)) + run
        sel = (kblk > thr) | (eq & (before < need))
        if masked:
            sel = sel & _visible(TK, tq, False)
        bias_scr[j] = jnp.where(sel, 0.0, NEG_BIG)
        return run + jnp.sum(eqf, axis=0, keepdims=True)

    def plain_bias():
        def body(j, c):
            plain_step(j, False)
            return c
        lax.fori_loop(0, i, body, 0)
        plain_step(i, True)

    def tie_bias():
        run = lax.fori_loop(0, i, lambda j, r: tie_step(j, r, False), jnp.zeros((1, tq), F32))
        tie_step(i, run, True)

    lax.cond(jnp.max(n_eq - need) > 0.0, tie_bias, plain_bias)

    q = q_ref[0]
    q_heads = _split_heads(q[:, 0:LANES]) + _split_heads(q[:, LANES:2 * LANES])
    acc_scr[...] = jnp.zeros_like(acc_scr)

    def attn_step(j0, count, ends, state):
        kb = _key_rows(k_ref, j0, count)
        scores = [_dot_nt(kb[:, (hh // 2) * LANES:(hh // 2 + 1) * LANES], q_heads[hh]) for hh in range(N_HEADS_DSA)]
        weights = []
        for b in range(count):
            bias = bias_scr[j0 + b]
            upd = [_softmax_update(scores[hh][b * TK:(b + 1) * TK, :] + bias, *state[hh])
                   for hh in range(N_HEADS_DSA)]
            state = tuple((u[2], u[3]) for u in upd)
            weights.append([(u[0], u[1]) for u in upd])
        accs = [acc_scr[hh // 2, _head_rows(hh % 2), :] for hh in range(N_HEADS_DSA)]
        for b in range(count):
            for hh in range(N_HEADS_DSA):
                p, alpha = weights[b][hh]
                accs[hh] = alpha * accs[hh] + _dot(vt_ref[0, hh // 2, j0 + b, _head_rows(hh % 2), :],
                                                   p.astype(BF16))
        for hh in range(N_HEADS_DSA):
            acc_scr[hh // 2, _head_rows(hh % 2), :] = accs[hh]
        return state

    init = tuple((jnp.full((1, tq), -jnp.inf, F32), jnp.zeros((1, tq), F32)) for _ in range(N_HEADS_DSA))
    state = _walk_blocks(i + 1, attn_step, init)
    outs = []
    for pr in range(n_pairs):
        inv = jnp.concatenate([jnp.broadcast_to(1.0 / state[2 * pr + hd][1], (HEAD_DIM, tq)) for hd in range(2)],
                              axis=0)
        outs.append((acc_scr[pr] * inv).T)
    o_ref[0] = jnp.concatenate(outs, axis=1).astype(BF16)


def _dsa(zb, vt, misct, tri_lt, topk):
    b, s, _ = zb.shape
    nq, nk = s // TQ, s // TK
    n_pairs = W_DSA // LANES
    return pl.pallas_call(
        functools.partial(_dsa_body, topk=topk),
        grid=(b, nq),
        in_specs=[
            pl.BlockSpec((1, TQ, W_IDX), lambda bi, i: (bi, i, Z_IQ // W_IDX)),
            pl.BlockSpec((1, s, LANES), lambda bi, i: (bi, 0, Z_IK // LANES)),
            pl.BlockSpec((1, TQ, W_DSA), lambda bi, i: (bi, i, Z_CQ // W_DSA)),
            pl.BlockSpec((1, s, W_DSA), lambda bi, i: (bi, 0, Z_CK // W_DSA)),
            pl.BlockSpec((1, n_pairs, nk, LANES, TK), lambda bi, i: (bi, 0, 0, 0, 0)),
            pl.BlockSpec((1, 1, MISC_ROWS, TQ), lambda bi, i: (bi, i, 0, 0)),
            pl.BlockSpec((TK, TK), lambda bi, i: (0, 0)),
        ],
        out_specs=pl.BlockSpec((1, TQ, W_DSA), lambda bi, i: (bi, i, 0)),
        out_shape=jax.ShapeDtypeStruct((b, s, W_DSA), BF16),
        scratch_shapes=[
            pltpu.VMEM((nk, TK, TQ), jnp.int32),
            pltpu.VMEM((nk, TK, TQ), jnp.int16),
            pltpu.VMEM((nk, TK, TQ), jnp.int16),
            pltpu.VMEM((nk, TK, TQ), F32),
            pltpu.VMEM((n_pairs, LANES, TQ), F32),
        ],
        compiler_params=pltpu.CompilerParams(
            dimension_semantics=("parallel", "arbitrary"), vmem_limit_bytes=VMEM_LIMIT),
        name="dsa",
    )(zb, zb, zb, zb, vt, misct, tri_lt)


def _merge_body(x_ref, g_ref, of_ref, os_ref, oc_ref, wg_ref, bg_ref, wf_ref, ws_ref, wc_ref, wo_ref, o_ref):
    x = x_ref[...]
    d = x.shape[1]
    h = _rms_rows(x, g_ref[...]).astype(BF16)
    merged = jnp.zeros_like(x)
    for n, (o_br, w_br) in enumerate(((of_ref, wf_ref), (os_ref, ws_ref), (oc_ref, wc_ref))):
        gate = _dot(h, wg_ref[:, n * d:(n + 1) * d]) + bg_ref[n:n + 1, :]
        gate = 1.0 / (1.0 + jnp.exp(-gate))
        merged = merged + gate * _dot(o_br[...], w_br[...])
    o_ref[...] = x + _dot(merged.astype(BF16), wo_ref[...])


def _merge(x2, g, o_f, o_s, o_c, w_gates, b_gates, w_f, w_s, w_c, w_o):
    m, d = x2.shape
    tm = min(TM_MERGE, m)
    row = lambda w: pl.BlockSpec((tm, w), lambda i: (i, 0))
    const = lambda a: pl.BlockSpec(a.shape, lambda i: (0, 0))
    return pl.pallas_call(
        _merge_body,
        grid=(m // tm,),
        in_specs=[row(d), const(g), row(W_FOX), row(W_SB), row(W_DSA), const(w_gates), const(b_gates),
                  const(w_f), const(w_s), const(w_c), const(w_o)],
        out_specs=row(d),
        out_shape=jax.ShapeDtypeStruct((m, d), F32),
        compiler_params=pltpu.CompilerParams(
            dimension_semantics=("parallel",), vmem_limit_bytes=VMEM_LIMIT),
        name="merge",
    )(x2, g, o_f, o_s, o_c, w_gates, b_gates, w_f, w_s, w_c, w_o)


def _tile_heads(g, n):
    return jnp.tile(g.astype(F32), n)


def _constants(tm_proj):
    r = np.arange(W_FOX)
    bd = (r[:, None] // HEAD_DIM == r[None, :] // HEAD_DIM).astype(np.float32)
    rt = np.arange(tm_proj)
    tri_tok = (rt[None, :] <= rt[:, None]).astype(np.float32)
    rk = np.arange(TK)
    tri_ge = (rk[None, :] >= rk[:, None]).astype(np.float32)
    tri_lt = (rk[None, :] < rk[:, None]).astype(np.float32)

    n_src = W_FOX + N_BIAS_FEATS * LANES
    pq = np.zeros((n_src, W_FOX_AUG), np.float32)
    pk = np.zeros((n_src, W_FOX_AUG), np.float32)
    ones_q = np.zeros((1, W_FOX_AUG), np.float32)
    ones_k = np.zeros((1, W_FOX_AUG), np.float32)
    for hd in range(N_HEADS_FOX):
        for dd in range(HEAD_DIM):
            pq[hd * HEAD_DIM + dd, hd * LANES + dd] = 1.0
            pk[hd * HEAD_DIM + dd, hd * LANES + dd] = 1.0
        for part in range(N_BIAS_FEATS):
            src_row = W_FOX + part * LANES + L_FORGET + hd
            pq[src_row, hd * LANES + HEAD_DIM + part] = 1.0
            ones_q[0, hd * LANES + HEAD_DIM + N_BIAS_FEATS + part] = 1.0
            ones_k[0, hd * LANES + HEAD_DIM + part] = 1.0
            pk[src_row, hd * LANES + HEAD_DIM + N_BIAS_FEATS + part] = -1.0
    half = ROT_DIM // 2
    inv_freq = jnp.power(ROPE_THETA, -jnp.arange(half, dtype=F32) * 2.0 / ROT_DIM)
    dlane = np.arange(LANES) % HEAD_DIM
    freq = jnp.where(jnp.asarray(dlane < ROT_DIM), inv_freq[jnp.asarray(dlane % half)], 0.0)
    return dict(bd=jnp.asarray(bd, BF16), tri_tok=jnp.asarray(tri_tok, BF16), tri_ge=jnp.asarray(tri_ge, BF16),
                tri_lt=jnp.asarray(tri_lt, BF16), pq=jnp.asarray(pq, BF16), pk=jnp.asarray(pk, BF16),
                ones_q=jnp.asarray(ones_q, F32), ones_k=jnp.asarray(ones_k, F32),
                freq=freq.reshape(1, LANES).astype(F32))


def kernel(x, positions, ffn1_norm, ffn1_w_gate, ffn1_w_up, ffn1_w_down, mix_norm, w_in, b_forget, b_gates, q_norm_fox, k_norm_fox, q_norm_sb, k_norm_sb, q_norm_dsa, k_norm_dsa, w_branch_fox, w_branch_sb, w_branch_dsa, w_out, ffn2_norm, ffn2_w_gate, ffn2_w_up, ffn2_w_down):
    b, s, d = x.shape
    depth = w_in.shape[0]
    topk = min(TOPK_MAX, s // 4)
    assert s % TQ == 0 and s % TK == 0 and TQ == TK and d % LANES == 0
    consts = _constants(min(TM_PROJ, s))
    tm_proj = min(TM_PROJ, s)
    pos3 = positions.reshape(b, s // tm_proj, 1, tm_proj)
    scale = HEAD_DIM ** -0.5
    idx_scale = IDX_DIM ** -0.5

    splits = (W_FOX, W_FOX, W_FOX, N_HEADS_FOX, W_SB, W_SB, W_SB, W_DSA, W_DSA, W_DSA,
              W_IDX, IDX_DIM, N_IDX_HEADS, d, d, d)
    offs = np.concatenate([[0], np.cumsum(splits)]).tolist()
    (o_qf, o_kf, o_vf, o_ff, o_qs, o_ks, o_vs, o_qc, o_kc, o_vc, o_qi, o_ki, o_wi, o_ga) = offs[:14]

    x2 = x.reshape(b * s, d)
    for l in range(depth):
        wl = w_in[l].astype(BF16)
        pad = jnp.zeros((d, LANES - IDX_DIM - N_HEADS_FOX - N_IDX_HEADS), BF16)
        w_proj = jnp.concatenate([
            wl[:, o_qf:o_ff], wl[:, o_qs:o_qi], wl[:, o_qi:o_ki],
            wl[:, o_ki:o_wi], wl[:, o_ff:o_qs], wl[:, o_wi:o_ga], pad], axis=1)
        w_gates = wl[:, o_ga:]
        ones = lambda n: jnp.ones((n,), F32)
        gains = jnp.concatenate([
            _tile_heads(q_norm_fox[l], N_HEADS_FOX) * scale, _tile_heads(k_norm_fox[l], N_HEADS_FOX), ones(W_FOX),
            _tile_heads(q_norm_sb[l], N_HEADS_SB) * scale, _tile_heads(k_norm_sb[l], N_HEADS_SB), ones(W_SB),
            _tile_heads(q_norm_dsa[l], N_HEADS_DSA) * scale, _tile_heads(k_norm_dsa[l], N_HEADS_DSA), ones(W_DSA),
            ones(W_IDX) * idx_scale, ones(LANES)]).reshape(1, N_PROJ)
        fbias = jnp.zeros((LANES,), F32).at[L_FORGET:L_FORGET + N_HEADS_FOX].set(b_forget[l].astype(F32))
        fbias = fbias.reshape(1, LANES)

        x2 = _ffn(x2, ffn1_norm[l].reshape(1, d), ffn1_w_gate[l].astype(BF16), ffn1_w_up[l].astype(BF16),
                  ffn1_w_down[l].astype(BF16))
        zb, vt_f, vt_s, vt_c, misct = _proj(x2.reshape(b, s, d), pos3, mix_norm[l].reshape(1, d), w_proj, gains,
                                            consts, fbias)
        o_f = _fox(zb, vt_f)
        o_s = _sb(zb, vt_s, consts["tri_ge"])
        o_c = _dsa(zb, vt_c, misct, consts["tri_lt"], topk)
        x2 = _merge(x2, mix_norm[l].reshape(1, d), o_f.reshape(b * s, W_FOX), o_s.reshape(b * s, W_SB),
                    o_c.reshape(b * s, W_DSA), w_gates, b_gates[l].astype(F32),
                    w_branch_fox[l].astype(BF16), w_branch_sb[l].astype(BF16), w_branch_dsa[l].astype(BF16),
                    w_out[l].astype(BF16))
        x2 = _ffn(x2, ffn2_norm[l].reshape(1, d), ffn2_w_gate[l].astype(BF16), ffn2_w_up[l].astype(BF16),
                  ffn2_w_down[l].astype(BF16))
    return x2.reshape(b, s, d)
```

```python
import functools

import jax
import jax.numpy as jnp
import numpy as np
from jax import lax
from jax.experimental import pallas as pl
from jax.experimental.pallas import tpu as pltpu

F32 = jnp.float32
BF16 = jnp.bfloat16

HEAD_DIM = 64
N_HEADS_FOX = 6
N_HEADS_SB = 6
N_HEADS_DSA = 4
N_IDX_HEADS = 4
IDX_DIM = 64
TOPK_MAX = 256
ROPE_THETA = 500000.0
ROT_DIM = HEAD_DIM // 4
EPS = 1e-6

LANES = 128
SUBLANES = 8
PACK16 = 2 * SUBLANES
N_COUNT_ACC = 4
W_FOX = N_HEADS_FOX * HEAD_DIM
W_SB = N_HEADS_SB * HEAD_DIM
W_DSA = N_HEADS_DSA * HEAD_DIM
W_IDX = N_IDX_HEADS * IDX_DIM

C_FQ, C_FK, C_FV = 0, W_FOX, 2 * W_FOX
C_SQ, C_SK, C_SV = 3 * W_FOX, 3 * W_FOX + W_SB, 3 * W_FOX + 2 * W_SB
C_CQ = 3 * W_FOX + 3 * W_SB
C_CK, C_CV = C_CQ + W_DSA, C_CQ + 2 * W_DSA
C_IQ = C_CQ + 3 * W_DSA
C_MISC = C_IQ + W_IDX
N_PROJ = C_MISC + LANES
L_FORGET = IDX_DIM
L_WIDX = IDX_DIM + N_HEADS_FOX
MISC_ROWS = 16

W_FOX_AUG = N_HEADS_FOX * LANES
Z_FQ, Z_FK = 0, W_FOX_AUG
Z_SQ, Z_SK = 2 * W_FOX_AUG, 2 * W_FOX_AUG + W_SB
Z_CQ = 2 * W_FOX_AUG + 2 * W_SB
Z_CK = Z_CQ + W_DSA
Z_IQ = Z_CK + W_DSA
Z_IK = Z_IQ + W_IDX
N_ZB = Z_IK + LANES
N_BIAS_FEATS = 3

TQ = 256
TK = 256
TM_PROJ = 512
TM_FFN = 512
TM_MERGE = 512
NEG_BIG = -1e30
EXP_UNDERFLOW = -110.0
VMEM_LIMIT = 56 * 1024 * 1024

_NT = (((1,), (1,)), ((), ()))


def _dot(a, b):
    return jnp.dot(a, b, preferred_element_type=F32)


def _dot_nt(a, b):
    return lax.dot_general(a, b, _NT, preferred_element_type=F32)


def _split2(x):
    hi = x.astype(BF16)
    lo = (x - hi.astype(F32)).astype(BF16)
    return hi, lo


def _split3(x):
    hi = x.astype(BF16)
    r = x - hi.astype(F32)
    mid = r.astype(BF16)
    lo = (r - mid.astype(F32)).astype(BF16)
    return hi, mid, lo


def _rms_rows(x, g):
    ms = jnp.mean(x * x, axis=-1, keepdims=True)
    return x * lax.rsqrt(ms + EPS) * g


def _softplus(z):
    return jnp.maximum(z, 0.0) + jnp.log(1.0 + jnp.exp(-jnp.abs(z)))


def _ffn_body(x_ref, g_ref, wg_ref, wu_ref, wd_ref, o_ref, *, tf):
    x = x_ref[...]
    h = _rms_rows(x, g_ref[...]).astype(BF16)
    acc = None
    for c in range(0, wg_ref.shape[1], tf):
        a = _dot(h, wg_ref[:, c:c + tf])
        u = _dot(h, wu_ref[:, c:c + tf])
        p = (a * (1.0 / (1.0 + jnp.exp(-a))) * u).astype(BF16)
        part = _dot(p, wd_ref[c:c + tf, :])
        acc = part if acc is None else acc + part
    o_ref[...] = x + 0.5 * acc


def _ffn_tile_f(d_ff):
    best = LANES
    for t in range(LANES, d_ff + 1, LANES):
        if d_ff % t == 0 and t <= 1536:
            best = t
    return best


def _resident(shape):
    return pl.BlockSpec(shape, lambda *_: (0,) * len(shape), pipeline_mode=pl.Buffered(1))


def _ffn(x2, g, wg, wu, wd):
    m, d = x2.shape
    d_ff = wg.shape[1]
    tm = min(TM_FFN, m)
    return pl.pallas_call(
        functools.partial(_ffn_body, tf=_ffn_tile_f(d_ff)),
        grid=(m // tm,),
        in_specs=[
            pl.BlockSpec((tm, d), lambda i: (i, 0)),
            _resident((1, d)), _resident((d, d_ff)), _resident((d, d_ff)), _resident((d_ff, d)),
        ],
        out_specs=pl.BlockSpec((tm, d), lambda i: (i, 0)),
        out_shape=jax.ShapeDtypeStruct((m, d), F32),
        compiler_params=pltpu.CompilerParams(
            dimension_semantics=("parallel",), vmem_limit_bytes=VMEM_LIMIT),
        name="ffn",
    )(x2, g, wg, wu, wd)


def _proj_body(x_ref, pos_ref, g_ref, w_ref, gains_ref, bd_ref, freq_ref, fbias_ref, tri_ref,
               pq_ref, pk_ref, ones_q_ref, ones_k_ref,
               zb_ref, vtf_ref, vts_ref, vtc_ref, misct_ref, carry_scr):
    t = pl.program_id(1)
    tm = x_ref.shape[1]
    h = _rms_rows(x_ref[0], g_ref[...]).astype(BF16)

    lane = lax.broadcasted_iota(jnp.int32, (1, LANES), 1)
    d_in_head = lane % HEAD_DIM
    pos_cols = jnp.broadcast_to(pos_ref[0, 0].astype(F32), (LANES, tm)).T
    first_half = d_in_head < ROT_DIM // 2
    shared = {}

    def rope(x):
        if "cos" not in shared:
            ang = pos_cols * freq_ref[...]
            shared["cos"] = jnp.cos(ang)
            shared["sin"] = jnp.where(first_half, -jnp.sin(ang), jnp.sin(ang))
        partner = jnp.where(first_half, pltpu.roll(x, LANES - ROT_DIM // 2, 1), pltpu.roll(x, ROT_DIM // 2, 1))
        return x * shared["cos"] + partner * shared["sin"]

    def head_norm(z, c0, width):
        hi, lo = _split2(z * z)
        bd = bd_ref[:width, :width]
        ss = _dot(hi, bd) + _dot(lo, bd)
        return z * lax.rsqrt(ss * (1.0 / HEAD_DIM) + EPS) * gains_ref[:, c0:c0 + width]

    def project(c0, width):
        return _dot(h, w_ref[:, c0:c0 + width])

    def finish_heads(z, c0, width, norm, rot):
        if norm:
            z = head_norm(z, c0, width)
        else:
            z = z * gains_ref[:, c0:c0 + width]
        if rot:
            z = jnp.concatenate([rope(z[:, c:c + LANES]) for c in range(0, width, LANES)], axis=1)
        return z

    def finish_index_key(zm):
        ki = rope(zm)
        zb_ref[0, :, Z_IK:Z_IK + LANES] = jnp.where(lane < IDX_DIM, ki, pltpu.roll(ki, IDX_DIM, 1)).astype(BF16)

    def finish_misc(zm):
        shared["zm"] = zm

        @pl.when(t == 0)
        def _():
            carry_scr[...] = jnp.zeros_like(carry_scr)

        logf = -_softplus(-(zm + fbias_ref[...]))
        hi, mid, lo = _split3(logf)
        tri = tri_ref[...]
        cum = _dot(tri, hi) + _dot(tri, mid) + _dot(tri, lo) + carry_scr[0:1, :]
        carry_scr[...] = jnp.broadcast_to(cum[tm - 1:tm, :], carry_scr.shape)
        is_forget = (lane >= L_FORGET) & (lane < L_FORGET + N_HEADS_FOX)
        is_widx = (lane >= L_WIDX) & (lane < L_WIDX + N_IDX_HEADS)
        misc = jnp.where(is_forget, cum, jnp.where(is_widx, zm * (N_IDX_HEADS ** -0.5), 0.0))
        misc_t = misc.T
        for c in range(tm // TQ):
            misct_ref[0, c] = misc_t[L_FORGET:L_FORGET + MISC_ROWS, c * TQ:(c + 1) * TQ]
        shared["cum"] = _split3(cum)

    def finish_fox(z, c0, z0, p_ref, ones_ref):
        zn = finish_heads(z, c0, W_FOX, True, False).astype(BF16)
        src = jnp.concatenate((zn,) + shared["cum"], axis=1)
        zb_ref[0, :, z0:z0 + W_FOX_AUG] = (_dot(src, p_ref[...]) + ones_ref[...]).astype(BF16)

    def finish_plain(z, c0, z0, width, norm, rot):
        zb_ref[0, :, z0:z0 + width] = finish_heads(z, c0, width, norm, rot).astype(BF16)

    def finish_values(z, vt_ref, width):
        zt = z.T
        for p in range(width // LANES):
            for c in range(tm // TK):
                vt_ref[0, p, c] = zt[p * LANES:(p + 1) * LANES, c * TK:(c + 1) * TK].astype(BF16)

    P = functools.partial
    stages = [
        (C_MISC, LANES, finish_misc),
        (C_FV, W_FOX, P(finish_values, vt_ref=vtf_ref, width=W_FOX)),
        (C_FQ, W_FOX, P(finish_fox, c0=C_FQ, z0=Z_FQ, p_ref=pq_ref, ones_ref=ones_q_ref)),
        (C_FK, W_FOX, P(finish_fox, c0=C_FK, z0=Z_FK, p_ref=pk_ref, ones_ref=ones_k_ref)),
        (C_SV, W_SB, P(finish_values, vt_ref=vts_ref, width=W_SB)),
        (C_SQ, W_SB, P(finish_plain, c0=C_SQ, z0=Z_SQ, width=W_SB, norm=True, rot=False)),
        (C_SK, W_SB, P(finish_plain, c0=C_SK, z0=Z_SK, width=W_SB, norm=True, rot=False)),
        (C_CV, W_DSA, P(finish_values, vt_ref=vtc_ref, width=W_DSA)),
        (C_CQ, W_DSA, P(finish_plain, c0=C_CQ, z0=Z_CQ, width=W_DSA, norm=True, rot=True)),
        (C_CK, W_DSA, P(finish_plain, c0=C_CK, z0=Z_CK, width=W_DSA, norm=True, rot=True)),
        (C_IQ, W_IDX, P(finish_plain, c0=C_IQ, z0=Z_IQ, width=W_IDX, norm=False, rot=True)),
    ]
    pending = None
    for c0, width, finish in stages:
        z = project(c0, width)
        if pending is not None:
            pending[0](pending[1])
        pending = (finish, z)
    pending[0](pending[1])
    finish_index_key(shared["zm"])


def _proj(x, pos3, g, w, gains, consts, fbias):
    b, s, d = x.shape
    tm = min(TM_PROJ, s)
    const = lambda a: pl.BlockSpec(a.shape, lambda bi, ti: (0,) * a.ndim)
    vt_spec = lambda n: pl.BlockSpec((1, n, tm // TK, LANES, TK), lambda bi, ti: (bi, 0, ti, 0, 0))
    vt_shape = lambda n: jax.ShapeDtypeStruct((b, n, s // TK, LANES, TK), BF16)
    cs = (consts["bd"], consts["freq"], fbias, consts["tri_tok"], consts["pq"], consts["pk"],
          consts["ones_q"], consts["ones_k"])
    return pl.pallas_call(
        _proj_body,
        grid=(b, s // tm),
        in_specs=[
            pl.BlockSpec((1, tm, d), lambda bi, ti: (bi, ti, 0)),
            pl.BlockSpec((1, 1, 1, tm), lambda bi, ti: (bi, ti, 0, 0)),
            const(g), const(w), const(gains),
        ] + [const(a) for a in cs],
        out_specs=[
            pl.BlockSpec((1, tm, N_ZB), lambda bi, ti: (bi, ti, 0)),
            vt_spec(W_FOX // LANES), vt_spec(W_SB // LANES), vt_spec(W_DSA // LANES),
            pl.BlockSpec((1, tm // TQ, MISC_ROWS, TQ), lambda bi, ti: (bi, ti, 0, 0)),
        ],
        out_shape=[
            jax.ShapeDtypeStruct((b, s, N_ZB), BF16),
            vt_shape(W_FOX // LANES), vt_shape(W_SB // LANES), vt_shape(W_DSA // LANES),
            jax.ShapeDtypeStruct((b, s // TQ, MISC_ROWS, TQ), F32),
        ],
        scratch_shapes=[pltpu.VMEM((SUBLANES, LANES), F32)],
        compiler_params=pltpu.CompilerParams(
            dimension_semantics=("parallel", "arbitrary"), vmem_limit_bytes=VMEM_LIMIT),
        name="proj",
    )(x, pos3, g, w, gains, *cs)


def _head_masks():
    lane = lax.broadcasted_iota(jnp.int32, (1, LANES), 1)
    return lane < HEAD_DIM, lane >= HEAD_DIM


def _split_heads(q128):
    lo_half, hi_half = _head_masks()
    zero = jnp.zeros_like(q128)
    return jnp.where(lo_half, q128, zero), jnp.where(hi_half, q128, zero)


def _visible(tk, tq, strict):
    key = lax.broadcasted_iota(jnp.int32, (tk, tq), 0)
    qry = lax.broadcasted_iota(jnp.int32, (tk, tq), 1)
    return key < qry if strict else key <= qry


def _softmax_update(s, m, l):
    m_new = jnp.maximum(m, jnp.max(s, axis=0, keepdims=True))
    alpha = jnp.exp(m - m_new)
    p = jnp.exp(s - m_new)
    return p, alpha, m_new, alpha * l + jnp.sum(p, axis=0, keepdims=True)


def _head_rows(hd):
    return slice(hd * HEAD_DIM, (hd + 1) * HEAD_DIM)


def _walk_blocks(n_blocks, step, state):
    last = n_blocks - 1
    state = lax.fori_loop(0, last // 2, lambda n, st: step(2 * n, 2, False, st), state)
    return lax.cond(last % 2 == 1,
                    lambda st: step(last - 1, 2, True, st),
                    lambda st: step(last, 1, True, st),
                    state)


def _key_rows(ref, j0, count):
    return ref[0, pl.ds(pl.multiple_of(j0 * TK, TK), count * TK), :]


def _fox_body(q_ref, k_ref, vt_ref, o_ref, acc_scr):
    i = pl.program_id(1)
    tq = q_ref.shape[1]
    nh = N_HEADS_FOX
    q_heads = [q_ref[0, :, hd * LANES:(hd + 1) * LANES] for hd in range(nh)]
    acc_scr[...] = jnp.zeros_like(acc_scr)

    def step(j0, count, ends, state):
        kb = _key_rows(k_ref, j0, count)
        scores = [_dot_nt(kb[:, hd * LANES:(hd + 1) * LANES], q_heads[hd]) for hd in range(nh)]
        weights = []
        for b in range(count):
            s = [sh[b * TK:(b + 1) * TK, :] for sh in scores]
            if ends and b == count - 1:
                vis = _visible(TK, tq, False)
                s = [jnp.where(vis, sh, -jnp.inf) for sh in s]
            upd = [_softmax_update(s[hd], *state[hd]) for hd in range(nh)]
            state = tuple((u[2], u[3]) for u in upd)
            weights.append([(u[0], u[1]) for u in upd])
        accs = [acc_scr[hd // 2, _head_rows(hd % 2), :] for hd in range(nh)]
        for b in range(count):
            for hd in range(nh):
                p, alpha = weights[b][hd]
                accs[hd] = alpha * accs[hd] + _dot(vt_ref[0, hd // 2, j0 + b, _head_rows(hd % 2), :], p.astype(BF16))
        for hd in range(nh):
            acc_scr[hd // 2, _head_rows(hd % 2), :] = accs[hd]
        return state

    init = tuple((jnp.full((1, tq), -jnp.inf, F32), jnp.zeros((1, tq), F32)) for _ in range(nh))
    state = _walk_blocks(i + 1, step, init)
    for pr in range(nh // 2):
        inv = jnp.concatenate([jnp.broadcast_to(1.0 / state[2 * pr + hd][1], (HEAD_DIM, tq)) for hd in range(2)],
                              axis=0)
        o_ref[0, :, pr * LANES:(pr + 1) * LANES] = (acc_scr[pr] * inv).T.astype(BF16)


def _fox(zb, vt):
    b, s, _ = zb.shape
    nq, nk = s // TQ, s // TK
    n_pairs = W_FOX // LANES
    return pl.pallas_call(
        _fox_body,
        grid=(b, nq),
        in_specs=[
            pl.BlockSpec((1, TQ, W_FOX_AUG), lambda bi, i: (bi, i, Z_FQ // W_FOX_AUG)),
            pl.BlockSpec((1, s, W_FOX_AUG), lambda bi, i: (bi, 0, Z_FK // W_FOX_AUG)),
            pl.BlockSpec((1, n_pairs, nk, LANES, TK), lambda bi, i: (bi, 0, 0, 0, 0)),
        ],
        out_specs=pl.BlockSpec((1, TQ, W_FOX), lambda bi, i: (bi, i, 0)),
        out_shape=jax.ShapeDtypeStruct((b, s, W_FOX), BF16),
        scratch_shapes=[pltpu.VMEM((n_pairs, LANES, TQ), F32)],
        compiler_params=pltpu.CompilerParams(
            dimension_semantics=("parallel", "arbitrary"), vmem_limit_bytes=VMEM_LIMIT),
        name="fox",
    )(zb, zb, vt)


def _sb_body(q_ref, k_ref, vt_ref, tri_ref, o_ref, acc_scr):
    i = pl.program_id(1)
    tq = q_ref.shape[1]
    nh = N_HEADS_SB
    q_heads = []
    for pr in range(nh // 2):
        q_heads.extend(_split_heads(q_ref[0, :, pr * LANES:(pr + 1) * LANES]))
    acc_scr[...] = jnp.zeros_like(acc_scr)

    def step(j, carry, masked):
        kb = k_ref[0, pl.ds(pl.multiple_of(j * TK, TK), TK), :]
        tri = tri_ref[...]
        zs = [_dot_nt(kb[:, (hd // 2) * LANES:(hd // 2 + 1) * LANES], q_heads[hd]) for hd in range(nh)]
        loms = [-_softplus(z) for z in zs]
        if masked:
            strict = _visible(TK, tq, True)
            loms = [jnp.where(strict, lom, 0.0) for lom in loms]
        splits = [_split2(lom) for lom in loms]
        suffixes = [_dot(tri, hi) + _dot(tri, lo) for hi, lo in splits]
        new_carry = []
        for hd in range(nh):
            a = jnp.exp(zs[hd] + suffixes[hd] + carry[hd])
            if masked:
                a = jnp.where(strict, a, 0.0)
            pr, rows = hd // 2, _head_rows(hd % 2)
            acc_scr[pr, rows, :] = acc_scr[pr, rows, :] + _dot(vt_ref[0, pr, j, rows, :], a.astype(BF16))
            new_carry.append(carry[hd] + suffixes[hd][0:1, :])
        return tuple(new_carry)

    carry = step(i, tuple(jnp.zeros((1, tq), F32) for _ in range(nh)), True)

    def live(state):
        n, carry = state
        worst = functools.reduce(jnp.maximum, carry)
        return jnp.logical_and(n < i, jnp.max(worst) > EXP_UNDERFLOW)

    def body(state):
        n, carry = state
        return n + 1, step(i - 1 - n, carry, False)

    lax.while_loop(live, body, (jnp.int32(0), carry))
    for pr in range(nh // 2):
        o_ref[0, :, pr * LANES:(pr + 1) * LANES] = acc_scr[pr].T.astype(BF16)


def _sb(zb, vt, tri_ge):
    b, s, _ = zb.shape
    nq, nk = s // TQ, s // TK
    n_pairs = W_SB // LANES
    return pl.pallas_call(
        _sb_body,
        grid=(b, nq),
        in_specs=[
            pl.BlockSpec((1, TQ, W_SB), lambda bi, i: (bi, i, Z_SQ // W_SB)),
            pl.BlockSpec((1, s, W_SB), lambda bi, i: (bi, 0, Z_SK // W_SB)),
            pl.BlockSpec((1, n_pairs, nk, LANES, TK), lambda bi, i: (bi, 0, 0, 0, 0)),
            pl.BlockSpec((TK, TK), lambda bi, i: (0, 0)),
        ],
        out_specs=pl.BlockSpec((1, TQ, W_SB), lambda bi, i: (bi, i, 0)),
        out_shape=jax.ShapeDtypeStruct((b, s, W_SB), BF16),
        scratch_shapes=[pltpu.VMEM((n_pairs, LANES, TQ), F32)],
        compiler_params=pltpu.CompilerParams(
            dimension_semantics=("parallel", "arbitrary"), vmem_limit_bytes=VMEM_LIMIT),
        name="sb",
    )(zb, zb, vt, tri_ge)


def _dsa_body(iq_ref, ik_ref, q_ref, k_ref, vt_ref, misct_ref, tri_ref, o_ref,
              keys_scr, hi_scr, lo_scr, bias_scr, acc_scr, *, topk):
    i = pl.program_id(1)
    tq = q_ref.shape[1]
    n_pairs = N_HEADS_DSA // 2
    i16_min, i16_max = -2 ** 15, 2 ** 15 - 1

    iq = iq_ref[0]
    iq_heads = _split_heads(iq[:, 0:LANES]) + _split_heads(iq[:, LANES:2 * LANES])
    w_rows = [misct_ref[0, 0, L_WIDX - L_FORGET + hh:L_WIDX - L_FORGET + hh + 1, :] for hh in range(N_IDX_HEADS)]

    def score_step(j0, count, ends, carry):
        kb = _key_rows(ik_ref, j0, count)
        dots = [_dot_nt(kb, iq_heads[hh]) for hh in range(N_IDX_HEADS)]
        for b in range(count):
            rows = slice(b * TK, (b + 1) * TK)
            sc = jnp.zeros((TK, tq), F32)
            for hh in range(N_IDX_HEADS):
                sc = sc + w_rows[hh] * jnp.maximum(dots[hh][rows, :], 0.0)
            if ends and b == count - 1:
                sc = jnp.where(_visible(TK, tq, False), sc, -jnp.inf)
            bits = lax.bitcast_convert_type(sc, jnp.int32)
            key = jnp.where(bits < 0, bits ^ jnp.int32(0x7FFFFFFF), bits)
            keys_scr[j0 + b] = key
            hi_scr[j0 + b] = lax.shift_right_arithmetic(key, 16).astype(jnp.int16)
            lo_scr[j0 + b] = ((key & 0xFFFF) - 2 ** 15).astype(jnp.int16)
        return carry

    _walk_blocks(i + 1, score_step, 0)

    def count16(ref, cand):
        cand_b = jnp.broadcast_to(cand.astype(jnp.int16), (PACK16, tq))
        one, zero = jnp.ones((), jnp.int16), jnp.zeros((), jnp.int16)

        def body(j, accs):
            c = jnp.where(ref[j].reshape(TK // PACK16, PACK16, tq) >= cand_b, one, zero)
            out = []
            for a in range(N_COUNT_ACC):
                g = [c[r] for r in range(a, TK // PACK16, N_COUNT_ACC)]
                while len(g) > 1:
                    g = [g[r] + g[r + 1] for r in range(0, len(g), 2)]
                out.append(accs[a] + g[0])
            return tuple(out)

        zeros = jnp.zeros((PACK16, tq), jnp.int16)
        accs = lax.fori_loop(0, i + 1, body, (zeros,) * N_COUNT_ACC)
        acc = functools.reduce(lambda x, y: x + y, accs)
        return jnp.sum(acc.astype(jnp.int32).astype(F32), axis=0, keepdims=True)

    def count16_gt(ref, thr16):
        return jnp.where(thr16 == i16_max, 0.0, count16(ref, jnp.minimum(thr16 + 1, i16_max)))

    def bisect16(ref, want):
        def it(n, thr16):
            cand = thr16 + lax.shift_left(jnp.int32(1), 15 - n)
            return jnp.where(count16(ref, cand) >= want, cand, thr16)
        return lax.fori_loop(0, 16, it, jnp.full((1, tq), i16_min, jnp.int32))

    kf = jnp.float32(topk)
    t_hi = bisect16(hi_scr, kf)
    n_gt_hi = count16_gt(hi_scr, t_hi)
    t_hi_b = jnp.broadcast_to(t_hi.astype(jnp.int16), (PACK16, tq))

    def low_body(j, c):
        shape = (TK // PACK16, PACK16, tq)
        lo = jnp.where(hi_scr[j].reshape(shape) == t_hi_b, lo_scr[j].reshape(shape), jnp.int16(i16_min))
        lo_scr[j] = lo.reshape(TK, tq)
        return c

    lax.fori_loop(0, i + 1, low_body, 0)
    t_lo = bisect16(lo_scr, kf - n_gt_hi)
    thr = t_hi * 2 ** 16 + t_lo + 2 ** 15
    n_gt_lo = count16_gt(lo_scr, t_lo)
    n_eq = count16(lo_scr, t_lo) - n_gt_lo
    need = kf - n_gt_hi - n_gt_lo

    def plain_step(j, masked):
        sel = keys_scr[j] >= thr
        if masked:
            sel = sel & _visible(TK, tq, False)
        bias_scr[j] = jnp.where(sel, 0.0, NEG_BIG)

    def tie_step(j, run, masked):
        kblk = keys_scr[j]
        eq = kblk == thr
        eqf = jnp.where(eq, 1.0, 0.0)
        before = _dot(tri_ref[...], eqf.astype(BF16)) + run
        sel = (kblk > thr) | (eq & (before < need))
        if masked:
            sel = sel & _visible(TK, tq, False)
        bias_scr[j] = jnp.where(sel, 0.0, NEG_BIG)
        return run + jnp.sum(eqf, axis=0, keepdims=True)

    def plain_bias():
        def body(j, c):
            plain_step(j, False)
            return c
        lax.fori_loop(0, i, body, 0)
        plain_step(i, True)

    def tie_bias():
        run = lax.fori_loop(0, i, lambda j, r: tie_step(j, r, False), jnp.zeros((1, tq), F32))
        tie_step(i, run, True)

    lax.cond(jnp.max(n_eq - need) > 0.0, tie_bias, plain_bias)

    q = q_ref[0]
    q_heads = _split_heads(q[:, 0:LANES]) + _split_heads(q[:, LANES:2 * LANES])
    acc_scr[...] = jnp.zeros_like(acc_scr)

    def attn_step(j0, count, ends, state):
        kb = _key_rows(k_ref, j0, count)
        scores = [_dot_nt(kb[:, (hh // 2) * LANES:(hh // 2 + 1) * LANES], q_heads[hh]) for hh in range(N_HEADS_DSA)]
        weights = []
        for b in range(count):
            bias = bias_scr[j0 + b]
            upd = [_softmax_update(scores[hh][b * TK:(b + 1) * TK, :] + bias, *state[hh])
                   for hh in range(N_HEADS_DSA)]
            state = tuple((u[2], u[3]) for u in upd)
            weights.append([(u[0], u[1]) for u in upd])
        accs = [acc_scr[hh // 2, _head_rows(hh % 2), :] for hh in range(N_HEADS_DSA)]
        for b in range(count):
            for hh in range(N_HEADS_DSA):
                p, alpha = weights[b][hh]
                accs[hh] = alpha * accs[hh] + _dot(vt_ref[0, hh // 2, j0 + b, _head_rows(hh % 2), :],
                                                   p.astype(BF16))
        for hh in range(N_HEADS_DSA):
            acc_scr[hh // 2, _head_rows(hh % 2), :] = accs[hh]
        return state

    init = tuple((jnp.full((1, tq), -jnp.inf, F32), jnp.zeros((1, tq), F32)) for _ in range(N_HEADS_DSA))
    state = _walk_blocks(i + 1, attn_step, init)
    outs = []
    for pr in range(n_pairs):
        inv = jnp.concatenate([jnp.broadcast_to(1.0 / state[2 * pr + hd][1], (HEAD_DIM, tq)) for hd in range(2)],
                              axis=0)
        outs.append((acc_scr[pr] * inv).T)
    o_ref[0] = jnp.concatenate(outs, axis=1).astype(BF16)


def _dsa(zb, vt, misct, tri_lt, topk):
    b, s, _ = zb.shape
    nq, nk = s // TQ, s // TK
    n_pairs = W_DSA // LANES
    return pl.pallas_call(
        functools.partial(_dsa_body, topk=topk),
        grid=(b, nq),
        in_specs=[
            pl.BlockSpec((1, TQ, W_IDX), lambda bi, i: (bi, i, Z_IQ // W_IDX)),
            pl.BlockSpec((1, s, LANES), lambda bi, i: (bi, 0, Z_IK // LANES)),
            pl.BlockSpec((1, TQ, W_DSA), lambda bi, i: (bi, i, Z_CQ // W_DSA)),
            pl.BlockSpec((1, s, W_DSA), lambda bi, i: (bi, 0, Z_CK // W_DSA)),
            pl.BlockSpec((1, n_pairs, nk, LANES, TK), lambda bi, i: (bi, 0, 0, 0, 0)),
            pl.BlockSpec((1, 1, MISC_ROWS, TQ), lambda bi, i: (bi, i, 0, 0)),
            pl.BlockSpec((TK, TK), lambda bi, i: (0, 0)),
        ],
        out_specs=pl.BlockSpec((1, TQ, W_DSA), lambda bi, i: (bi, i, 0)),
        out_shape=jax.ShapeDtypeStruct((b, s, W_DSA), BF16),
        scratch_shapes=[
            pltpu.VMEM((nk, TK, TQ), jnp.int32),
            pltpu.VMEM((nk, TK, TQ), jnp.int16),
            pltpu.VMEM((nk, TK, TQ), jnp.int16),
            pltpu.VMEM((nk, TK, TQ), F32),
            pltpu.VMEM((n_pairs, LANES, TQ), F32),
        ],
        compiler_params=pltpu.CompilerParams(
            dimension_semantics=("parallel", "arbitrary"), vmem_limit_bytes=VMEM_LIMIT),
        name="dsa",
    )(zb, zb, zb, zb, vt, misct, tri_lt)


def _merge_body(x_ref, g_ref, of_ref, os_ref, oc_ref, wg_ref, bg_ref, wf_ref, ws_ref, wc_ref, wo_ref, o_ref):
    x = x_ref[...]
    d = x.shape[1]
    h = _rms_rows(x, g_ref[...]).astype(BF16)
    merged = jnp.zeros_like(x)
    for n, (o_br, w_br) in enumerate(((of_ref, wf_ref), (os_ref, ws_ref), (oc_ref, wc_ref))):
        gate = _dot(h, wg_ref[:, n * d:(n + 1) * d]) + bg_ref[n:n + 1, :]
        gate = 1.0 / (1.0 + jnp.exp(-gate))
        merged = merged + gate * _dot(o_br[...], w_br[...])
    o_ref[...] = x + _dot(merged.astype(BF16), wo_ref[...])


def _merge(x2, g, o_f, o_s, o_c, w_gates, b_gates, w_f, w_s, w_c, w_o):
    m, d = x2.shape
    tm = min(TM_MERGE, m)
    row = lambda w: pl.BlockSpec((tm, w), lambda i: (i, 0))
    const = lambda a: pl.BlockSpec(a.shape, lambda i: (0, 0))
    return pl.pallas_call(
        _merge_body,
        grid=(m // tm,),
        in_specs=[row(d), const(g), row(W_FOX), row(W_SB), row(W_DSA), const(w_gates), const(b_gates),
                  const(w_f), const(w_s), const(w_c), const(w_o)],
        out_specs=row(d),
        out_shape=jax.ShapeDtypeStruct((m, d), F32),
        compiler_params=pltpu.CompilerParams(
            dimension_semantics=("parallel",), vmem_limit_bytes=VMEM_LIMIT),
        name="merge",
    )(x2, g, o_f, o_s, o_c, w_gates, b_gates, w_f, w_s, w_c, w_o)


def _tile_heads(g, n):
    return jnp.tile(g.astype(F32), n)


def _constants(tm_proj):
    r = np.arange(W_FOX)
    bd = (r[:, None] // HEAD_DIM == r[None, :] // HEAD_DIM).astype(np.float32)
    rt = np.arange(tm_proj)
    tri_tok = (rt[None, :] <= rt[:, None]).astype(np.float32)
    rk = np.arange(TK)
    tri_ge = (rk[None, :] >= rk[:, None]).astype(np.float32)
    tri_lt = (rk[None, :] < rk[:, None]).astype(np.float32)

    n_src = W_FOX + N_BIAS_FEATS * LANES
    pq = np.zeros((n_src, W_FOX_AUG), np.float32)
    pk = np.zeros((n_src, W_FOX_AUG), np.float32)
    ones_q = np.zeros((1, W_FOX_AUG), np.float32)
    ones_k = np.zeros((1, W_FOX_AUG), np.float32)
    for hd in range(N_HEADS_FOX):
        for dd in range(HEAD_DIM):
            pq[hd * HEAD_DIM + dd, hd * LANES + dd] = 1.0
            pk[hd * HEAD_DIM + dd, hd * LANES + dd] = 1.0
        for part in range(N_BIAS_FEATS):
            src_row = W_FOX + part * LANES + L_FORGET + hd
            pq[src_row, hd * LANES + HEAD_DIM + part] = 1.0
            ones_q[0, hd * LANES + HEAD_DIM + N_BIAS_FEATS + part] = 1.0
            ones_k[0, hd * LANES + HEAD_DIM + part] = 1.0
            pk[src_row, hd * LANES + HEAD_DIM + N_BIAS_FEATS + part] = -1.0
    half = ROT_DIM // 2
    inv_freq = jnp.power(ROPE_THETA, -jnp.arange(half, dtype=F32) * 2.0 / ROT_DIM)
    dlane = np.arange(LANES) % HEAD_DIM
    freq = jnp.where(jnp.asarray(dlane < ROT_DIM), inv_freq[jnp.asarray(dlane % half)], 0.0)
    return dict(bd=jnp.asarray(bd, BF16), tri_tok=jnp.asarray(tri_tok, BF16), tri_ge=jnp.asarray(tri_ge, BF16),
                tri_lt=jnp.asarray(tri_lt, BF16), pq=jnp.asarray(pq, BF16), pk=jnp.asarray(pk, BF16),
                ones_q=jnp.asarray(ones_q, F32), ones_k=jnp.asarray(ones_k, F32),
                freq=freq.reshape(1, LANES).astype(F32))


def kernel(x, positions, ffn1_norm, ffn1_w_gate, ffn1_w_up, ffn1_w_down, mix_norm, w_in, b_forget, b_gates, q_norm_fox, k_norm_fox, q_norm_sb, k_norm_sb, q_norm_dsa, k_norm_dsa, w_branch_fox, w_branch_sb, w_branch_dsa, w_out, ffn2_norm, ffn2_w_gate, ffn2_w_up, ffn2_w_down):
    b, s, d = x.shape
    depth = w_in.shape[0]
    topk = min(TOPK_MAX, s // 4)
    assert s % TQ == 0 and s % TK == 0 and TQ == TK and d % LANES == 0
    consts = _constants(min(TM_PROJ, s))
    tm_proj = min(TM_PROJ, s)
    pos3 = positions.reshape(b, s // tm_proj, 1, tm_proj)
    scale = HEAD_DIM ** -0.5
    idx_scale = IDX_DIM ** -0.5

    splits = (W_FOX, W_FOX, W_FOX, N_HEADS_FOX, W_SB, W_SB, W_SB, W_DSA, W_DSA, W_DSA,
              W_IDX, IDX_DIM, N_IDX_HEADS, d, d, d)
    offs = np.concatenate([[0], np.cumsum(splits)]).tolist()
    (o_qf, o_kf, o_vf, o_ff, o_qs, o_ks, o_vs, o_qc, o_kc, o_vc, o_qi, o_ki, o_wi, o_ga) = offs[:14]

    x2 = x.reshape(b * s, d)
    for l in range(depth):
        wl = w_in[l].astype(BF16)
        pad = jnp.zeros((d, LANES - IDX_DIM - N_HEADS_FOX - N_IDX_HEADS), BF16)
        w_proj = jnp.concatenate([
            wl[:, o_qf:o_ff], wl[:, o_qs:o_qi], wl[:, o_qi:o_ki],
            wl[:, o_ki:o_wi], wl[:, o_ff:o_qs], wl[:, o_wi:o_ga], pad], axis=1)
        w_gates = wl[:, o_ga:]
        ones = lambda n: jnp.ones((n,), F32)
        gains = jnp.concatenate([
            _tile_heads(q_norm_fox[l], N_HEADS_FOX) * scale, _tile_heads(k_norm_fox[l], N_HEADS_FOX), ones(W_FOX),
            _tile_heads(q_norm_sb[l], N_HEADS_SB) * scale, _tile_heads(k_norm_sb[l], N_HEADS_SB), ones(W_SB),
            _tile_heads(q_norm_dsa[l], N_HEADS_DSA) * scale, _tile_heads(k_norm_dsa[l], N_HEADS_DSA), ones(W_DSA),
            ones(W_IDX) * idx_scale, ones(LANES)]).reshape(1, N_PROJ)
        fbias = jnp.zeros((LANES,), F32).at[L_FORGET:L_FORGET + N_HEADS_FOX].set(b_forget[l].astype(F32))
        fbias = fbias.reshape(1, LANES)

        x2 = _ffn(x2, ffn1_norm[l].reshape(1, d), ffn1_w_gate[l].astype(BF16), ffn1_w_up[l].astype(BF16),
                  ffn1_w_down[l].astype(BF16))
        zb, vt_f, vt_s, vt_c, misct = _proj(x2.reshape(b, s, d), pos3, mix_norm[l].reshape(1, d), w_proj, gains,
                                            consts, fbias)
        o_f = _fox(zb, vt_f)
        o_s = _sb(zb, vt_s, consts["tri_ge"])
        o_c = _dsa(zb, vt_c, misct, consts["tri_lt"], topk)
        x2 = _merge(x2, mix_norm[l].reshape(1, d), o_f.reshape(b * s, W_FOX), o_s.reshape(b * s, W_SB),
                    o_c.reshape(b * s, W_DSA), w_gates, b_gates[l].astype(F32),
                    w_branch_fox[l].astype(BF16), w_branch_sb[l].astype(BF16), w_branch_dsa[l].astype(BF16),
                    w_out[l].astype(BF16))
        x2 = _ffn(x2, ffn2_norm[l].reshape(1, d), ffn2_w_gate[l].astype(BF16), ffn2_w_up[l].astype(BF16),
                  ffn2_w_down[l].astype(BF16))
    return x2.reshape(b, s, d)
```

```python
import functools

import jax
import jax.numpy as jnp
import numpy as np
from jax import lax
from jax.experimental import pallas as pl
from jax.experimental.pallas import tpu as pltpu

F32 = jnp.float32
BF16 = jnp.bfloat16

HEAD_DIM = 64
N_HEADS_FOX = 6
N_HEADS_SB = 6
N_HEADS_DSA = 4
N_IDX_HEADS = 4
IDX_DIM = 64
TOPK_MAX = 256
ROPE_THETA = 500000.0
ROT_DIM = HEAD_DIM // 4
EPS = 1e-6

LANES = 128
SUBLANES = 8
PACK16 = 2 * SUBLANES
N_COUNT_ACC = 4
W_FOX = N_HEADS_FOX * HEAD_DIM
W_SB = N_HEADS_SB * HEAD_DIM
W_DSA = N_HEADS_DSA * HEAD_DIM
W_IDX = N_IDX_HEADS * IDX_DIM

C_FQ, C_FK, C_FV = 0, W_FOX, 2 * W_FOX
C_SQ, C_SK, C_SV = 3 * W_FOX, 3 * W_FOX + W_SB, 3 * W_FOX + 2 * W_SB
C_CQ = 3 * W_FOX + 3 * W_SB
C_CK, C_CV = C_CQ + W_DSA, C_CQ + 2 * W_DSA
C_IQ = C_CQ + 3 * W_DSA
C_MISC = C_IQ + W_IDX
N_PROJ = C_MISC + LANES
L_FORGET = IDX_DIM
L_WIDX = IDX_DIM + N_HEADS_FOX
MISC_ROWS = 16

W_FOX_AUG = N_HEADS_FOX * LANES
Z_FQ, Z_FK = 0, W_FOX_AUG
Z_SQ, Z_SK = 2 * W_FOX_AUG, 2 * W_FOX_AUG + W_SB
Z_CQ = 2 * W_FOX_AUG + 2 * W_SB
Z_CK = Z_CQ + W_DSA
Z_IQ = Z_CK + W_DSA
Z_IK = Z_IQ + W_IDX
N_ZB = Z_IK + LANES
N_BIAS_FEATS = 3

TQ = 256
TK = 256
TM_PROJ = 512
TM_FFN = 512
TM_MERGE = 512
NEG_BIG = -1e30
EXP_UNDERFLOW = -110.0
VMEM_LIMIT = 56 * 1024 * 1024

_NT = (((1,), (1,)), ((), ()))


def _dot(a, b):
    return jnp.dot(a, b, preferred_element_type=F32)


def _dot_nt(a, b):
    return lax.dot_general(a, b, _NT, preferred_element_type=F32)


def _split2(x):
    hi = x.astype(BF16)
    lo = (x - hi.astype(F32)).astype(BF16)
    return hi, lo


def _split3(x):
    hi = x.astype(BF16)
    r = x - hi.astype(F32)
    mid = r.astype(BF16)
    lo = (r - mid.astype(F32)).astype(BF16)
    return hi, mid, lo


def _rms_rows(x, g):
    ms = jnp.mean(x * x, axis=-1, keepdims=True)
    return x * lax.rsqrt(ms + EPS) * g


def _softplus(z):
    return jnp.maximum(z, 0.0) + jnp.log(1.0 + jnp.exp(-jnp.abs(z)))


def _ffn_body(x_ref, g_ref, wg_ref, wu_ref, wd_ref, o_ref, *, tf):
    x = x_ref[...]
    h = _rms_rows(x, g_ref[...]).astype(BF16)
    acc = None
    for c in range(0, wg_ref.shape[1], tf):
        a = _dot(h, wg_ref[:, c:c + tf])
        u = _dot(h, wu_ref[:, c:c + tf])
        p = (a * (1.0 / (1.0 + jnp.exp(-a))) * u).astype(BF16)
        part = _dot(p, wd_ref[c:c + tf, :])
        acc = part if acc is None else acc + part
    o_ref[...] = x + 0.5 * acc


def _ffn_tile_f(d_ff):
    best = LANES
    for t in range(LANES, d_ff + 1, LANES):
        if d_ff % t == 0 and t <= 1536:
            best = t
    return best


def _resident(shape):
    return pl.BlockSpec(shape, lambda *_: (0,) * len(shape), pipeline_mode=pl.Buffered(1))


def _ffn(x2, g, wg, wu, wd):
    m, d = x2.shape
    d_ff = wg.shape[1]
    tm = min(TM_FFN, m)
    return pl.pallas_call(
        functools.partial(_ffn_body, tf=_ffn_tile_f(d_ff)),
        grid=(m // tm,),
        in_specs=[
            pl.BlockSpec((tm, d), lambda i: (i, 0)),
            _resident((1, d)), _resident((d, d_ff)), _resident((d, d_ff)), _resident((d_ff, d)),
        ],
        out_specs=pl.BlockSpec((tm, d), lambda i: (i, 0)),
        out_shape=jax.ShapeDtypeStruct((m, d), F32),
        compiler_params=pltpu.CompilerParams(
            dimension_semantics=("parallel",), vmem_limit_bytes=VMEM_LIMIT),
        name="ffn",
    )(x2, g, wg, wu, wd)


def _proj_body(x_ref, pos_ref, g_ref, w_ref, gains_ref, bd_ref, freq_ref, fbias_ref, tri_ref,
               pq_ref, pk_ref, ones_q_ref, ones_k_ref,
               zb_ref, vtf_ref, vts_ref, vtc_ref, misct_ref, carry_scr):
    t = pl.program_id(1)
    tm = x_ref.shape[1]
    h = _rms_rows(x_ref[0], g_ref[...]).astype(BF16)

    lane = lax.broadcasted_iota(jnp.int32, (1, LANES), 1)
    d_in_head = lane % HEAD_DIM
    pos_cols = jnp.broadcast_to(pos_ref[0, 0].astype(F32), (LANES, tm)).T
    first_half = d_in_head < ROT_DIM // 2
    shared = {}

    def rope(x):
        if "cos" not in shared:
            ang = pos_cols * freq_ref[...]
            shared["cos"] = jnp.cos(ang)
            shared["sin"] = jnp.where(first_half, -jnp.sin(ang), jnp.sin(ang))
        partner = jnp.where(first_half, pltpu.roll(x, LANES - ROT_DIM // 2, 1), pltpu.roll(x, ROT_DIM // 2, 1))
        return x * shared["cos"] + partner * shared["sin"]

    def head_norm(z, c0, width):
        hi, lo = _split2(z * z)
        bd = bd_ref[:width, :width]
        ss = _dot(hi, bd) + _dot(lo, bd)
        return z * lax.rsqrt(ss * (1.0 / HEAD_DIM) + EPS) * gains_ref[:, c0:c0 + width]

    def project(c0, width):
        return _dot(h, w_ref[:, c0:c0 + width])

    def finish_heads(z, c0, width, norm, rot):
        if norm:
            z = head_norm(z, c0, width)
        else:
            z = z * gains_ref[:, c0:c0 + width]
        if rot:
            z = jnp.concatenate([rope(z[:, c:c + LANES]) for c in range(0, width, LANES)], axis=1)
        return z

    def finish_index_key(zm):
        ki = rope(zm)
        zb_ref[0, :, Z_IK:Z_IK + LANES] = jnp.where(lane < IDX_DIM, ki, pltpu.roll(ki, IDX_DIM, 1)).astype(BF16)

    def finish_misc(zm):
        shared["zm"] = zm

        @pl.when(t == 0)
        def _():
            carry_scr[...] = jnp.zeros_like(carry_scr)

        logf = -_softplus(-(zm + fbias_ref[...]))
        hi, mid, lo = _split3(logf)
        tri = tri_ref[...]
        cum = _dot(tri, hi) + _dot(tri, mid) + _dot(tri, lo) + carry_scr[0:1, :]
        carry_scr[...] = jnp.broadcast_to(cum[tm - 1:tm, :], carry_scr.shape)
        is_forget = (lane >= L_FORGET) & (lane < L_FORGET + N_HEADS_FOX)
        is_widx = (lane >= L_WIDX) & (lane < L_WIDX + N_IDX_HEADS)
        misc = jnp.where(is_forget, cum, jnp.where(is_widx, zm * (N_IDX_HEADS ** -0.5), 0.0))
        misc_t = misc.T
        for c in range(tm // TQ):
            misct_ref[0, c] = misc_t[L_FORGET:L_FORGET + MISC_ROWS, c * TQ:(c + 1) * TQ]
        shared["cum"] = _split3(cum)

    def finish_fox(z, c0, z0, p_ref, ones_ref):
        zn = finish_heads(z, c0, W_FOX, True, False).astype(BF16)
        src = jnp.concatenate((zn,) + shared["cum"], axis=1)
        zb_ref[0, :, z0:z0 + W_FOX_AUG] = (_dot(src, p_ref[...]) + ones_ref[...]).astype(BF16)

    def finish_plain(z, c0, z0, width, norm, rot):
        zb_ref[0, :, z0:z0 + width] = finish_heads(z, c0, width, norm, rot).astype(BF16)

    def finish_values(z, vt_ref, width):
        zt = z.T
        for p in range(width // LANES):
            for c in range(tm // TK):
                vt_ref[0, p, c] = zt[p * LANES:(p + 1) * LANES, c * TK:(c + 1) * TK].astype(BF16)

    P = functools.partial
    stages = [
        (C_MISC, LANES, finish_misc),
        (C_FV, W_FOX, P(finish_values, vt_ref=vtf_ref, width=W_FOX)),
        (C_FQ, W_FOX, P(finish_fox, c0=C_FQ, z0=Z_FQ, p_ref=pq_ref, ones_ref=ones_q_ref)),
        (C_FK, W_FOX, P(finish_fox, c0=C_FK, z0=Z_FK, p_ref=pk_ref, ones_ref=ones_k_ref)),
        (C_SV, W_SB, P(finish_values, vt_ref=vts_ref, width=W_SB)),
        (C_SQ, W_SB, P(finish_plain, c0=C_SQ, z0=Z_SQ, width=W_SB, norm=True, rot=False)),
        (C_SK, W_SB, P(finish_plain, c0=C_SK, z0=Z_SK, width=W_SB, norm=True, rot=False)),
        (C_CV, W_DSA, P(finish_values, vt_ref=vtc_ref, width=W_DSA)),
        (C_CQ, W_DSA, P(finish_plain, c0=C_CQ, z0=Z_CQ, width=W_DSA, norm=True, rot=True)),
        (C_CK, W_DSA, P(finish_plain, c0=C_CK, z0=Z_CK, width=W_DSA, norm=True, rot=True)),
        (C_IQ, W_IDX, P(finish_plain, c0=C_IQ, z0=Z_IQ, width=W_IDX, norm=False, rot=True)),
    ]
    pending = None
    for c0, width, finish in stages:
        z = project(c0, width)
        if pending is not None:
            pending[0](pending[1])
        pending = (finish, z)
    pending[0](pending[1])
    finish_index_key(shared["zm"])


def _proj(x, pos3, g, w, gains, consts, fbias):
    b, s, d = x.shape
    tm = min(TM_PROJ, s)
    const = lambda a: pl.BlockSpec(a.shape, lambda bi, ti: (0,) * a.ndim)
    vt_spec = lambda n: pl.BlockSpec((1, n, tm // TK, LANES, TK), lambda bi, ti: (bi, 0, ti, 0, 0))
    vt_shape = lambda n: jax.ShapeDtypeStruct((b, n, s // TK, LANES, TK), BF16)
    cs = (consts["bd"], consts["freq"], fbias, consts["tri_tok"], consts["pq"], consts["pk"],
          consts["ones_q"], consts["ones_k"])
    return pl.pallas_call(
        _proj_body,
        grid=(b, s // tm),
        in_specs=[
            pl.BlockSpec((1, tm, d), lambda bi, ti: (bi, ti, 0)),
            pl.BlockSpec((1, 1, 1, tm), lambda bi, ti: (bi, ti, 0, 0)),
            const(g), const(w), const(gains),
        ] + [const(a) for a in cs],
        out_specs=[
            pl.BlockSpec((1, tm, N_ZB), lambda bi, ti: (bi, ti, 0)),
            vt_spec(W_FOX // LANES), vt_spec(W_SB // LANES), vt_spec(W_DSA // LANES),
            pl.BlockSpec((1, tm // TQ, MISC_ROWS, TQ), lambda bi, ti: (bi, ti, 0, 0)),
        ],
        out_shape=[
            jax.ShapeDtypeStruct((b, s, N_ZB), BF16),
            vt_shape(W_FOX // LANES), vt_shape(W_SB // LANES), vt_shape(W_DSA // LANES),
            jax.ShapeDtypeStruct((b, s // TQ, MISC_ROWS, TQ), F32),
        ],
        scratch_shapes=[pltpu.VMEM((SUBLANES, LANES), F32)],
        compiler_params=pltpu.CompilerParams(
            dimension_semantics=("parallel", "arbitrary"), vmem_limit_bytes=VMEM_LIMIT),
        name="proj",
    )(x, pos3, g, w, gains, *cs)


def _head_masks():
    lane = lax.broadcasted_iota(jnp.int32, (1, LANES), 1)
    return lane < HEAD_DIM, lane >= HEAD_DIM


def _split_heads(q128):
    lo_half, hi_half = _head_masks()
    zero = jnp.zeros_like(q128)
    return jnp.where(lo_half, q128, zero), jnp.where(hi_half, q128, zero)


def _visible(tk, tq, strict):
    key = lax.broadcasted_iota(jnp.int32, (tk, tq), 0)
    qry = lax.broadcasted_iota(jnp.int32, (tk, tq), 1)
    return key < qry if strict else key <= qry


def _softmax_update(s, m, l):
    m_new = jnp.maximum(m, jnp.max(s, axis=0, keepdims=True))
    alpha = jnp.exp(m - m_new)
    p = jnp.exp(s - m_new)
    return p, alpha, m_new, alpha * l + jnp.sum(p, axis=0, keepdims=True)


def _head_rows(hd):
    return slice(hd * HEAD_DIM, (hd + 1) * HEAD_DIM)


def _walk_blocks(n_blocks, step, state):
    last = n_blocks - 1
    state = lax.fori_loop(0, last // 2, lambda n, st: step(2 * n, 2, False, st), state)
    return lax.cond(last % 2 == 1,
                    lambda st: step(last - 1, 2, True, st),
                    lambda st: step(last, 1, True, st),
                    state)


def _key_rows(ref, j0, count):
    return ref[0, pl.ds(pl.multiple_of(j0 * TK, TK), count * TK), :]


def _fox_body(q_ref, k_ref, vt_ref, o_ref, acc_scr):
    i = pl.program_id(1)
    tq = q_ref.shape[1]
    nh = N_HEADS_FOX
    q_heads = [q_ref[0, :, hd * LANES:(hd + 1) * LANES] for hd in range(nh)]
    acc_scr[...] = jnp.zeros_like(acc_scr)

    def step(j0, count, ends, state):
        kb = _key_rows(k_ref, j0, count)
        scores = [_dot_nt(kb[:, hd * LANES:(hd + 1) * LANES], q_heads[hd]) for hd in range(nh)]
        weights = []
        for b in range(count):
            s = [sh[b * TK:(b + 1) * TK, :] for sh in scores]
            if ends and b == count - 1:
                vis = _visible(TK, tq, False)
                s = [jnp.where(vis, sh, -jnp.inf) for sh in s]
            upd = [_softmax_update(s[hd], *state[hd]) for hd in range(nh)]
            state = tuple((u[2], u[3]) for u in upd)
            weights.append([(u[0], u[1]) for u in upd])
        accs = [acc_scr[hd // 2, _head_rows(hd % 2), :] for hd in range(nh)]
        for b in range(count):
            for hd in range(nh):
                p, alpha = weights[b][hd]
                accs[hd] = alpha * accs[hd] + _dot(vt_ref[0, hd // 2, j0 + b, _head_rows(hd % 2), :], p.astype(BF16))
        for hd in range(nh):
            acc_scr[hd // 2, _head_rows(hd % 2), :] = accs[hd]
        return state

    init = tuple((jnp.full((1, tq), -jnp.inf, F32), jnp.zeros((1, tq), F32)) for _ in range(nh))
    state = _walk_blocks(i + 1, step, init)
    for pr in range(nh // 2):
        inv = jnp.concatenate([jnp.broadcast_to(1.0 / state[2 * pr + hd][1], (HEAD_DIM, tq)) for hd in range(2)],
                              axis=0)
        o_ref[0, :, pr * LANES:(pr + 1) * LANES] = (acc_scr[pr] * inv).T.astype(BF16)


def _fox(zb, vt):
    b, s, _ = zb.shape
    nq, nk = s // TQ, s // TK
    n_pairs = W_FOX // LANES
    return pl.pallas_call(
        _fox_body,
        grid=(b, nq),
        in_specs=[
            pl.BlockSpec((1, TQ, W_FOX_AUG), lambda bi, i: (bi, i, Z_FQ // W_FOX_AUG)),
            pl.BlockSpec((1, s, W_FOX_AUG), lambda bi, i: (bi, 0, Z_FK // W_FOX_AUG)),
            pl.BlockSpec((1, n_pairs, nk, LANES, TK), lambda bi, i: (bi, 0, 0, 0, 0)),
        ],
        out_specs=pl.BlockSpec((1, TQ, W_FOX), lambda bi, i: (bi, i, 0)),
        out_shape=jax.ShapeDtypeStruct((b, s, W_FOX), BF16),
        scratch_shapes=[pltpu.VMEM((n_pairs, LANES, TQ), F32)],
        compiler_params=pltpu.CompilerParams(
            dimension_semantics=("parallel", "arbitrary"), vmem_limit_bytes=VMEM_LIMIT),
        name="fox",
    )(zb, zb, vt)


def _sb_body(q_ref, k_ref, vt_ref, tri_ref, o_ref, acc_scr):
    i = pl.program_id(1)
    tq = q_ref.shape[1]
    nh = N_HEADS_SB
    q_heads = []
    for pr in range(nh // 2):
        q_heads.extend(_split_heads(q_ref[0, :, pr * LANES:(pr + 1) * LANES]))
    acc_scr[...] = jnp.zeros_like(acc_scr)

    def step(j0, count, diagonal, carry):
        kb = _key_rows(k_ref, j0, count)
        tri = tri_ref[...]
        zs = [_dot_nt(kb[:, (hd // 2) * LANES:(hd // 2 + 1) * LANES], q_heads[hd]) for hd in range(nh)]
        strict = _visible(TK, tq, True)
        order = list(range(count - 1, -1, -1))
        suffixes = {}
        for b in order:
            rows = slice(b * TK, (b + 1) * TK)
            loms = [-_softplus(z[rows, :]) for z in zs]
            if diagonal and b == count - 1:
                loms = [jnp.where(strict, lom, 0.0) for lom in loms]
            splits = [_split2(lom) for lom in loms]
            suffixes[b] = [_dot(tri, hi) + _dot(tri, lo) for hi, lo in splits]
        accs = [acc_scr[hd // 2, _head_rows(hd % 2), :] for hd in range(nh)]
        carry = list(carry)
        for b in order:
            rows = slice(b * TK, (b + 1) * TK)
            for hd in range(nh):
                a = jnp.exp(zs[hd][rows, :] + suffixes[b][hd] + carry[hd])
                if diagonal and b == count - 1:
                    a = jnp.where(strict, a, 0.0)
                accs[hd] = accs[hd] + _dot(vt_ref[0, hd // 2, j0 + b, _head_rows(hd % 2), :], a.astype(BF16))
                carry[hd] = carry[hd] + suffixes[b][hd][0:1, :]
        for hd in range(nh):
            acc_scr[hd // 2, _head_rows(hd % 2), :] = accs[hd]
        return tuple(carry)

    zero = tuple(jnp.zeros((1, tq), F32) for _ in range(nh))
    carry = lax.cond(i > 0, lambda c: step(i - 1, 2, True, c), lambda c: step(i, 1, True, c), zero)

    def live(state):
        j, carry = state
        worst = functools.reduce(jnp.maximum, carry)
        return jnp.logical_and(j >= 0, jnp.max(worst) > EXP_UNDERFLOW)

    def body(state):
        j, carry = state
        return j - 1, step(j, 1, False, carry)

    lax.while_loop(live, body, (i - 2, carry))
    for pr in range(nh // 2):
        o_ref[0, :, pr * LANES:(pr + 1) * LANES] = acc_scr[pr].T.astype(BF16)


def _sb(zb, vt, tri_ge):
    b, s, _ = zb.shape
    nq, nk = s // TQ, s // TK
    n_pairs = W_SB // LANES
    return pl.pallas_call(
        _sb_body,
        grid=(b, nq),
        in_specs=[
            pl.BlockSpec((1, TQ, W_SB), lambda bi, i: (bi, i, Z_SQ // W_SB)),
            pl.BlockSpec((1, s, W_SB), lambda bi, i: (bi, 0, Z_SK // W_SB)),
            pl.BlockSpec((1, n_pairs, nk, LANES, TK), lambda bi, i: (bi, 0, 0, 0, 0)),
            pl.BlockSpec((TK, TK), lambda bi, i: (0, 0)),
        ],
        out_specs=pl.BlockSpec((1, TQ, W_SB), lambda bi, i: (bi, i, 0)),
        out_shape=jax.ShapeDtypeStruct((b, s, W_SB), BF16),
        scratch_shapes=[pltpu.VMEM((n_pairs, LANES, TQ), F32)],
        compiler_params=pltpu.CompilerParams(
            dimension_semantics=("parallel", "arbitrary"), vmem_limit_bytes=VMEM_LIMIT),
        name="sb",
    )(zb, zb, vt, tri_ge)


def _dsa_body(iq_ref, ik_ref, q_ref, k_ref, vt_ref, misct_ref, tri_ref, o_ref,
              keys_scr, hi_scr, lo_scr, bias_scr, acc_scr, *, topk):
    i = pl.program_id(1)
    tq = q_ref.shape[1]
    n_pairs = N_HEADS_DSA // 2
    i16_min, i16_max = -2 ** 15, 2 ** 15 - 1

    iq = iq_ref[0]
    iq_heads = _split_heads(iq[:, 0:LANES]) + _split_heads(iq[:, LANES:2 * LANES])
    w_rows = [misct_ref[0, 0, L_WIDX - L_FORGET + hh:L_WIDX - L_FORGET + hh + 1, :] for hh in range(N_IDX_HEADS)]

    def score_step(j0, count, ends, carry):
        kb = _key_rows(ik_ref, j0, count)
        dots = [_dot_nt(kb, iq_heads[hh]) for hh in range(N_IDX_HEADS)]
        for b in range(count):
            rows = slice(b * TK, (b + 1) * TK)
            sc = jnp.zeros((TK, tq), F32)
            for hh in range(N_IDX_HEADS):
                sc = sc + w_rows[hh] * jnp.maximum(dots[hh][rows, :], 0.0)
            if ends and b == count - 1:
                sc = jnp.where(_visible(TK, tq, False), sc, -jnp.inf)
            bits = lax.bitcast_convert_type(sc, jnp.int32)
            key = jnp.where(bits < 0, bits ^ jnp.int32(0x7FFFFFFF), bits)
            keys_scr[j0 + b] = key
            hi_scr[j0 + b] = lax.shift_right_arithmetic(key, 16).astype(jnp.int16)
            lo_scr[j0 + b] = ((key & 0xFFFF) - 2 ** 15).astype(jnp.int16)
        return carry

    _walk_blocks(i + 1, score_step, 0)

    n_block_pairs = (i + 2) // 2

    @pl.when(i % 2 == 0)
    def _():
        hi_scr[i + 1] = jnp.full((TK, tq), i16_min, jnp.int16)
        lo_scr[i + 1] = jnp.full((TK, tq), i16_min, jnp.int16)

    def count16(ref, cand):
        cand_b = jnp.broadcast_to(cand.astype(jnp.int16), (PACK16, tq))
        one, zero = jnp.ones((), jnp.int16), jnp.zeros((), jnp.int16)

        def body(n, accs):
            out = list(accs)
            for b in range(2):
                c = jnp.where(ref[2 * n + b].reshape(TK // PACK16, PACK16, tq) >= cand_b, one, zero)
                for a in range(N_COUNT_ACC):
                    g = [c[r] for r in range(a, TK // PACK16, N_COUNT_ACC)]
                    while len(g) > 1:
                        g = [g[r] + g[r + 1] for r in range(0, len(g), 2)]
                    out[a] = out[a] + g[0]
            return tuple(out)

        zeros = jnp.zeros((PACK16, tq), jnp.int16)
        accs = lax.fori_loop(0, n_block_pairs, body, (zeros,) * N_COUNT_ACC)
        acc = functools.reduce(lambda x, y: x + y, accs)
        return jnp.sum(acc.astype(jnp.int32).astype(F32), axis=0, keepdims=True)

    def count16_gt(ref, thr16):
        return jnp.where(thr16 == i16_max, 0.0, count16(ref, jnp.minimum(thr16 + 1, i16_max)))

    def bisect16(ref, want):
        def it(n, thr16):
            cand = thr16 + lax.shift_left(jnp.int32(1), 15 - n)
            return jnp.where(count16(ref, cand) >= want, cand, thr16)
        return lax.fori_loop(0, 16, it, jnp.full((1, tq), i16_min, jnp.int32))

    kf = jnp.float32(topk)
    t_hi = bisect16(hi_scr, kf)
    n_gt_hi = count16_gt(hi_scr, t_hi)
    t_hi_b = jnp.broadcast_to(t_hi.astype(jnp.int16), (PACK16, tq))

    def low_body(n, c):
        shape = (TK // PACK16, PACK16, tq)
        for j in (2 * n, 2 * n + 1):
            lo = jnp.where(hi_scr[j].reshape(shape) == t_hi_b, lo_scr[j].reshape(shape), jnp.int16(i16_min))
            lo_scr[j] = lo.reshape(TK, tq)
        return c

    lax.fori_loop(0, n_block_pairs, low_body, 0)
    t_lo = bisect16(lo_scr, kf - n_gt_hi)
    thr = t_hi * 2 ** 16 + t_lo + 2 ** 15
    n_gt_lo = count16_gt(lo_scr, t_lo)
    n_eq = count16(lo_scr, t_lo) - n_gt_lo
    need = kf - n_gt_hi - n_gt_lo

    def plain_step(j, masked):
        sel = keys_scr[j] >= thr
        if masked:
            sel = sel & _visible(TK, tq, False)
        bias_scr[j] = jnp.where(sel, 0.0, NEG_BIG)

    def tie_step(j, run, masked):
        kblk = keys_scr[j]
        eq = kblk == thr
        eqf = jnp.where(eq, 1.0, 0.0)
        before = _dot(tri_ref[...], eqf.astype(BF16)) + run
        sel = (kblk > thr) | (eq & (before < need))
        if masked:
            sel = sel & _visible(TK, tq, False)
        bias_scr[j] = jnp.where(sel, 0.0, NEG_BIG)
        return run + jnp.sum(eqf, axis=0, keepdims=True)

    def plain_bias():
        def body(j, c):
            plain_step(j, False)
            return c
        lax.fori_loop(0, i, body, 0)
        plain_step(i, True)

    def tie_bias():
        run = lax.fori_loop(0, i, lambda j, r: tie_step(j, r, False), jnp.zeros((1, tq), F32))
        tie_step(i, run, True)

    lax.cond(jnp.max(n_eq - need) > 0.0, tie_bias, plain_bias)

    q = q_ref[0]
    q_heads = _split_heads(q[:, 0:LANES]) + _split_heads(q[:, LANES:2 * LANES])
    acc_scr[...] = jnp.zeros_like(acc_scr)

    def attn_step(j0, count, ends, state):
        kb = _key_rows(k_ref, j0, count)
        scores = [_dot_nt(kb[:, (hh // 2) * LANES:(hh // 2 + 1) * LANES], q_heads[hh]) for hh in range(N_HEADS_DSA)]
        weights = []
        for b in range(count):
            bias = bias_scr[j0 + b]
            upd = [_softmax_update(scores[hh][b * TK:(b + 1) * TK, :] + bias, *state[hh])
                   for hh in range(N_HEADS_DSA)]
            state = tuple((u[2], u[3]) for u in upd)
            weights.append([(u[0], u[1]) for u in upd])
        accs = [acc_scr[hh // 2, _head_rows(hh % 2), :] for hh in range(N_HEADS_DSA)]
        for b in range(count):
            for hh in range(N_HEADS_DSA):
                p, alpha = weights[b][hh]
                accs[hh] = alpha * accs[hh] + _dot(vt_ref[0, hh // 2, j0 + b, _head_rows(hh % 2), :],
                                                   p.astype(BF16))
        for hh in range(N_HEADS_DSA):
            acc_scr[hh // 2, _head_rows(hh % 2), :] = accs[hh]
        return state

    init = tuple((jnp.full((1, tq), -jnp.inf, F32), jnp.zeros((1, tq), F32)) for _ in range(N_HEADS_DSA))
    state = _walk_blocks(i + 1, attn_step, init)
    outs = []
    for pr in range(n_pairs):
        inv = jnp.concatenate([jnp.broadcast_to(1.0 / state[2 * pr + hd][1], (HEAD_DIM, tq)) for hd in range(2)],
                              axis=0)
        outs.append((acc_scr[pr] * inv).T)
    o_ref[0] = jnp.concatenate(outs, axis=1).astype(BF16)


def _dsa(zb, vt, misct, tri_lt, topk):
    b, s, _ = zb.shape
    nq, nk = s // TQ, s // TK
    n_pairs = W_DSA // LANES
    return pl.pallas_call(
        functools.partial(_dsa_body, topk=topk),
        grid=(b, nq),
        in_specs=[
            pl.BlockSpec((1, TQ, W_IDX), lambda bi, i: (bi, i, Z_IQ // W_IDX)),
            pl.BlockSpec((1, s, LANES), lambda bi, i: (bi, 0, Z_IK // LANES)),
            pl.BlockSpec((1, TQ, W_DSA), lambda bi, i: (bi, i, Z_CQ // W_DSA)),
            pl.BlockSpec((1, s, W_DSA), lambda bi, i: (bi, 0, Z_CK // W_DSA)),
            pl.BlockSpec((1, n_pairs, nk, LANES, TK), lambda bi, i: (bi, 0, 0, 0, 0)),
            pl.BlockSpec((1, 1, MISC_ROWS, TQ), lambda bi, i: (bi, i, 0, 0)),
            pl.BlockSpec((TK, TK), lambda bi, i: (0, 0)),
        ],
        out_specs=pl.BlockSpec((1, TQ, W_DSA), lambda bi, i: (bi, i, 0)),
        out_shape=jax.ShapeDtypeStruct((b, s, W_DSA), BF16),
        scratch_shapes=[
            pltpu.VMEM((nk, TK, TQ), jnp.int32),
            pltpu.VMEM((nk, TK, TQ), jnp.int16),
            pltpu.VMEM((nk, TK, TQ), jnp.int16),
            pltpu.VMEM((nk, TK, TQ), F32),
            pltpu.VMEM((n_pairs, LANES, TQ), F32),
        ],
        compiler_params=pltpu.CompilerParams(
            dimension_semantics=("parallel", "arbitrary"), vmem_limit_bytes=VMEM_LIMIT),
        name="dsa",
    )(zb, zb, zb, zb, vt, misct, tri_lt)


def _merge_body(x_ref, g_ref, of_ref, os_ref, oc_ref, wg_ref, bg_ref, wf_ref, ws_ref, wc_ref, wo_ref, o_ref):
    x = x_ref[...]
    d = x.shape[1]
    h = _rms_rows(x, g_ref[...]).astype(BF16)
    merged = jnp.zeros_like(x)
    for n, (o_br, w_br) in enumerate(((of_ref, wf_ref), (os_ref, ws_ref), (oc_ref, wc_ref))):
        gate = _dot(h, wg_ref[:, n * d:(n + 1) * d]) + bg_ref[n:n + 1, :]
        gate = 1.0 / (1.0 + jnp.exp(-gate))
        merged = merged + gate * _dot(o_br[...], w_br[...])
    o_ref[...] = x + _dot(merged.astype(BF16), wo_ref[...])


def _merge(x2, g, o_f, o_s, o_c, w_gates, b_gates, w_f, w_s, w_c, w_o):
    m, d = x2.shape
    tm = min(TM_MERGE, m)
    row = lambda w: pl.BlockSpec((tm, w), lambda i: (i, 0))
    const = lambda a: pl.BlockSpec(a.shape, lambda i: (0, 0))
    return pl.pallas_call(
        _merge_body,
        grid=(m // tm,),
        in_specs=[row(d), const(g), row(W_FOX), row(W_SB), row(W_DSA), const(w_gates), const(b_gates),
                  const(w_f), const(w_s), const(w_c), const(w_o)],
        out_specs=row(d),
        out_shape=jax.ShapeDtypeStruct((m, d), F32),
        compiler_params=pltpu.CompilerParams(
            dimension_semantics=("parallel",), vmem_limit_bytes=VMEM_LIMIT),
        name="merge",
    )(x2, g, o_f, o_s, o_c, w_gates, b_gates, w_f, w_s, w_c, w_o)


def _tile_heads(g, n):
    return jnp.tile(g.astype(F32), n)


def _constants(tm_proj):
    r = np.arange(W_FOX)
    bd = (r[:, None] // HEAD_DIM == r[None, :] // HEAD_DIM).astype(np.float32)
    rt = np.arange(tm_proj)
    tri_tok = (rt[None, :] <= rt[:, None]).astype(np.float32)
    rk = np.arange(TK)
    tri_ge = (rk[None, :] >= rk[:, None]).astype(np.float32)
    tri_lt = (rk[None, :] < rk[:, None]).astype(np.float32)

    n_src = W_FOX + N_BIAS_FEATS * LANES
    pq = np.zeros((n_src, W_FOX_AUG), np.float32)
    pk = np.zeros((n_src, W_FOX_AUG), np.float32)
    ones_q = np.zeros((1, W_FOX_AUG), np.float32)
    ones_k = np.zeros((1, W_FOX_AUG), np.float32)
    for hd in range(N_HEADS_FOX):
        for dd in range(HEAD_DIM):
            pq[hd * HEAD_DIM + dd, hd * LANES + dd] = 1.0
            pk[hd * HEAD_DIM + dd, hd * LANES + dd] = 1.0
        for part in range(N_BIAS_FEATS):
            src_row = W_FOX + part * LANES + L_FORGET + hd
            pq[src_row, hd * LANES + HEAD_DIM + part] = 1.0
            ones_q[0, hd * LANES + HEAD_DIM + N_BIAS_FEATS + part] = 1.0
            ones_k[0, hd * LANES + HEAD_DIM + part] = 1.0
            pk[src_row, hd * LANES + HEAD_DIM + N_BIAS_FEATS + part] = -1.0
    half = ROT_DIM // 2
    inv_freq = jnp.power(ROPE_THETA, -jnp.arange(half, dtype=F32) * 2.0 / ROT_DIM)
    dlane = np.arange(LANES) % HEAD_DIM
    freq = jnp.where(jnp.asarray(dlane < ROT_DIM), inv_freq[jnp.asarray(dlane % half)], 0.0)
    return dict(bd=jnp.asarray(bd, BF16), tri_tok=jnp.asarray(tri_tok, BF16), tri_ge=jnp.asarray(tri_ge, BF16),
                tri_lt=jnp.asarray(tri_lt, BF16), pq=jnp.asarray(pq, BF16), pk=jnp.asarray(pk, BF16),
                ones_q=jnp.asarray(ones_q, F32), ones_k=jnp.asarray(ones_k, F32),
                freq=freq.reshape(1, LANES).astype(F32))


def kernel(x, positions, ffn1_norm, ffn1_w_gate, ffn1_w_up, ffn1_w_down, mix_norm, w_in, b_forget, b_gates, q_norm_fox, k_norm_fox, q_norm_sb, k_norm_sb, q_norm_dsa, k_norm_dsa, w_branch_fox, w_branch_sb, w_branch_dsa, w_out, ffn2_norm, ffn2_w_gate, ffn2_w_up, ffn2_w_down):
    b, s, d = x.shape
    depth = w_in.shape[0]
    topk = min(TOPK_MAX, s // 4)
    assert s % (2 * TK) == 0 and TQ == TK and d % LANES == 0
    consts = _constants(min(TM_PROJ, s))
    tm_proj = min(TM_PROJ, s)
    pos3 = positions.reshape(b, s // tm_proj, 1, tm_proj)
    scale = HEAD_DIM ** -0.5
    idx_scale = IDX_DIM ** -0.5

    splits = (W_FOX, W_FOX, W_FOX, N_HEADS_FOX, W_SB, W_SB, W_SB, W_DSA, W_DSA, W_DSA,
              W_IDX, IDX_DIM, N_IDX_HEADS, d, d, d)
    offs = np.concatenate([[0], np.cumsum(splits)]).tolist()
    (o_qf, o_kf, o_vf, o_ff, o_qs, o_ks, o_vs, o_qc, o_kc, o_vc, o_qi, o_ki, o_wi, o_ga) = offs[:14]

    x2 = x.reshape(b * s, d)
    for l in range(depth):
        wl = w_in[l].astype(BF16)
        pad = jnp.zeros((d, LANES - IDX_DIM - N_HEADS_FOX - N_IDX_HEADS), BF16)
        w_proj = jnp.concatenate([
            wl[:, o_qf:o_ff], wl[:, o_qs:o_qi], wl[:, o_qi:o_ki],
            wl[:, o_ki:o_wi], wl[:, o_ff:o_qs], wl[:, o_wi:o_ga], pad], axis=1)
        w_gates = wl[:, o_ga:]
        ones = lambda n: jnp.ones((n,), F32)
        gains = jnp.concatenate([
            _tile_heads(q_norm_fox[l], N_HEADS_FOX) * scale, _tile_heads(k_norm_fox[l], N_HEADS_FOX), ones(W_FOX),
            _tile_heads(q_norm_sb[l], N_HEADS_SB) * scale, _tile_heads(k_norm_sb[l], N_HEADS_SB), ones(W_SB),
            _tile_heads(q_norm_dsa[l], N_HEADS_DSA) * scale, _tile_heads(k_norm_dsa[l], N_HEADS_DSA), ones(W_DSA),
            ones(W_IDX) * idx_scale, ones(LANES)]).reshape(1, N_PROJ)
        fbias = jnp.zeros((LANES,), F32).at[L_FORGET:L_FORGET + N_HEADS_FOX].set(b_forget[l].astype(F32))
        fbias = fbias.reshape(1, LANES)

        x2 = _ffn(x2, ffn1_norm[l].reshape(1, d), ffn1_w_gate[l].astype(BF16), ffn1_w_up[l].astype(BF16),
                  ffn1_w_down[l].astype(BF16))
        zb, vt_f, vt_s, vt_c, misct = _proj(x2.reshape(b, s, d), pos3, mix_norm[l].reshape(1, d), w_proj, gains,
                                            consts, fbias)
        o_f = _fox(zb, vt_f)
        o_s = _sb(zb, vt_s, consts["tri_ge"])
        o_c = _dsa(zb, vt_c, misct, consts["tri_lt"], topk)
        x2 = _merge(x2, mix_norm[l].reshape(1, d), o_f.reshape(b * s, W_FOX), o_s.reshape(b * s, W_SB),
                    o_c.reshape(b * s, W_DSA), w_gates, b_gates[l].astype(F32),
                    w_branch_fox[l].astype(BF16), w_branch_sb[l].astype(BF16), w_branch_dsa[l].astype(BF16),
                    w_out[l].astype(BF16))
        x2 = _ffn(x2, ffn2_norm[l].reshape(1, d), ffn2_w_gate[l].astype(BF16), ffn2_w_up[l].astype(BF16),
                  ffn2_w_down[l].astype(BF16))
    return x2.reshape(b, s, d)
```

```python
import functools

import jax
import jax.numpy as jnp
import numpy as np
from jax import lax
from jax.experimental import pallas as pl
from jax.experimental.pallas import tpu as pltpu

F32 = jnp.float32
BF16 = jnp.bfloat16

HEAD_DIM = 64
N_HEADS_FOX = 6
N_HEADS_SB = 6
N_HEADS_DSA = 4
N_IDX_HEADS = 4
IDX_DIM = 64
TOPK_MAX = 256
ROPE_THETA = 500000.0
ROT_DIM = HEAD_DIM // 4
EPS = 1e-6

LANES = 128
SUBLANES = 8
PACK16 = 2 * SUBLANES
N_COUNT_ACC = 4
W_FOX = N_HEADS_FOX * HEAD_DIM
W_SB = N_HEADS_SB * HEAD_DIM
W_DSA = N_HEADS_DSA * HEAD_DIM
W_IDX = N_IDX_HEADS * IDX_DIM

C_FQ, C_FK, C_FV = 0, W_FOX, 2 * W_FOX
C_SQ, C_SK, C_SV = 3 * W_FOX, 3 * W_FOX + W_SB, 3 * W_FOX + 2 * W_SB
C_CQ = 3 * W_FOX + 3 * W_SB
C_CK, C_CV = C_CQ + W_DSA, C_CQ + 2 * W_DSA
C_IQ = C_CQ + 3 * W_DSA
C_MISC = C_IQ + W_IDX
N_PROJ = C_MISC + LANES
L_FORGET = IDX_DIM
L_WIDX = IDX_DIM + N_HEADS_FOX
MISC_ROWS = 16

W_FOX_AUG = N_HEADS_FOX * LANES
Z_FQ, Z_FK = 0, W_FOX_AUG
Z_SQ, Z_SK = 2 * W_FOX_AUG, 2 * W_FOX_AUG + W_SB
Z_CQ = 2 * W_FOX_AUG + 2 * W_SB
Z_CK = Z_CQ + W_DSA
Z_IQ = Z_CK + W_DSA
Z_IK = Z_IQ + W_IDX
N_ZB = Z_IK + LANES
N_BIAS_FEATS = 3

TQ = 256
TK = 256
TQ_DSA = 2 * TK
TM_PROJ = 512
TM_FFN = 512
TM_MERGE = 512
NEG_BIG = -1e30
EXP_UNDERFLOW = -110.0
VMEM_LIMIT = 56 * 1024 * 1024

_NT = (((1,), (1,)), ((), ()))


def _dot(a, b):
    return jnp.dot(a, b, preferred_element_type=F32)


def _dot_nt(a, b):
    return lax.dot_general(a, b, _NT, preferred_element_type=F32)


def _split2(x):
    hi = x.astype(BF16)
    lo = (x - hi.astype(F32)).astype(BF16)
    return hi, lo


def _split3(x):
    hi = x.astype(BF16)
    r = x - hi.astype(F32)
    mid = r.astype(BF16)
    lo = (r - mid.astype(F32)).astype(BF16)
    return hi, mid, lo


def _rms_rows(x, g):
    ms = jnp.mean(x * x, axis=-1, keepdims=True)
    return x * lax.rsqrt(ms + EPS) * g


def _softplus(z):
    return jnp.maximum(z, 0.0) + jnp.log(1.0 + jnp.exp(-jnp.abs(z)))


def _ffn_body(x_ref, g_ref, wg_ref, wu_ref, wd_ref, o_ref, *, tf):
    x = x_ref[...]
    h = _rms_rows(x, g_ref[...]).astype(BF16)
    acc = None
    for c in range(0, wg_ref.shape[1], tf):
        a = _dot(h, wg_ref[:, c:c + tf])
        u = _dot(h, wu_ref[:, c:c + tf])
        p = (a * (1.0 / (1.0 + jnp.exp(-a))) * u).astype(BF16)
        part = _dot(p, wd_ref[c:c + tf, :])
        acc = part if acc is None else acc + part
    o_ref[...] = x + 0.5 * acc


def _ffn_tile_f(d_ff):
    best = LANES
    for t in range(LANES, d_ff + 1, LANES):
        if d_ff % t == 0 and t <= 1536:
            best = t
    return best


def _resident(shape):
    return pl.BlockSpec(shape, lambda *_: (0,) * len(shape), pipeline_mode=pl.Buffered(1))


def _ffn(x2, g, wg, wu, wd):
    m, d = x2.shape
    d_ff = wg.shape[1]
    tm = min(TM_FFN, m)
    return pl.pallas_call(
        functools.partial(_ffn_body, tf=_ffn_tile_f(d_ff)),
        grid=(m // tm,),
        in_specs=[
            pl.BlockSpec((tm, d), lambda i: (i, 0)),
            _resident((1, d)), _resident((d, d_ff)), _resident((d, d_ff)), _resident((d_ff, d)),
        ],
        out_specs=pl.BlockSpec((tm, d), lambda i: (i, 0)),
        out_shape=jax.ShapeDtypeStruct((m, d), F32),
        compiler_params=pltpu.CompilerParams(
            dimension_semantics=("parallel",), vmem_limit_bytes=VMEM_LIMIT),
        name="ffn",
    )(x2, g, wg, wu, wd)


def _proj_body(x_ref, pos_ref, g_ref, w_ref, gains_ref, bd_ref, freq_ref, fbias_ref, tri_ref,
               pq_ref, pk_ref, ones_q_ref, ones_k_ref,
               zb_ref, vtf_ref, vts_ref, vtc_ref, misct_ref, carry_scr):
    t = pl.program_id(1)
    tm = x_ref.shape[1]
    h = _rms_rows(x_ref[0], g_ref[...]).astype(BF16)

    lane = lax.broadcasted_iota(jnp.int32, (1, LANES), 1)
    d_in_head = lane % HEAD_DIM
    pos_cols = jnp.broadcast_to(pos_ref[0, 0].astype(F32), (LANES, tm)).T
    first_half = d_in_head < ROT_DIM // 2
    shared = {}

    def rope(x):
        if "cos" not in shared:
            ang = pos_cols * freq_ref[...]
            shared["cos"] = jnp.cos(ang)
            shared["sin"] = jnp.where(first_half, -jnp.sin(ang), jnp.sin(ang))
        partner = jnp.where(first_half, pltpu.roll(x, LANES - ROT_DIM // 2, 1), pltpu.roll(x, ROT_DIM // 2, 1))
        return x * shared["cos"] + partner * shared["sin"]

    def head_norm(z, c0, width):
        hi, lo = _split2(z * z)
        bd = bd_ref[:width, :width]
        ss = _dot(hi, bd) + _dot(lo, bd)
        return z * lax.rsqrt(ss * (1.0 / HEAD_DIM) + EPS) * gains_ref[:, c0:c0 + width]

    def project(c0, width):
        return _dot(h, w_ref[:, c0:c0 + width])

    def finish_heads(z, c0, width, norm, rot):
        if norm:
            z = head_norm(z, c0, width)
        else:
            z = z * gains_ref[:, c0:c0 + width]
        if rot:
            z = jnp.concatenate([rope(z[:, c:c + LANES]) for c in range(0, width, LANES)], axis=1)
        return z

    def finish_index_key(zm):
        ki = rope(zm)
        zb_ref[0, :, Z_IK:Z_IK + LANES] = jnp.where(lane < IDX_DIM, ki, pltpu.roll(ki, IDX_DIM, 1)).astype(BF16)

    def finish_misc(zm):
        shared["zm"] = zm

        @pl.when(t == 0)
        def _():
            carry_scr[...] = jnp.zeros_like(carry_scr)

        logf = -_softplus(-(zm + fbias_ref[...]))
        hi, mid, lo = _split3(logf)
        tri = tri_ref[...]
        cum = _dot(tri, hi) + _dot(tri, mid) + _dot(tri, lo) + carry_scr[0:1, :]
        carry_scr[...] = jnp.broadcast_to(cum[tm - 1:tm, :], carry_scr.shape)
        is_forget = (lane >= L_FORGET) & (lane < L_FORGET + N_HEADS_FOX)
        is_widx = (lane >= L_WIDX) & (lane < L_WIDX + N_IDX_HEADS)
        misc = jnp.where(is_forget, cum, jnp.where(is_widx, zm * (N_IDX_HEADS ** -0.5), 0.0))
        misc_t = misc.T
        for c in range(tm // TQ_DSA):
            misct_ref[0, c] = misc_t[L_FORGET:L_FORGET + MISC_ROWS, c * TQ_DSA:(c + 1) * TQ_DSA]
        shared["cum"] = _split3(cum)

    def finish_fox(z, c0, z0, p_ref, ones_ref):
        zn = finish_heads(z, c0, W_FOX, True, False).astype(BF16)
        src = jnp.concatenate((zn,) + shared["cum"], axis=1)
        zb_ref[0, :, z0:z0 + W_FOX_AUG] = (_dot(src, p_ref[...]) + ones_ref[...]).astype(BF16)

    def finish_plain(z, c0, z0, width, norm, rot):
        zb_ref[0, :, z0:z0 + width] = finish_heads(z, c0, width, norm, rot).astype(BF16)

    def finish_values(z, vt_ref, width):
        zt = z.T
        for p in range(width // LANES):
            for c in range(tm // TK):
                vt_ref[0, p, c] = zt[p * LANES:(p + 1) * LANES, c * TK:(c + 1) * TK].astype(BF16)

    P = functools.partial
    stages = [
        (C_MISC, LANES, finish_misc),
        (C_FV, W_FOX, P(finish_values, vt_ref=vtf_ref, width=W_FOX)),
        (C_FQ, W_FOX, P(finish_fox, c0=C_FQ, z0=Z_FQ, p_ref=pq_ref, ones_ref=ones_q_ref)),
        (C_FK, W_FOX, P(finish_fox, c0=C_FK, z0=Z_FK, p_ref=pk_ref, ones_ref=ones_k_ref)),
        (C_SV, W_SB, P(finish_values, vt_ref=vts_ref, width=W_SB)),
        (C_SQ, W_SB, P(finish_plain, c0=C_SQ, z0=Z_SQ, width=W_SB, norm=True, rot=False)),
        (C_SK, W_SB, P(finish_plain, c0=C_SK, z0=Z_SK, width=W_SB, norm=True, rot=False)),
        (C_CV, W_DSA, P(finish_values, vt_ref=vtc_ref, width=W_DSA)),
        (C_CQ, W_DSA, P(finish_plain, c0=C_CQ, z0=Z_CQ, width=W_DSA, norm=True, rot=True)),
        (C_CK, W_DSA, P(finish_plain, c0=C_CK, z0=Z_CK, width=W_DSA, norm=True, rot=True)),
        (C_IQ, W_IDX, P(finish_plain, c0=C_IQ, z0=Z_IQ, width=W_IDX, norm=False, rot=True)),
    ]
    pending = None
    for c0, width, finish in stages:
        z = project(c0, width)
        if pending is not None:
            pending[0](pending[1])
        pending = (finish, z)
    pending[0](pending[1])
    finish_index_key(shared["zm"])


def _proj(x, pos3, g, w, gains, consts, fbias):
    b, s, d = x.shape
    tm = min(TM_PROJ, s)
    const = lambda a: pl.BlockSpec(a.shape, lambda bi, ti: (0,) * a.ndim)
    vt_spec = lambda n: pl.BlockSpec((1, n, tm // TK, LANES, TK), lambda bi, ti: (bi, 0, ti, 0, 0))
    vt_shape = lambda n: jax.ShapeDtypeStruct((b, n, s // TK, LANES, TK), BF16)
    cs = (consts["bd"], consts["freq"], fbias, consts["tri_tok"], consts["pq"], consts["pk"],
          consts["ones_q"], consts["ones_k"])
    return pl.pallas_call(
        _proj_body,
        grid=(b, s // tm),
        in_specs=[
            pl.BlockSpec((1, tm, d), lambda bi, ti: (bi, ti, 0)),
            pl.BlockSpec((1, 1, 1, tm), lambda bi, ti: (bi, ti, 0, 0)),
            const(g), const(w), const(gains),
        ] + [const(a) for a in cs],
        out_specs=[
            pl.BlockSpec((1, tm, N_ZB), lambda bi, ti: (bi, ti, 0)),
            vt_spec(W_FOX // LANES), vt_spec(W_SB // LANES), vt_spec(W_DSA // LANES),
            pl.BlockSpec((1, tm // TQ_DSA, MISC_ROWS, TQ_DSA), lambda bi, ti: (bi, ti, 0, 0)),
        ],
        out_shape=[
            jax.ShapeDtypeStruct((b, s, N_ZB), BF16),
            vt_shape(W_FOX // LANES), vt_shape(W_SB // LANES), vt_shape(W_DSA // LANES),
            jax.ShapeDtypeStruct((b, s // TQ_DSA, MISC_ROWS, TQ_DSA), F32),
        ],
        scratch_shapes=[pltpu.VMEM((SUBLANES, LANES), F32)],
        compiler_params=pltpu.CompilerParams(
            dimension_semantics=("parallel", "arbitrary"), vmem_limit_bytes=VMEM_LIMIT),
        name="proj",
    )(x, pos3, g, w, gains, *cs)


def _head_masks():
    lane = lax.broadcasted_iota(jnp.int32, (1, LANES), 1)
    return lane < HEAD_DIM, lane >= HEAD_DIM


def _split_heads(q128):
    lo_half, hi_half = _head_masks()
    zero = jnp.zeros_like(q128)
    return jnp.where(lo_half, q128, zero), jnp.where(hi_half, q128, zero)


def _visible(tk, tq, strict, offset=0):
    key = lax.broadcasted_iota(jnp.int32, (tk, tq), 0) + offset
    qry = lax.broadcasted_iota(jnp.int32, (tk, tq), 1)
    return key < qry if strict else key <= qry


def _softmax_update(s, m, l):
    m_new = jnp.maximum(m, jnp.max(s, axis=0, keepdims=True))
    alpha = jnp.exp(m - m_new)
    p = jnp.exp(s - m_new)
    return p, alpha, m_new, alpha * l + jnp.sum(p, axis=0, keepdims=True)


def _head_rows(hd):
    return slice(hd * HEAD_DIM, (hd + 1) * HEAD_DIM)


def _walk_blocks(n_blocks, step, state, always_even=False):
    last = n_blocks - 1
    state = lax.fori_loop(0, last // 2, lambda n, st: step(2 * n, 2, False, st), state)
    if always_even:
        return step(last - 1, 2, True, state)
    return lax.cond(last % 2 == 1,
                    lambda st: step(last - 1, 2, True, st),
                    lambda st: step(last, 1, True, st),
                    state)


def _key_rows(ref, j0, count):
    return ref[0, pl.ds(pl.multiple_of(j0 * TK, TK), count * TK), :]


def _fox_body(q_ref, k_ref, vt_ref, o_ref, acc_scr):
    i = pl.program_id(1)
    tq = q_ref.shape[1]
    nh = N_HEADS_FOX
    q_heads = [q_ref[0, :, hd * LANES:(hd + 1) * LANES] for hd in range(nh)]
    acc_scr[...] = jnp.zeros_like(acc_scr)

    def step(j0, count, ends, state):
        kb = _key_rows(k_ref, j0, count)
        scores = [_dot_nt(kb[:, hd * LANES:(hd + 1) * LANES], q_heads[hd]) for hd in range(nh)]
        weights = []
        for b in range(count):
            s = [sh[b * TK:(b + 1) * TK, :] for sh in scores]
            if ends and b == count - 1:
                vis = _visible(TK, tq, False)
                s = [jnp.where(vis, sh, -jnp.inf) for sh in s]
            upd = [_softmax_update(s[hd], *state[hd]) for hd in range(nh)]
            state = tuple((u[2], u[3]) for u in upd)
            weights.append([(u[0], u[1]) for u in upd])
        accs = [acc_scr[hd // 2, _head_rows(hd % 2), :] for hd in range(nh)]
        for b in range(count):
            for hd in range(nh):
                p, alpha = weights[b][hd]
                accs[hd] = alpha * accs[hd] + _dot(vt_ref[0, hd // 2, j0 + b, _head_rows(hd % 2), :], p.astype(BF16))
        for hd in range(nh):
            acc_scr[hd // 2, _head_rows(hd % 2), :] = accs[hd]
        return state

    init = tuple((jnp.full((1, tq), -jnp.inf, F32), jnp.zeros((1, tq), F32)) for _ in range(nh))
    state = _walk_blocks(i + 1, step, init)
    for pr in range(nh // 2):
        inv = jnp.concatenate([jnp.broadcast_to(1.0 / state[2 * pr + hd][1], (HEAD_DIM, tq)) for hd in range(2)],
                              axis=0)
        o_ref[0, :, pr * LANES:(pr + 1) * LANES] = (acc_scr[pr] * inv).T.astype(BF16)


def _fox(zb, vt):
    b, s, _ = zb.shape
    nq, nk = s // TQ, s // TK
    n_pairs = W_FOX // LANES
    return pl.pallas_call(
        _fox_body,
        grid=(b, nq),
        in_specs=[
            pl.BlockSpec((1, TQ, W_FOX_AUG), lambda bi, i: (bi, i, Z_FQ // W_FOX_AUG)),
            pl.BlockSpec((1, s, W_FOX_AUG), lambda bi, i: (bi, 0, Z_FK // W_FOX_AUG)),
            pl.BlockSpec((1, n_pairs, nk, LANES, TK), lambda bi, i: (bi, 0, 0, 0, 0)),
        ],
        out_specs=pl.BlockSpec((1, TQ, W_FOX), lambda bi, i: (bi, i, 0)),
        out_shape=jax.ShapeDtypeStruct((b, s, W_FOX), BF16),
        scratch_shapes=[pltpu.VMEM((n_pairs, LANES, TQ), F32)],
        compiler_params=pltpu.CompilerParams(
            dimension_semantics=("parallel", "arbitrary"), vmem_limit_bytes=VMEM_LIMIT),
        name="fox",
    )(zb, zb, vt)


def _sb_body(q_ref, k_ref, vt_ref, tri_ref, o_ref, acc_scr):
    i = pl.program_id(1)
    tq = q_ref.shape[1]
    nh = N_HEADS_SB
    q_heads = []
    for pr in range(nh // 2):
        q_heads.extend(_split_heads(q_ref[0, :, pr * LANES:(pr + 1) * LANES]))
    acc_scr[...] = jnp.zeros_like(acc_scr)

    def step(j0, count, diagonal, carry):
        kb = _key_rows(k_ref, j0, count)
        tri = tri_ref[...]
        zs = [_dot_nt(kb[:, (hd // 2) * LANES:(hd // 2 + 1) * LANES], q_heads[hd]) for hd in range(nh)]
        strict = _visible(TK, tq, True)
        order = list(range(count - 1, -1, -1))
        suffixes = {}
        for b in order:
            rows = slice(b * TK, (b + 1) * TK)
            loms = [-_softplus(z[rows, :]) for z in zs]
            if diagonal and b == count - 1:
                loms = [jnp.where(strict, lom, 0.0) for lom in loms]
            splits = [_split2(lom) for lom in loms]
            suffixes[b] = [_dot(tri, hi) + _dot(tri, lo) for hi, lo in splits]
        accs = [acc_scr[hd // 2, _head_rows(hd % 2), :] for hd in range(nh)]
        carry = list(carry)
        for b in order:
            rows = slice(b * TK, (b + 1) * TK)
            for hd in range(nh):
                a = jnp.exp(zs[hd][rows, :] + suffixes[b][hd] + carry[hd])
                if diagonal and b == count - 1:
                    a = jnp.where(strict, a, 0.0)
                accs[hd] = accs[hd] + _dot(vt_ref[0, hd // 2, j0 + b, _head_rows(hd % 2), :], a.astype(BF16))
                carry[hd] = carry[hd] + suffixes[b][hd][0:1, :]
        for hd in range(nh):
            acc_scr[hd // 2, _head_rows(hd % 2), :] = accs[hd]
        return tuple(carry)

    zero = tuple(jnp.zeros((1, tq), F32) for _ in range(nh))
    carry = lax.cond(i > 0, lambda c: step(i - 1, 2, True, c), lambda c: step(i, 1, True, c), zero)

    def live(state):
        j, carry = state
        worst = functools.reduce(jnp.maximum, carry)
        return jnp.logical_and(j >= 0, jnp.max(worst) > EXP_UNDERFLOW)

    def body(state):
        j, carry = state
        return j - 1, step(j, 1, False, carry)

    lax.while_loop(live, body, (i - 2, carry))
    for pr in range(nh // 2):
        o_ref[0, :, pr * LANES:(pr + 1) * LANES] = acc_scr[pr].T.astype(BF16)


def _sb(zb, vt, tri_ge):
    b, s, _ = zb.shape
    nq, nk = s // TQ, s // TK
    n_pairs = W_SB // LANES
    return pl.pallas_call(
        _sb_body,
        grid=(b, nq),
        in_specs=[
            pl.BlockSpec((1, TQ, W_SB), lambda bi, i: (bi, i, Z_SQ // W_SB)),
            pl.BlockSpec((1, s, W_SB), lambda bi, i: (bi, 0, Z_SK // W_SB)),
            pl.BlockSpec((1, n_pairs, nk, LANES, TK), lambda bi, i: (bi, 0, 0, 0, 0)),
            pl.BlockSpec((TK, TK), lambda bi, i: (0, 0)),
        ],
        out_specs=pl.BlockSpec((1, TQ, W_SB), lambda bi, i: (bi, i, 0)),
        out_shape=jax.ShapeDtypeStruct((b, s, W_SB), BF16),
        scratch_shapes=[pltpu.VMEM((n_pairs, LANES, TQ), F32)],
        compiler_params=pltpu.CompilerParams(
            dimension_semantics=("parallel", "arbitrary"), vmem_limit_bytes=VMEM_LIMIT),
        name="sb",
    )(zb, zb, vt, tri_ge)


def _dsa_body(iq_ref, ik_ref, q_ref, k_ref, vt_ref, misct_ref, tri_ref, o_ref,
              keys_scr, hi_scr, lo_scr, bias_scr, acc_scr, *, topk):
    i = pl.program_id(1)
    tq = q_ref.shape[1]
    qb = tq // TK
    n_vis = qb * (i + 1)
    n_pairs = N_HEADS_DSA // 2
    i16_min, i16_max = -2 ** 15, 2 ** 15 - 1

    iq = iq_ref[0]
    iq_heads = _split_heads(iq[:, 0:LANES]) + _split_heads(iq[:, LANES:2 * LANES])
    w_rows = [misct_ref[0, 0, L_WIDX - L_FORGET + hh:L_WIDX - L_FORGET + hh + 1, :] for hh in range(N_IDX_HEADS)]

    def score_step(j0, count, ends, carry):
        kb = _key_rows(ik_ref, j0, count)
        dots = [_dot_nt(kb, iq_heads[hh]) for hh in range(N_IDX_HEADS)]
        for b in range(count):
            rows = slice(b * TK, (b + 1) * TK)
            sc = jnp.zeros((TK, tq), F32)
            for hh in range(N_IDX_HEADS):
                sc = sc + w_rows[hh] * jnp.maximum(dots[hh][rows, :], 0.0)
            d = b - (count - qb)
            if ends and d >= 0:
                sc = jnp.where(_visible(TK, tq, False, d * TK), sc, -jnp.inf)
            bits = lax.bitcast_convert_type(sc, jnp.int32)
            key = jnp.where(bits < 0, bits ^ jnp.int32(0x7FFFFFFF), bits)
            keys_scr[j0 + b] = key
            hi_scr[j0 + b] = lax.shift_right_arithmetic(key, 16).astype(jnp.int16)
            lo_scr[j0 + b] = ((key & 0xFFFF) - 2 ** 15).astype(jnp.int16)
        return carry

    _walk_blocks(n_vis, score_step, 0, always_even=True)

    n_block_pairs = n_vis // 2

    def count16(ref, cand):
        cand_b = jnp.broadcast_to(cand.astype(jnp.int16), (PACK16, tq))
        one, zero = jnp.ones((), jnp.int16), jnp.zeros((), jnp.int16)

        def body(n, accs):
            out = list(accs)
            for b in range(2):
                c = jnp.where(ref[2 * n + b].reshape(TK // PACK16, PACK16, tq) >= cand_b, one, zero)
                for a in range(N_COUNT_ACC):
                    g = [c[r] for r in range(a, TK // PACK16, N_COUNT_ACC)]
                    while len(g) > 1:
                        g = [g[r] + g[r + 1] for r in range(0, len(g), 2)]
                    out[a] = out[a] + g[0]
            return tuple(out)

        zeros = jnp.zeros((PACK16, tq), jnp.int16)
        accs = lax.fori_loop(0, n_block_pairs, body, (zeros,) * N_COUNT_ACC)
        acc = functools.reduce(lambda x, y: x + y, accs)
        return jnp.sum(acc.astype(jnp.int32).astype(F32), axis=0, keepdims=True)

    def count16_gt(ref, thr16):
        return jnp.where(thr16 == i16_max, 0.0, count16(ref, jnp.minimum(thr16 + 1, i16_max)))

    def bisect16(ref, want):
        def it(n, thr16):
            cand = thr16 + lax.shift_left(jnp.int32(1), 15 - n)
            return jnp.where(count16(ref, cand) >= want, cand, thr16)
        return lax.fori_loop(0, 16, it, jnp.full((1, tq), i16_min, jnp.int32))

    kf = jnp.float32(topk)
    t_hi = bisect16(hi_scr, kf)
    n_gt_hi = count16_gt(hi_scr, t_hi)
    t_hi_b = jnp.broadcast_to(t_hi.astype(jnp.int16), (PACK16, tq))

    def low_body(n, c):
        shape = (TK // PACK16, PACK16, tq)
        for j in (2 * n, 2 * n + 1):
            lo = jnp.where(hi_scr[j].reshape(shape) == t_hi_b, lo_scr[j].reshape(shape), jnp.int16(i16_min))
            lo_scr[j] = lo.reshape(TK, tq)
        return c

    lax.fori_loop(0, n_block_pairs, low_body, 0)
    t_lo = bisect16(lo_scr, kf - n_gt_hi)
    thr = t_hi * 2 ** 16 + t_lo + 2 ** 15
    n_gt_lo = count16_gt(lo_scr, t_lo)
    n_eq = count16(lo_scr, t_lo) - n_gt_lo
    need = kf - n_gt_hi - n_gt_lo

    def plain_step(j, diag):
        sel = keys_scr[j] >= thr
        if diag is not None:
            sel = sel & _visible(TK, tq, False, diag * TK)
        bias_scr[j] = jnp.where(sel, 0.0, NEG_BIG)

    def tie_step(j, run, diag):
        kblk = keys_scr[j]
        eq = kblk == thr
        eqf = jnp.where(eq, 1.0, 0.0)
        before = _dot(tri_ref[...], eqf.astype(BF16)) + run
        sel = (kblk > thr) | (eq & (before < need))
        if diag is not None:
            sel = sel & _visible(TK, tq, False, diag * TK)
        bias_scr[j] = jnp.where(sel, 0.0, NEG_BIG)
        return run + jnp.sum(eqf, axis=0, keepdims=True)

    def plain_bias():
        def body(j, c):
            plain_step(j, None)
            return c
        lax.fori_loop(0, n_vis - qb, body, 0)
        for d in range(qb):
            plain_step(n_vis - qb + d, d)

    def tie_bias():
        run = lax.fori_loop(0, n_vis - qb, lambda j, r: tie_step(j, r, None), jnp.zeros((1, tq), F32))
        for d in range(qb):
            run = tie_step(n_vis - qb + d, run, d)

    lax.cond(jnp.max(n_eq - need) > 0.0, tie_bias, plain_bias)

    q = q_ref[0]
    q_heads = _split_heads(q[:, 0:LANES]) + _split_heads(q[:, LANES:2 * LANES])
    acc_scr[...] = jnp.zeros_like(acc_scr)

    def attn_step(j0, count, ends, state):
        kb = _key_rows(k_ref, j0, count)
        scores = [_dot_nt(kb[:, (hh // 2) * LANES:(hh // 2 + 1) * LANES], q_heads[hh]) for hh in range(N_HEADS_DSA)]
        weights = []
        for b in range(count):
            bias = bias_scr[j0 + b]
            upd = [_softmax_update(scores[hh][b * TK:(b + 1) * TK, :] + bias, *state[hh])
                   for hh in range(N_HEADS_DSA)]
            state = tuple((u[2], u[3]) for u in upd)
            weights.append([(u[0], u[1]) for u in upd])
        accs = [acc_scr[hh // 2, _head_rows(hh % 2), :] for hh in range(N_HEADS_DSA)]
        for b in range(count):
            for hh in range(N_HEADS_DSA):
                p, alpha = weights[b][hh]
                accs[hh] = alpha * accs[hh] + _dot(vt_ref[0, hh // 2, j0 + b, _head_rows(hh % 2), :],
                                                   p.astype(BF16))
        for hh in range(N_HEADS_DSA):
            acc_scr[hh // 2, _head_rows(hh % 2), :] = accs[hh]
        return state

    init = tuple((jnp.full((1, tq), -jnp.inf, F32), jnp.zeros((1, tq), F32)) for _ in range(N_HEADS_DSA))
    state = _walk_blocks(n_vis, attn_step, init, always_even=True)
    outs = []
    for pr in range(n_pairs):
        inv = jnp.concatenate([jnp.broadcast_to(1.0 / state[2 * pr + hd][1], (HEAD_DIM, tq)) for hd in range(2)],
                              axis=0)
        outs.append((acc_scr[pr] * inv).T)
    o_ref[0] = jnp.concatenate(outs, axis=1).astype(BF16)


def _dsa(zb, vt, misct, tri_lt, topk):
    b, s, _ = zb.shape
    nq, nk = s // TQ_DSA, s // TK
    n_pairs = W_DSA // LANES
    return pl.pallas_call(
        functools.partial(_dsa_body, topk=topk),
        grid=(b, nq),
        in_specs=[
            pl.BlockSpec((1, TQ_DSA, W_IDX), lambda bi, i: (bi, i, Z_IQ // W_IDX)),
            pl.BlockSpec((1, s, LANES), lambda bi, i: (bi, 0, Z_IK // LANES)),
            pl.BlockSpec((1, TQ_DSA, W_DSA), lambda bi, i: (bi, i, Z_CQ // W_DSA)),
            pl.BlockSpec((1, s, W_DSA), lambda bi, i: (bi, 0, Z_CK // W_DSA)),
            pl.BlockSpec((1, n_pairs, nk, LANES, TK), lambda bi, i: (bi, 0, 0, 0, 0)),
            pl.BlockSpec((1, 1, MISC_ROWS, TQ_DSA), lambda bi, i: (bi, i, 0, 0)),
            pl.BlockSpec((TK, TK), lambda bi, i: (0, 0)),
        ],
        out_specs=pl.BlockSpec((1, TQ_DSA, W_DSA), lambda bi, i: (bi, i, 0)),
        out_shape=jax.ShapeDtypeStruct((b, s, W_DSA), BF16),
        scratch_shapes=[
            pltpu.VMEM((nk, TK, TQ_DSA), jnp.int32),
            pltpu.VMEM((nk, TK, TQ_DSA), jnp.int16),
            pltpu.VMEM((nk, TK, TQ_DSA), jnp.int16),
            pltpu.VMEM((nk, TK, TQ_DSA), F32),
            pltpu.VMEM((n_pairs, LANES, TQ_DSA), F32),
        ],
        compiler_params=pltpu.CompilerParams(
            dimension_semantics=("parallel", "arbitrary"), vmem_limit_bytes=VMEM_LIMIT),
        name="dsa",
    )(zb, zb, zb, zb, vt, misct, tri_lt)


def _merge_body(x_ref, g_ref, of_ref, os_ref, oc_ref, wg_ref, bg_ref, wf_ref, ws_ref, wc_ref, wo_ref, o_ref):
    x = x_ref[...]
    d = x.shape[1]
    h = _rms_rows(x, g_ref[...]).astype(BF16)
    merged = jnp.zeros_like(x)
    for n, (o_br, w_br) in enumerate(((of_ref, wf_ref), (os_ref, ws_ref), (oc_ref, wc_ref))):
        gate = _dot(h, wg_ref[:, n * d:(n + 1) * d]) + bg_ref[n:n + 1, :]
        gate = 1.0 / (1.0 + jnp.exp(-gate))
        merged = merged + gate * _dot(o_br[...], w_br[...])
    o_ref[...] = x + _dot(merged.astype(BF16), wo_ref[...])


def _merge(x2, g, o_f, o_s, o_c, w_gates, b_gates, w_f, w_s, w_c, w_o):
    m, d = x2.shape
    tm = min(TM_MERGE, m)
    row = lambda w: pl.BlockSpec((tm, w), lambda i: (i, 0))
    const = lambda a: pl.BlockSpec(a.shape, lambda i: (0, 0))
    return pl.pallas_call(
        _merge_body,
        grid=(m // tm,),
        in_specs=[row(d), const(g), row(W_FOX), row(W_SB), row(W_DSA), const(w_gates), const(b_gates),
                  const(w_f), const(w_s), const(w_c), const(w_o)],
        out_specs=row(d),
        out_shape=jax.ShapeDtypeStruct((m, d), F32),
        compiler_params=pltpu.CompilerParams(
            dimension_semantics=("parallel",), vmem_limit_bytes=VMEM_LIMIT),
        name="merge",
    )(x2, g, o_f, o_s, o_c, w_gates, b_gates, w_f, w_s, w_c, w_o)


def _tile_heads(g, n):
    return jnp.tile(g.astype(F32), n)


def _constants(tm_proj):
    r = np.arange(W_FOX)
    bd = (r[:, None] // HEAD_DIM == r[None, :] // HEAD_DIM).astype(np.float32)
    rt = np.arange(tm_proj)
    tri_tok = (rt[None, :] <= rt[:, None]).astype(np.float32)
    rk = np.arange(TK)
    tri_ge = (rk[None, :] >= rk[:, None]).astype(np.float32)
    tri_lt = (rk[None, :] < rk[:, None]).astype(np.float32)

    n_src = W_FOX + N_BIAS_FEATS * LANES
    pq = np.zeros((n_src, W_FOX_AUG), np.float32)
    pk = np.zeros((n_src, W_FOX_AUG), np.float32)
    ones_q = np.zeros((1, W_FOX_AUG), np.float32)
    ones_k = np.zeros((1, W_FOX_AUG), np.float32)
    for hd in range(N_HEADS_FOX):
        for dd in range(HEAD_DIM):
            pq[hd * HEAD_DIM + dd, hd * LANES + dd] = 1.0
            pk[hd * HEAD_DIM + dd, hd * LANES + dd] = 1.0
        for part in range(N_BIAS_FEATS):
            src_row = W_FOX + part * LANES + L_FORGET + hd
            pq[src_row, hd * LANES + HEAD_DIM + part] = 1.0
            ones_q[0, hd * LANES + HEAD_DIM + N_BIAS_FEATS + part] = 1.0
            ones_k[0, hd * LANES + HEAD_DIM + part] = 1.0
            pk[src_row, hd * LANES + HEAD_DIM + N_BIAS_FEATS + part] = -1.0
    half = ROT_DIM // 2
    inv_freq = jnp.power(ROPE_THETA, -jnp.arange(half, dtype=F32) * 2.0 / ROT_DIM)
    dlane = np.arange(LANES) % HEAD_DIM
    freq = jnp.where(jnp.asarray(dlane < ROT_DIM), inv_freq[jnp.asarray(dlane % half)], 0.0)
    return dict(bd=jnp.asarray(bd, BF16), tri_tok=jnp.asarray(tri_tok, BF16), tri_ge=jnp.asarray(tri_ge, BF16),
                tri_lt=jnp.asarray(tri_lt, BF16), pq=jnp.asarray(pq, BF16), pk=jnp.asarray(pk, BF16),
                ones_q=jnp.asarray(ones_q, F32), ones_k=jnp.asarray(ones_k, F32),
                freq=freq.reshape(1, LANES).astype(F32))


def kernel(x, positions, ffn1_norm, ffn1_w_gate, ffn1_w_up, ffn1_w_down, mix_norm, w_in, b_forget, b_gates, q_norm_fox, k_norm_fox, q_norm_sb, k_norm_sb, q_norm_dsa, k_norm_dsa, w_branch_fox, w_branch_sb, w_branch_dsa, w_out, ffn2_norm, ffn2_w_gate, ffn2_w_up, ffn2_w_down):
    b, s, d = x.shape
    depth = w_in.shape[0]
    topk = min(TOPK_MAX, s // 4)
    assert s % TQ_DSA == 0 and TQ == TK and TQ_DSA % (2 * TK) == 0 and d % LANES == 0
    consts = _constants(min(TM_PROJ, s))
    tm_proj = min(TM_PROJ, s)
    pos3 = positions.reshape(b, s // tm_proj, 1, tm_proj)
    scale = HEAD_DIM ** -0.5
    idx_scale = IDX_DIM ** -0.5

    splits = (W_FOX, W_FOX, W_FOX, N_HEADS_FOX, W_SB, W_SB, W_SB, W_DSA, W_DSA, W_DSA,
              W_IDX, IDX_DIM, N_IDX_HEADS, d, d, d)
    offs = np.concatenate([[0], np.cumsum(splits)]).tolist()
    (o_qf, o_kf, o_vf, o_ff, o_qs, o_ks, o_vs, o_qc, o_kc, o_vc, o_qi, o_ki, o_wi, o_ga) = offs[:14]

    x2 = x.reshape(b * s, d)
    for l in range(depth):
        wl = w_in[l].astype(BF16)
        pad = jnp.zeros((d, LANES - IDX_DIM - N_HEADS_FOX - N_IDX_HEADS), BF16)
        w_proj = jnp.concatenate([
            wl[:, o_qf:o_ff], wl[:, o_qs:o_qi], wl[:, o_qi:o_ki],
            wl[:, o_ki:o_wi], wl[:, o_ff:o_qs], wl[:, o_wi:o_ga], pad], axis=1)
        w_gates = wl[:, o_ga:]
        ones = lambda n: jnp.ones((n,), F32)
        gains = jnp.concatenate([
            _tile_heads(q_norm_fox[l], N_HEADS_FOX) * scale, _tile_heads(k_norm_fox[l], N_HEADS_FOX), ones(W_FOX),
            _tile_heads(q_norm_sb[l], N_HEADS_SB) * scale, _tile_heads(k_norm_sb[l], N_HEADS_SB), ones(W_SB),
            _tile_heads(q_norm_dsa[l], N_HEADS_DSA) * scale, _tile_heads(k_norm_dsa[l], N_HEADS_DSA), ones(W_DSA),
            ones(W_IDX) * idx_scale, ones(LANES)]).reshape(1, N_PROJ)
        fbias = jnp.zeros((LANES,), F32).at[L_FORGET:L_FORGET + N_HEADS_FOX].set(b_forget[l].astype(F32))
        fbias = fbias.reshape(1, LANES)

        x2 = _ffn(x2, ffn1_norm[l].reshape(1, d), ffn1_w_gate[l].astype(BF16), ffn1_w_up[l].astype(BF16),
                  ffn1_w_down[l].astype(BF16))
        zb, vt_f, vt_s, vt_c, misct = _proj(x2.reshape(b, s, d), pos3, mix_norm[l].reshape(1, d), w_proj, gains,
                                            consts, fbias)
        o_f = _fox(zb, vt_f)
        o_s = _sb(zb, vt_s, consts["tri_ge"])
        o_c = _dsa(zb, vt_c, misct, consts["tri_lt"], topk)
        x2 = _merge(x2, mix_norm[l].reshape(1, d), o_f.reshape(b * s, W_FOX), o_s.reshape(b * s, W_SB),
                    o_c.reshape(b * s, W_DSA), w_gates, b_gates[l].astype(F32),
                    w_branch_fox[l].astype(BF16), w_branch_sb[l].astype(BF16), w_branch_dsa[l].astype(BF16),
                    w_out[l].astype(BF16))
        x2 = _ffn(x2, ffn2_norm[l].reshape(1, d), ffn2_w_gate[l].astype(BF16), ffn2_w_up[l].astype(BF16),
                  ffn2_w_down[l].astype(BF16))
    return x2.reshape(b, s, d)
```

```python
import functools

import jax
import jax.numpy as jnp
import numpy as np
from jax import lax
from jax.experimental import pallas as pl
from jax.experimental.pallas import tpu as pltpu

F32 = jnp.float32
BF16 = jnp.bfloat16

HEAD_DIM = 64
N_HEADS_FOX = 6
N_HEADS_SB = 6
N_HEADS_DSA = 4
N_IDX_HEADS = 4
IDX_DIM = 64
TOPK_MAX = 256
ROPE_THETA = 500000.0
ROT_DIM = HEAD_DIM // 4
EPS = 1e-6

LANES = 128
SUBLANES = 8
PACK16 = 2 * SUBLANES
N_COUNT_ACC = 4
W_FOX = N_HEADS_FOX * HEAD_DIM
W_SB = N_HEADS_SB * HEAD_DIM
W_DSA = N_HEADS_DSA * HEAD_DIM
W_IDX = N_IDX_HEADS * IDX_DIM

C_FQ, C_FK, C_FV = 0, W_FOX, 2 * W_FOX
C_SQ, C_SK, C_SV = 3 * W_FOX, 3 * W_FOX + W_SB, 3 * W_FOX + 2 * W_SB
C_CQ = 3 * W_FOX + 3 * W_SB
C_CK, C_CV = C_CQ + W_DSA, C_CQ + 2 * W_DSA
C_IQ = C_CQ + 3 * W_DSA
C_MISC = C_IQ + W_IDX
N_PROJ = C_MISC + LANES
L_FORGET = IDX_DIM
L_WIDX = IDX_DIM + N_HEADS_FOX
MISC_ROWS = 16

W_FOX_AUG = N_HEADS_FOX * LANES
Z_FQ, Z_FK = 0, W_FOX_AUG
Z_SQ, Z_SK = 2 * W_FOX_AUG, 2 * W_FOX_AUG + W_SB
Z_CQ = 2 * W_FOX_AUG + 2 * W_SB
Z_CK = Z_CQ + W_DSA
Z_IQ = Z_CK + W_DSA
Z_IK = Z_IQ + W_IDX
N_ZB = Z_IK + LANES
N_BIAS_FEATS = 3

TQ = 256
TK = 256
TQ_WIDE = 2 * TK
TM_PROJ = 512
TM_FFN = 512
TM_MERGE = 512
NEG_BIG = -1e30
EXP_UNDERFLOW = -110.0
VMEM_LIMIT = 56 * 1024 * 1024

_NT = (((1,), (1,)), ((), ()))


def _dot(a, b):
    return jnp.dot(a, b, preferred_element_type=F32)


def _dot_nt(a, b):
    return lax.dot_general(a, b, _NT, preferred_element_type=F32)


def _split2(x):
    hi = x.astype(BF16)
    lo = (x - hi.astype(F32)).astype(BF16)
    return hi, lo


def _split3(x):
    hi = x.astype(BF16)
    r = x - hi.astype(F32)
    mid = r.astype(BF16)
    lo = (r - mid.astype(F32)).astype(BF16)
    return hi, mid, lo


def _rms_rows(x, g):
    ms = jnp.mean(x * x, axis=-1, keepdims=True)
    return x * lax.rsqrt(ms + EPS) * g


def _softplus(z):
    return jnp.maximum(z, 0.0) + jnp.log(1.0 + jnp.exp(-jnp.abs(z)))


def _ffn_body(x_ref, g_ref, wg_ref, wu_ref, wd_ref, o_ref, *, tf):
    x = x_ref[...]
    h = _rms_rows(x, g_ref[...]).astype(BF16)
    acc = None
    for c in range(0, wg_ref.shape[1], tf):
        a = _dot(h, wg_ref[:, c:c + tf])
        u = _dot(h, wu_ref[:, c:c + tf])
        p = (a * (1.0 / (1.0 + jnp.exp(-a))) * u).astype(BF16)
        part = _dot(p, wd_ref[c:c + tf, :])
        acc = part if acc is None else acc + part
    o_ref[...] = x + 0.5 * acc


def _ffn_tile_f(d_ff):
    best = LANES
    for t in range(LANES, d_ff + 1, LANES):
        if d_ff % t == 0 and t <= 1536:
            best = t
    return best


def _resident(shape):
    return pl.BlockSpec(shape, lambda *_: (0,) * len(shape), pipeline_mode=pl.Buffered(1))


def _ffn(x2, g, wg, wu, wd):
    m, d = x2.shape
    d_ff = wg.shape[1]
    tm = min(TM_FFN, m)
    return pl.pallas_call(
        functools.partial(_ffn_body, tf=_ffn_tile_f(d_ff)),
        grid=(m // tm,),
        in_specs=[
            pl.BlockSpec((tm, d), lambda i: (i, 0)),
            _resident((1, d)), _resident((d, d_ff)), _resident((d, d_ff)), _resident((d_ff, d)),
        ],
        out_specs=pl.BlockSpec((tm, d), lambda i: (i, 0)),
        out_shape=jax.ShapeDtypeStruct((m, d), F32),
        compiler_params=pltpu.CompilerParams(
            dimension_semantics=("parallel",), vmem_limit_bytes=VMEM_LIMIT),
        name="ffn",
    )(x2, g, wg, wu, wd)


def _proj_body(x_ref, pos_ref, g_ref, w_ref, gains_ref, bd_ref, freq_ref, fbias_ref, tri_ref,
               pq_ref, pk_ref, ones_q_ref, ones_k_ref,
               zb_ref, vtf_ref, vts_ref, vtc_ref, misct_ref, carry_scr):
    t = pl.program_id(1)
    tm = x_ref.shape[1]
    h = _rms_rows(x_ref[0], g_ref[...]).astype(BF16)

    lane = lax.broadcasted_iota(jnp.int32, (1, LANES), 1)
    d_in_head = lane % HEAD_DIM
    pos_cols = jnp.broadcast_to(pos_ref[0, 0].astype(F32), (LANES, tm)).T
    first_half = d_in_head < ROT_DIM // 2
    shared = {}

    def rope(x):
        if "cos" not in shared:
            ang = pos_cols * freq_ref[...]
            shared["cos"] = jnp.cos(ang)
            shared["sin"] = jnp.where(first_half, -jnp.sin(ang), jnp.sin(ang))
        partner = jnp.where(first_half, pltpu.roll(x, LANES - ROT_DIM // 2, 1), pltpu.roll(x, ROT_DIM // 2, 1))
        return x * shared["cos"] + partner * shared["sin"]

    def head_norm(z, c0, width):
        hi, lo = _split2(z * z)
        bd = bd_ref[:width, :width]
        ss = _dot(hi, bd) + _dot(lo, bd)
        return z * lax.rsqrt(ss * (1.0 / HEAD_DIM) + EPS) * gains_ref[:, c0:c0 + width]

    def project(c0, width):
        return _dot(h, w_ref[:, c0:c0 + width])

    def finish_heads(z, c0, width, norm, rot):
        if norm:
            z = head_norm(z, c0, width)
        else:
            z = z * gains_ref[:, c0:c0 + width]
        if rot:
            z = jnp.concatenate([rope(z[:, c:c + LANES]) for c in range(0, width, LANES)], axis=1)
        return z

    def finish_index_key(zm):
        ki = rope(zm)
        zb_ref[0, :, Z_IK:Z_IK + LANES] = jnp.where(lane < IDX_DIM, ki, pltpu.roll(ki, IDX_DIM, 1)).astype(BF16)

    def finish_misc(zm):
        shared["zm"] = zm

        @pl.when(t == 0)
        def _():
            carry_scr[...] = jnp.zeros_like(carry_scr)

        logf = -_softplus(-(zm + fbias_ref[...]))
        hi, mid, lo = _split3(logf)
        tri = tri_ref[...]
        cum = _dot(tri, hi) + _dot(tri, mid) + _dot(tri, lo) + carry_scr[0:1, :]
        carry_scr[...] = jnp.broadcast_to(cum[tm - 1:tm, :], carry_scr.shape)
        is_forget = (lane >= L_FORGET) & (lane < L_FORGET + N_HEADS_FOX)
        is_widx = (lane >= L_WIDX) & (lane < L_WIDX + N_IDX_HEADS)
        misc = jnp.where(is_forget, cum, jnp.where(is_widx, zm * (N_IDX_HEADS ** -0.5), 0.0))
        misc_t = misc.T
        for c in range(tm // TQ_WIDE):
            misct_ref[0, c] = misc_t[L_FORGET:L_FORGET + MISC_ROWS, c * TQ_WIDE:(c + 1) * TQ_WIDE]
        shared["cum"] = _split3(cum)

    def finish_fox(z, c0, z0, p_ref, ones_ref):
        zn = finish_heads(z, c0, W_FOX, True, False).astype(BF16)
        src = jnp.concatenate((zn,) + shared["cum"], axis=1)
        zb_ref[0, :, z0:z0 + W_FOX_AUG] = (_dot(src, p_ref[...]) + ones_ref[...]).astype(BF16)

    def finish_plain(z, c0, z0, width, norm, rot):
        zb_ref[0, :, z0:z0 + width] = finish_heads(z, c0, width, norm, rot).astype(BF16)

    def finish_values(z, vt_ref, width):
        zt = z.T
        for p in range(width // LANES):
            for c in range(tm // TK):
                vt_ref[0, p, c] = zt[p * LANES:(p + 1) * LANES, c * TK:(c + 1) * TK].astype(BF16)

    P = functools.partial
    stages = [
        (C_MISC, LANES, finish_misc),
        (C_FV, W_FOX, P(finish_values, vt_ref=vtf_ref, width=W_FOX)),
        (C_FQ, W_FOX, P(finish_fox, c0=C_FQ, z0=Z_FQ, p_ref=pq_ref, ones_ref=ones_q_ref)),
        (C_FK, W_FOX, P(finish_fox, c0=C_FK, z0=Z_FK, p_ref=pk_ref, ones_ref=ones_k_ref)),
        (C_SV, W_SB, P(finish_values, vt_ref=vts_ref, width=W_SB)),
        (C_SQ, W_SB, P(finish_plain, c0=C_SQ, z0=Z_SQ, width=W_SB, norm=True, rot=False)),
        (C_SK, W_SB, P(finish_plain, c0=C_SK, z0=Z_SK, width=W_SB, norm=True, rot=False)),
        (C_CV, W_DSA, P(finish_values, vt_ref=vtc_ref, width=W_DSA)),
        (C_CQ, W_DSA, P(finish_plain, c0=C_CQ, z0=Z_CQ, width=W_DSA, norm=True, rot=True)),
        (C_CK, W_DSA, P(finish_plain, c0=C_CK, z0=Z_CK, width=W_DSA, norm=True, rot=True)),
        (C_IQ, W_IDX, P(finish_plain, c0=C_IQ, z0=Z_IQ, width=W_IDX, norm=False, rot=True)),
    ]
    pending = None
    for c0, width, finish in stages:
        z = project(c0, width)
        if pending is not None:
            pending[0](pending[1])
        pending = (finish, z)
    pending[0](pending[1])
    finish_index_key(shared["zm"])


def _proj(x, pos3, g, w, gains, consts, fbias):
    b, s, d = x.shape
    tm = min(TM_PROJ, s)
    const = lambda a: pl.BlockSpec(a.shape, lambda bi, ti: (0,) * a.ndim)
    vt_spec = lambda n: pl.BlockSpec((1, n, tm // TK, LANES, TK), lambda bi, ti: (bi, 0, ti, 0, 0))
    vt_shape = lambda n: jax.ShapeDtypeStruct((b, n, s // TK, LANES, TK), BF16)
    cs = (consts["bd"], consts["freq"], fbias, consts["tri_tok"], consts["pq"], consts["pk"],
          consts["ones_q"], consts["ones_k"])
    return pl.pallas_call(
        _proj_body,
        grid=(b, s // tm),
        in_specs=[
            pl.BlockSpec((1, tm, d), lambda bi, ti: (bi, ti, 0)),
            pl.BlockSpec((1, 1, 1, tm), lambda bi, ti: (bi, ti, 0, 0)),
            const(g), const(w), const(gains),
        ] + [const(a) for a in cs],
        out_specs=[
            pl.BlockSpec((1, tm, N_ZB), lambda bi, ti: (bi, ti, 0)),
            vt_spec(W_FOX // LANES), vt_spec(W_SB // LANES), vt_spec(W_DSA // LANES),
            pl.BlockSpec((1, tm // TQ_WIDE, MISC_ROWS, TQ_WIDE), lambda bi, ti: (bi, ti, 0, 0)),
        ],
        out_shape=[
            jax.ShapeDtypeStruct((b, s, N_ZB), BF16),
            vt_shape(W_FOX // LANES), vt_shape(W_SB // LANES), vt_shape(W_DSA // LANES),
            jax.ShapeDtypeStruct((b, s // TQ_WIDE, MISC_ROWS, TQ_WIDE), F32),
        ],
        scratch_shapes=[pltpu.VMEM((SUBLANES, LANES), F32)],
        compiler_params=pltpu.CompilerParams(
            dimension_semantics=("parallel", "arbitrary"), vmem_limit_bytes=VMEM_LIMIT),
        name="proj",
    )(x, pos3, g, w, gains, *cs)


def _head_masks():
    lane = lax.broadcasted_iota(jnp.int32, (1, LANES), 1)
    return lane < HEAD_DIM, lane >= HEAD_DIM


def _split_heads(q128):
    lo_half, hi_half = _head_masks()
    zero = jnp.zeros_like(q128)
    return jnp.where(lo_half, q128, zero), jnp.where(hi_half, q128, zero)


def _visible(tk, tq, strict, offset=0):
    key = lax.broadcasted_iota(jnp.int32, (tk, tq), 0) + offset
    qry = lax.broadcasted_iota(jnp.int32, (tk, tq), 1)
    return key < qry if strict else key <= qry


def _softmax_update(s, m, l):
    m_new = jnp.maximum(m, jnp.max(s, axis=0, keepdims=True))
    alpha = jnp.exp(m - m_new)
    p = jnp.exp(s - m_new)
    return p, alpha, m_new, alpha * l + jnp.sum(p, axis=0, keepdims=True)


def _head_rows(hd):
    return slice(hd * HEAD_DIM, (hd + 1) * HEAD_DIM)


def _walk_blocks(n_blocks, step, state, always_even=False):
    last = n_blocks - 1
    state = lax.fori_loop(0, last // 2, lambda n, st: step(2 * n, 2, False, st), state)
    if always_even:
        return step(last - 1, 2, True, state)
    return lax.cond(last % 2 == 1,
                    lambda st: step(last - 1, 2, True, st),
                    lambda st: step(last, 1, True, st),
                    state)


def _key_rows(ref, j0, count):
    return ref[0, pl.ds(pl.multiple_of(j0 * TK, TK), count * TK), :]


def _fox_body(q_ref, k_ref, vt_ref, o_ref, acc_scr):
    i = pl.program_id(1)
    tq = q_ref.shape[1]
    qb = tq // TK
    nh = N_HEADS_FOX
    q_heads = [q_ref[0, :, hd * LANES:(hd + 1) * LANES] for hd in range(nh)]
    acc_scr[...] = jnp.zeros_like(acc_scr)

    def step(j0, count, ends, state):
        kb = _key_rows(k_ref, j0, count)
        scores = [_dot_nt(kb[:, hd * LANES:(hd + 1) * LANES], q_heads[hd]) for hd in range(nh)]
        weights = []
        for b in range(count):
            s = [sh[b * TK:(b + 1) * TK, :] for sh in scores]
            d = b - (count - qb)
            if ends and d >= 0:
                vis = _visible(TK, tq, False, d * TK)
                s = [jnp.where(vis, sh, -jnp.inf) for sh in s]
            upd = [_softmax_update(s[hd], *state[hd]) for hd in range(nh)]
            state = tuple((u[2], u[3]) for u in upd)
            weights.append([(u[0], u[1]) for u in upd])
        accs = [acc_scr[hd // 2, _head_rows(hd % 2), :] for hd in range(nh)]
        for b in range(count):
            for hd in range(nh):
                p, alpha = weights[b][hd]
                accs[hd] = alpha * accs[hd] + _dot(vt_ref[0, hd // 2, j0 + b, _head_rows(hd % 2), :], p.astype(BF16))
        for hd in range(nh):
            acc_scr[hd // 2, _head_rows(hd % 2), :] = accs[hd]
        return state

    init = tuple((jnp.full((1, tq), -jnp.inf, F32), jnp.zeros((1, tq), F32)) for _ in range(nh))
    state = _walk_blocks(qb * (i + 1), step, init, always_even=qb % 2 == 0)
    for pr in range(nh // 2):
        inv = jnp.concatenate([jnp.broadcast_to(1.0 / state[2 * pr + hd][1], (HEAD_DIM, tq)) for hd in range(2)],
                              axis=0)
        o_ref[0, :, pr * LANES:(pr + 1) * LANES] = (acc_scr[pr] * inv).T.astype(BF16)


def _fox(zb, vt):
    b, s, _ = zb.shape
    nq, nk = s // TQ_WIDE, s // TK
    n_pairs = W_FOX // LANES
    return pl.pallas_call(
        _fox_body,
        grid=(b, nq),
        in_specs=[
            pl.BlockSpec((1, TQ_WIDE, W_FOX_AUG), lambda bi, i: (bi, i, Z_FQ // W_FOX_AUG)),
            pl.BlockSpec((1, s, W_FOX_AUG), lambda bi, i: (bi, 0, Z_FK // W_FOX_AUG)),
            pl.BlockSpec((1, n_pairs, nk, LANES, TK), lambda bi, i: (bi, 0, 0, 0, 0)),
        ],
        out_specs=pl.BlockSpec((1, TQ_WIDE, W_FOX), lambda bi, i: (bi, i, 0)),
        out_shape=jax.ShapeDtypeStruct((b, s, W_FOX), BF16),
        scratch_shapes=[pltpu.VMEM((n_pairs, LANES, TQ_WIDE), F32)],
        compiler_params=pltpu.CompilerParams(
            dimension_semantics=("parallel", "arbitrary"), vmem_limit_bytes=VMEM_LIMIT),
        name="fox",
    )(zb, zb, vt)


def _sb_body(q_ref, k_ref, vt_ref, tri_ref, o_ref, acc_scr):
    i = pl.program_id(1)
    tq = q_ref.shape[1]
    nh = N_HEADS_SB
    q_heads = []
    for pr in range(nh // 2):
        q_heads.extend(_split_heads(q_ref[0, :, pr * LANES:(pr + 1) * LANES]))
    acc_scr[...] = jnp.zeros_like(acc_scr)

    def step(j0, count, diagonal, carry):
        kb = _key_rows(k_ref, j0, count)
        tri = tri_ref[...]
        zs = [_dot_nt(kb[:, (hd // 2) * LANES:(hd // 2 + 1) * LANES], q_heads[hd]) for hd in range(nh)]
        strict = _visible(TK, tq, True)
        order = list(range(count - 1, -1, -1))
        suffixes = {}
        for b in order:
            rows = slice(b * TK, (b + 1) * TK)
            loms = [-_softplus(z[rows, :]) for z in zs]
            if diagonal and b == count - 1:
                loms = [jnp.where(strict, lom, 0.0) for lom in loms]
            splits = [_split2(lom) for lom in loms]
            suffixes[b] = [_dot(tri, hi) + _dot(tri, lo) for hi, lo in splits]
        accs = [acc_scr[hd // 2, _head_rows(hd % 2), :] for hd in range(nh)]
        carry = list(carry)
        for b in order:
            rows = slice(b * TK, (b + 1) * TK)
            for hd in range(nh):
                a = jnp.exp(zs[hd][rows, :] + suffixes[b][hd] + carry[hd])
                if diagonal and b == count - 1:
                    a = jnp.where(strict, a, 0.0)
                accs[hd] = accs[hd] + _dot(vt_ref[0, hd // 2, j0 + b, _head_rows(hd % 2), :], a.astype(BF16))
                carry[hd] = carry[hd] + suffixes[b][hd][0:1, :]
        for hd in range(nh):
            acc_scr[hd // 2, _head_rows(hd % 2), :] = accs[hd]
        return tuple(carry)

    zero = tuple(jnp.zeros((1, tq), F32) for _ in range(nh))
    carry = lax.cond(i > 0, lambda c: step(i - 1, 2, True, c), lambda c: step(i, 1, True, c), zero)

    def live(state):
        j, carry = state
        worst = functools.reduce(jnp.maximum, carry)
        return jnp.logical_and(j >= 0, jnp.max(worst) > EXP_UNDERFLOW)

    def body(state):
        j, carry = state
        return j - 1, step(j, 1, False, carry)

    lax.while_loop(live, body, (i - 2, carry))
    for pr in range(nh // 2):
        o_ref[0, :, pr * LANES:(pr + 1) * LANES] = acc_scr[pr].T.astype(BF16)


def _sb(zb, vt, tri_ge):
    b, s, _ = zb.shape
    nq, nk = s // TQ, s // TK
    n_pairs = W_SB // LANES
    return pl.pallas_call(
        _sb_body,
        grid=(b, nq),
        in_specs=[
            pl.BlockSpec((1, TQ, W_SB), lambda bi, i: (bi, i, Z_SQ // W_SB)),
            pl.BlockSpec((1, s, W_SB), lambda bi, i: (bi, 0, Z_SK // W_SB)),
            pl.BlockSpec((1, n_pairs, nk, LANES, TK), lambda bi, i: (bi, 0, 0, 0, 0)),
            pl.BlockSpec((TK, TK), lambda bi, i: (0, 0)),
        ],
        out_specs=pl.BlockSpec((1, TQ, W_SB), lambda bi, i: (bi, i, 0)),
        out_shape=jax.ShapeDtypeStruct((b, s, W_SB), BF16),
        scratch_shapes=[pltpu.VMEM((n_pairs, LANES, TQ), F32)],
        compiler_params=pltpu.CompilerParams(
            dimension_semantics=("parallel", "arbitrary"), vmem_limit_bytes=VMEM_LIMIT),
        name="sb",
    )(zb, zb, vt, tri_ge)


def _dsa_body(iq_ref, ik_ref, q_ref, k_ref, vt_ref, misct_ref, tri_ref, o_ref,
              keys_scr, hi_scr, lo_scr, bias_scr, acc_scr, *, topk):
    i = pl.program_id(1)
    tq = q_ref.shape[1]
    qb = tq // TK
    n_vis = qb * (i + 1)
    n_pairs = N_HEADS_DSA // 2
    i16_min, i16_max = -2 ** 15, 2 ** 15 - 1

    iq = iq_ref[0]
    iq_heads = _split_heads(iq[:, 0:LANES]) + _split_heads(iq[:, LANES:2 * LANES])
    w_rows = [misct_ref[0, 0, L_WIDX - L_FORGET + hh:L_WIDX - L_FORGET + hh + 1, :] for hh in range(N_IDX_HEADS)]

    def score_step(j0, count, ends, carry):
        kb = _key_rows(ik_ref, j0, count)
        dots = [_dot_nt(kb, iq_heads[hh]) for hh in range(N_IDX_HEADS)]
        for b in range(count):
            rows = slice(b * TK, (b + 1) * TK)
            sc = jnp.zeros((TK, tq), F32)
            for hh in range(N_IDX_HEADS):
                sc = sc + w_rows[hh] * jnp.maximum(dots[hh][rows, :], 0.0)
            d = b - (count - qb)
            if ends and d >= 0:
                sc = jnp.where(_visible(TK, tq, False, d * TK), sc, -jnp.inf)
            bits = lax.bitcast_convert_type(sc, jnp.int32)
            key = jnp.where(bits < 0, bits ^ jnp.int32(0x7FFFFFFF), bits)
            keys_scr[j0 + b] = key
            hi_scr[j0 + b] = lax.shift_right_arithmetic(key, 16).astype(jnp.int16)
            lo_scr[j0 + b] = ((key & 0xFFFF) - 2 ** 15).astype(jnp.int16)
        return carry

    _walk_blocks(n_vis, score_step, 0, always_even=True)

    n_block_pairs = n_vis // 2

    def count16(ref, cand):
        cand_b = jnp.broadcast_to(cand.astype(jnp.int16), (PACK16, tq))
        one, zero = jnp.ones((), jnp.int16), jnp.zeros((), jnp.int16)

        def body(n, accs):
            out = list(accs)
            for b in range(2):
                c = jnp.where(ref[2 * n + b].reshape(TK // PACK16, PACK16, tq) >= cand_b, one, zero)
                for a in range(N_COUNT_ACC):
                    g = [c[r] for r in range(a, TK // PACK16, N_COUNT_ACC)]
                    while len(g) > 1:
                        g = [g[r] + g[r + 1] for r in range(0, len(g), 2)]
                    out[a] = out[a] + g[0]
            return tuple(out)

        zeros = jnp.zeros((PACK16, tq), jnp.int16)
        accs = lax.fori_loop(0, n_block_pairs, body, (zeros,) * N_COUNT_ACC)
        acc = functools.reduce(lambda x, y: x + y, accs)
        return jnp.sum(acc.astype(jnp.int32).astype(F32), axis=0, keepdims=True)

    def count16_gt(ref, thr16):
        return jnp.where(thr16 == i16_max, 0.0, count16(ref, jnp.minimum(thr16 + 1, i16_max)))

    def bisect16(ref, want):
        def it(n, thr16):
            cand = thr16 + lax.shift_left(jnp.int32(1), 15 - n)
            return jnp.where(count16(ref, cand) >= want, cand, thr16)
        return lax.fori_loop(0, 16, it, jnp.full((1, tq), i16_min, jnp.int32))

    kf = jnp.float32(topk)
    t_hi = bisect16(hi_scr, kf)
    n_gt_hi = count16_gt(hi_scr, t_hi)
    t_hi_b = jnp.broadcast_to(t_hi.astype(jnp.int16), (PACK16, tq))

    def low_body(n, c):
        shape = (TK // PACK16, PACK16, tq)
        for j in (2 * n, 2 * n + 1):
            lo = jnp.where(hi_scr[j].reshape(shape) == t_hi_b, lo_scr[j].reshape(shape), jnp.int16(i16_min))
            lo_scr[j] = lo.reshape(TK, tq)
        return c

    lax.fori_loop(0, n_block_pairs, low_body, 0)
    t_lo = bisect16(lo_scr, kf - n_gt_hi)
    thr = t_hi * 2 ** 16 + t_lo + 2 ** 15
    n_gt_lo = count16_gt(lo_scr, t_lo)
    n_eq = count16(lo_scr, t_lo) - n_gt_lo
    need = kf - n_gt_hi - n_gt_lo

    def plain_step(j, diag):
        sel = keys_scr[j] >= thr
        if diag is not None:
            sel = sel & _visible(TK, tq, False, diag * TK)
        bias_scr[j] = jnp.where(sel, 0.0, NEG_BIG)

    def tie_step(j, run, diag):
        kblk = keys_scr[j]
        eq = kblk == thr
        eqf = jnp.where(eq, 1.0, 0.0)
        before = _dot(tri_ref[...], eqf.astype(BF16)) + run
        sel = (kblk > thr) | (eq & (before < need))
        if diag is not None:
            sel = sel & _visible(TK, tq, False, diag * TK)
        bias_scr[j] = jnp.where(sel, 0.0, NEG_BIG)
        return run + jnp.sum(eqf, axis=0, keepdims=True)

    def plain_bias():
        def body(j, c):
            plain_step(j, None)
            return c
        lax.fori_loop(0, n_vis - qb, body, 0)
        for d in range(qb):
            plain_step(n_vis - qb + d, d)

    def tie_bias():
        run = lax.fori_loop(0, n_vis - qb, lambda j, r: tie_step(j, r, None), jnp.zeros((1, tq), F32))
        for d in range(qb):
            run = tie_step(n_vis - qb + d, run, d)

    lax.cond(jnp.max(n_eq - need) > 0.0, tie_bias, plain_bias)

    q = q_ref[0]
    q_heads = _split_heads(q[:, 0:LANES]) + _split_heads(q[:, LANES:2 * LANES])
    acc_scr[...] = jnp.zeros_like(acc_scr)

    def attn_step(j0, count, ends, state):
        kb = _key_rows(k_ref, j0, count)
        scores = [_dot_nt(kb[:, (hh // 2) * LANES:(hh // 2 + 1) * LANES], q_heads[hh]) for hh in range(N_HEADS_DSA)]
        weights = []
        for b in range(count):
            bias = bias_scr[j0 + b]
            upd = [_softmax_update(scores[hh][b * TK:(b + 1) * TK, :] + bias, *state[hh])
                   for hh in range(N_HEADS_DSA)]
            state = tuple((u[2], u[3]) for u in upd)
            weights.append([(u[0], u[1]) for u in upd])
        accs = [acc_scr[hh // 2, _head_rows(hh % 2), :] for hh in range(N_HEADS_DSA)]
        for b in range(count):
            for hh in range(N_HEADS_DSA):
                p, alpha = weights[b][hh]
                accs[hh] = alpha * accs[hh] + _dot(vt_ref[0, hh // 2, j0 + b, _head_rows(hh % 2), :],
                                                   p.astype(BF16))
        for hh in range(N_HEADS_DSA):
            acc_scr[hh // 2, _head_rows(hh % 2), :] = accs[hh]
        return state

    init = tuple((jnp.full((1, tq), -jnp.inf, F32), jnp.zeros((1, tq), F32)) for _ in range(N_HEADS_DSA))
    state = _walk_blocks(n_vis, attn_step, init, always_even=True)
    outs = []
    for pr in range(n_pairs):
        inv = jnp.concatenate([jnp.broadcast_to(1.0 / state[2 * pr + hd][1], (HEAD_DIM, tq)) for hd in range(2)],
                              axis=0)
        outs.append((acc_scr[pr] * inv).T)
    o_ref[0] = jnp.concatenate(outs, axis=1).astype(BF16)


def _dsa(zb, vt, misct, tri_lt, topk):
    b, s, _ = zb.shape
    nq, nk = s // TQ_WIDE, s // TK
    n_pairs = W_DSA // LANES
    return pl.pallas_call(
        functools.partial(_dsa_body, topk=topk),
        grid=(b, nq),
        in_specs=[
            pl.BlockSpec((1, TQ_WIDE, W_IDX), lambda bi, i: (bi, i, Z_IQ // W_IDX)),
            pl.BlockSpec((1, s, LANES), lambda bi, i: (bi, 0, Z_IK // LANES)),
            pl.BlockSpec((1, TQ_WIDE, W_DSA), lambda bi, i: (bi, i, Z_CQ // W_DSA)),
            pl.BlockSpec((1, s, W_DSA), lambda bi, i: (bi, 0, Z_CK // W_DSA)),
            pl.BlockSpec((1, n_pairs, nk, LANES, TK), lambda bi, i: (bi, 0, 0, 0, 0)),
            pl.BlockSpec((1, 1, MISC_ROWS, TQ_WIDE), lambda bi, i: (bi, i, 0, 0)),
            pl.BlockSpec((TK, TK), lambda bi, i: (0, 0)),
        ],
        out_specs=pl.BlockSpec((1, TQ_WIDE, W_DSA), lambda bi, i: (bi, i, 0)),
        out_shape=jax.ShapeDtypeStruct((b, s, W_DSA), BF16),
        scratch_shapes=[
            pltpu.VMEM((nk, TK, TQ_WIDE), jnp.int32),
            pltpu.VMEM((nk, TK, TQ_WIDE), jnp.int16),
            pltpu.VMEM((nk, TK, TQ_WIDE), jnp.int16),
            pltpu.VMEM((nk, TK, TQ_WIDE), F32),
            pltpu.VMEM((n_pairs, LANES, TQ_WIDE), F32),
        ],
        compiler_params=pltpu.CompilerParams(
            dimension_semantics=("parallel", "arbitrary"), vmem_limit_bytes=VMEM_LIMIT),
        name="dsa",
    )(zb, zb, zb, zb, vt, misct, tri_lt)


def _merge_body(x_ref, g_ref, of_ref, os_ref, oc_ref, wg_ref, bg_ref, wf_ref, ws_ref, wc_ref, wo_ref, o_ref):
    x = x_ref[...]
    d = x.shape[1]
    h = _rms_rows(x, g_ref[...]).astype(BF16)
    merged = jnp.zeros_like(x)
    for n, (o_br, w_br) in enumerate(((of_ref, wf_ref), (os_ref, ws_ref), (oc_ref, wc_ref))):
        gate = _dot(h, wg_ref[:, n * d:(n + 1) * d]) + bg_ref[n:n + 1, :]
        gate = 1.0 / (1.0 + jnp.exp(-gate))
        merged = merged + gate * _dot(o_br[...], w_br[...])
    o_ref[...] = x + _dot(merged.astype(BF16), wo_ref[...])


def _merge(x2, g, o_f, o_s, o_c, w_gates, b_gates, w_f, w_s, w_c, w_o):
    m, d = x2.shape
    tm = min(TM_MERGE, m)
    row = lambda w: pl.BlockSpec((tm, w), lambda i: (i, 0))
    const = lambda a: pl.BlockSpec(a.shape, lambda i: (0, 0))
    return pl.pallas_call(
        _merge_body,
        grid=(m // tm,),
        in_specs=[row(d), const(g), row(W_FOX), row(W_SB), row(W_DSA), const(w_gates), const(b_gates),
                  const(w_f), const(w_s), const(w_c), const(w_o)],
        out_specs=row(d),
        out_shape=jax.ShapeDtypeStruct((m, d), F32),
        compiler_params=pltpu.CompilerParams(
            dimension_semantics=("parallel",), vmem_limit_bytes=VMEM_LIMIT),
        name="merge",
    )(x2, g, o_f, o_s, o_c, w_gates, b_gates, w_f, w_s, w_c, w_o)


def _tile_heads(g, n):
    return jnp.tile(g.astype(F32), n)


def _constants(tm_proj):
    r = np.arange(W_FOX)
    bd = (r[:, None] // HEAD_DIM == r[None, :] // HEAD_DIM).astype(np.float32)
    rt = np.arange(tm_proj)
    tri_tok = (rt[None, :] <= rt[:, None]).astype(np.float32)
    rk = np.arange(TK)
    tri_ge = (rk[None, :] >= rk[:, None]).astype(np.float32)
    tri_lt = (rk[None, :] < rk[:, None]).astype(np.float32)

    n_src = W_FOX + N_BIAS_FEATS * LANES
    pq = np.zeros((n_src, W_FOX_AUG), np.float32)
    pk = np.zeros((n_src, W_FOX_AUG), np.float32)
    ones_q = np.zeros((1, W_FOX_AUG), np.float32)
    ones_k = np.zeros((1, W_FOX_AUG), np.float32)
    for hd in range(N_HEADS_FOX):
        for dd in range(HEAD_DIM):
            pq[hd * HEAD_DIM + dd, hd * LANES + dd] = 1.0
            pk[hd * HEAD_DIM + dd, hd * LANES + dd] = 1.0
        for part in range(N_BIAS_FEATS):
            src_row = W_FOX + part * LANES + L_FORGET + hd
            pq[src_row, hd * LANES + HEAD_DIM + part] = 1.0
            ones_q[0, hd * LANES + HEAD_DIM + N_BIAS_FEATS + part] = 1.0
            ones_k[0, hd * LANES + HEAD_DIM + part] = 1.0
            pk[src_row, hd * LANES + HEAD_DIM + N_BIAS_FEATS + part] = -1.0
    half = ROT_DIM // 2
    inv_freq = jnp.power(ROPE_THETA, -jnp.arange(half, dtype=F32) * 2.0 / ROT_DIM)
    dlane = np.arange(LANES) % HEAD_DIM
    freq = jnp.where(jnp.asarray(dlane < ROT_DIM), inv_freq[jnp.asarray(dlane % half)], 0.0)
    return dict(bd=jnp.asarray(bd, BF16), tri_tok=jnp.asarray(tri_tok, BF16), tri_ge=jnp.asarray(tri_ge, BF16),
                tri_lt=jnp.asarray(tri_lt, BF16), pq=jnp.asarray(pq, BF16), pk=jnp.asarray(pk, BF16),
                ones_q=jnp.asarray(ones_q, F32), ones_k=jnp.asarray(ones_k, F32),
                freq=freq.reshape(1, LANES).astype(F32))


def kernel(x, positions, ffn1_norm, ffn1_w_gate, ffn1_w_up, ffn1_w_down, mix_norm, w_in, b_forget, b_gates, q_norm_fox, k_norm_fox, q_norm_sb, k_norm_sb, q_norm_dsa, k_norm_dsa, w_branch_fox, w_branch_sb, w_branch_dsa, w_out, ffn2_norm, ffn2_w_gate, ffn2_w_up, ffn2_w_down):
    b, s, d = x.shape
    depth = w_in.shape[0]
    topk = min(TOPK_MAX, s // 4)
    assert s % TQ_WIDE == 0 and TQ == TK and TQ_WIDE % (2 * TK) == 0 and d % LANES == 0
    consts = _constants(min(TM_PROJ, s))
    tm_proj = min(TM_PROJ, s)
    pos3 = positions.reshape(b, s // tm_proj, 1, tm_proj)
    scale = HEAD_DIM ** -0.5
    idx_scale = IDX_DIM ** -0.5

    splits = (W_FOX, W_FOX, W_FOX, N_HEADS_FOX, W_SB, W_SB, W_SB, W_DSA, W_DSA, W_DSA,
              W_IDX, IDX_DIM, N_IDX_HEADS, d, d, d)
    offs = np.concatenate([[0], np.cumsum(splits)]).tolist()
    (o_qf, o_kf, o_vf, o_ff, o_qs, o_ks, o_vs, o_qc, o_kc, o_vc, o_qi, o_ki, o_wi, o_ga) = offs[:14]

    x2 = x.reshape(b * s, d)
    for l in range(depth):
        wl = w_in[l].astype(BF16)
        pad = jnp.zeros((d, LANES - IDX_DIM - N_HEADS_FOX - N_IDX_HEADS), BF16)
        w_proj = jnp.concatenate([
            wl[:, o_qf:o_ff], wl[:, o_qs:o_qi], wl[:, o_qi:o_ki],
            wl[:, o_ki:o_wi], wl[:, o_ff:o_qs], wl[:, o_wi:o_ga], pad], axis=1)
        w_gates = wl[:, o_ga:]
        ones = lambda n: jnp.ones((n,), F32)
        gains = jnp.concatenate([
            _tile_heads(q_norm_fox[l], N_HEADS_FOX) * scale, _tile_heads(k_norm_fox[l], N_HEADS_FOX), ones(W_FOX),
            _tile_heads(q_norm_sb[l], N_HEADS_SB) * scale, _tile_heads(k_norm_sb[l], N_HEADS_SB), ones(W_SB),
            _tile_heads(q_norm_dsa[l], N_HEADS_DSA) * scale, _tile_heads(k_norm_dsa[l], N_HEADS_DSA), ones(W_DSA),
            ones(W_IDX) * idx_scale, ones(LANES)]).reshape(1, N_PROJ)
        fbias = jnp.zeros((LANES,), F32).at[L_FORGET:L_FORGET + N_HEADS_FOX].set(b_forget[l].astype(F32))
        fbias = fbias.reshape(1, LANES)

        x2 = _ffn(x2, ffn1_norm[l].reshape(1, d), ffn1_w_gate[l].astype(BF16), ffn1_w_up[l].astype(BF16),
                  ffn1_w_down[l].astype(BF16))
        zb, vt_f, vt_s, vt_c, misct = _proj(x2.reshape(b, s, d), pos3, mix_norm[l].reshape(1, d), w_proj, gains,
                                            consts, fbias)
        o_f = _fox(zb, vt_f)
        o_s = _sb(zb, vt_s, consts["tri_ge"])
        o_c = _dsa(zb, vt_c, misct, consts["tri_lt"], topk)
        x2 = _merge(x2, mix_norm[l].reshape(1, d), o_f.reshape(b * s, W_FOX), o_s.reshape(b * s, W_SB),
                    o_c.reshape(b * s, W_DSA), w_gates, b_gates[l].astype(F32),
                    w_branch_fox[l].astype(BF16), w_branch_sb[l].astype(BF16), w_branch_dsa[l].astype(BF16),
                    w_out[l].astype(BF16))
        x2 = _ffn(x2, ffn2_norm[l].reshape(1, d), ffn2_w_gate[l].astype(BF16), ffn2_w_up[l].astype(BF16),
                  ffn2_w_down[l].astype(BF16))
    return x2.reshape(b, s, d)
```

```python
import functools

import jax
import jax.numpy as jnp
import numpy as np
from jax import lax
from jax.experimental import pallas as pl
from jax.experimental.pallas import tpu as pltpu

F32 = jnp.float32
BF16 = jnp.bfloat16

HEAD_DIM = 64
N_HEADS_FOX = 6
N_HEADS_SB = 6
N_HEADS_DSA = 4
N_IDX_HEADS = 4
IDX_DIM = 64
TOPK_MAX = 256
ROPE_THETA = 500000.0
ROT_DIM = HEAD_DIM // 4
EPS = 1e-6

LANES = 128
SUBLANES = 8
MXU_DIM = 256
FFN_CHUNK = 1536
PACK16 = 2 * SUBLANES
N_COUNT_ACC = 4
W_FOX = N_HEADS_FOX * HEAD_DIM
W_SB = N_HEADS_SB * HEAD_DIM
W_DSA = N_HEADS_DSA * HEAD_DIM
W_IDX = N_IDX_HEADS * IDX_DIM

C_FQ, C_FK, C_FV = 0, W_FOX, 2 * W_FOX
C_SQ, C_SK, C_SV = 3 * W_FOX, 3 * W_FOX + W_SB, 3 * W_FOX + 2 * W_SB
C_CQ = 3 * W_FOX + 3 * W_SB
C_CK, C_CV = C_CQ + W_DSA, C_CQ + 2 * W_DSA
C_IQ = C_CQ + 3 * W_DSA
C_MISC = C_IQ + W_IDX
N_PROJ = C_MISC + LANES
L_FORGET = IDX_DIM
L_WIDX = IDX_DIM + N_HEADS_FOX
MISC_ROWS = 16

W_FOX_AUG = N_HEADS_FOX * LANES
Z_FQ, Z_FK = 0, W_FOX_AUG
Z_SQ, Z_SK = 2 * W_FOX_AUG, 2 * W_FOX_AUG + W_SB
Z_CQ = 2 * W_FOX_AUG + 2 * W_SB
Z_CK = Z_CQ + W_DSA
Z_IQ = Z_CK + W_DSA
Z_IK = Z_IQ + W_IDX
N_ZB = Z_IK + LANES
N_BIAS_FEATS = 3

TQ = 256
TK = 256
TQ_WIDE = 2 * TK
TM_PROJ = 512
TM_FFN = 512
TM_MERGE = 512
NEG_BIG = -1e30
EXP_UNDERFLOW = -110.0
VMEM_LIMIT = 56 * 1024 * 1024

_NT = (((1,), (1,)), ((), ()))


def _dot(a, b):
    return jnp.dot(a, b, preferred_element_type=F32)


def _dot_nt(a, b):
    return lax.dot_general(a, b, _NT, preferred_element_type=F32)


def _split2(x):
    hi = x.astype(BF16)
    lo = (x - hi.astype(F32)).astype(BF16)
    return hi, lo


def _split3(x):
    hi = x.astype(BF16)
    r = x - hi.astype(F32)
    mid = r.astype(BF16)
    lo = (r - mid.astype(F32)).astype(BF16)
    return hi, mid, lo


def _rms_rows(x, g):
    ms = jnp.mean(x * x, axis=-1, keepdims=True)
    return x * lax.rsqrt(ms + EPS) * g


def _softplus(z):
    return jnp.maximum(z, 0.0) + jnp.log(1.0 + jnp.exp(-jnp.abs(z)))


def _ffn_body(x_ref, g_ref, wg_ref, wu_ref, wd_ref, o_ref, *, chunks):
    x = x_ref[...]
    h = _rms_rows(x, g_ref[...]).astype(BF16)
    acc = None
    for c0, c1 in chunks:
        a = _dot(h, wg_ref[:, c0:c1])
        u = _dot(h, wu_ref[:, c0:c1])
        p = (a * (1.0 / (1.0 + jnp.exp(-a))) * u).astype(BF16)
        part = _dot(p, wd_ref[c0:c1, :])
        acc = part if acc is None else acc + part
    o_ref[...] = x + 0.5 * acc


def _ffn_chunks(d_ff):
    unit = MXU_DIM if d_ff % MXU_DIM == 0 else LANES
    n_units = d_ff // unit
    n_chunks = -(-d_ff // FFN_CHUNK)
    bounds = [unit * (n_units * c // n_chunks) for c in range(n_chunks + 1)]
    return tuple(zip(bounds[:-1], bounds[1:]))


def _resident(shape):
    return pl.BlockSpec(shape, lambda *_: (0,) * len(shape), pipeline_mode=pl.Buffered(1))


def _ffn(x2, g, wg, wu, wd):
    m, d = x2.shape
    d_ff = wg.shape[1]
    tm = min(TM_FFN, m)
    return pl.pallas_call(
        functools.partial(_ffn_body, chunks=_ffn_chunks(d_ff)),
        grid=(m // tm,),
        in_specs=[
            pl.BlockSpec((tm, d), lambda i: (i, 0)),
            _resident((1, d)), _resident((d, d_ff)), _resident((d, d_ff)), _resident((d_ff, d)),
        ],
        out_specs=pl.BlockSpec((tm, d), lambda i: (i, 0)),
        out_shape=jax.ShapeDtypeStruct((m, d), F32),
        compiler_params=pltpu.CompilerParams(
            dimension_semantics=("parallel",), vmem_limit_bytes=VMEM_LIMIT),
        name="ffn",
    )(x2, g, wg, wu, wd)


def _proj_body(x_ref, pos_ref, g_ref, w_ref, gains_ref, bd_ref, freq_ref, fbias_ref, tri_ref,
               pq_ref, pk_ref, ones_q_ref, ones_k_ref,
               zb_ref, vtf_ref, vts_ref, vtc_ref, misct_ref, carry_scr):
    t = pl.program_id(1)
    tm = x_ref.shape[1]
    h = _rms_rows(x_ref[0], g_ref[...]).astype(BF16)

    lane = lax.broadcasted_iota(jnp.int32, (1, LANES), 1)
    d_in_head = lane % HEAD_DIM
    pos_cols = jnp.broadcast_to(pos_ref[0, 0].astype(F32), (LANES, tm)).T
    first_half = d_in_head < ROT_DIM // 2
    shared = {}

    def rope(x):
        if "cos" not in shared:
            ang = pos_cols * freq_ref[...]
            shared["cos"] = jnp.cos(ang)
            shared["sin"] = jnp.where(first_half, -jnp.sin(ang), jnp.sin(ang))
        partner = jnp.where(first_half, pltpu.roll(x, LANES - ROT_DIM // 2, 1), pltpu.roll(x, ROT_DIM // 2, 1))
        return x * shared["cos"] + partner * shared["sin"]

    def head_norm(z, c0, width):
        hi, lo = _split2(z * z)
        bd = bd_ref[:width, :width]
        ss = _dot(hi, bd) + _dot(lo, bd)
        return z * lax.rsqrt(ss * (1.0 / HEAD_DIM) + EPS) * gains_ref[:, c0:c0 + width]

    def project(c0, width):
        return _dot(h, w_ref[:, c0:c0 + width])

    def finish_heads(z, c0, width, norm, rot):
        if norm:
            z = head_norm(z, c0, width)
        else:
            z = z * gains_ref[:, c0:c0 + width]
        if rot:
            z = jnp.concatenate([rope(z[:, c:c + LANES]) for c in range(0, width, LANES)], axis=1)
        return z

    def finish_index_key(zm):
        ki = rope(zm)
        zb_ref[0, :, Z_IK:Z_IK + LANES] = jnp.where(lane < IDX_DIM, ki, pltpu.roll(ki, IDX_DIM, 1)).astype(BF16)

    def finish_misc(zm):
        shared["zm"] = zm

        @pl.when(t == 0)
        def _():
            carry_scr[...] = jnp.zeros_like(carry_scr)

        logf = -_softplus(-(zm + fbias_ref[...]))
        hi, mid, lo = _split3(logf)
        tri = tri_ref[...]
        cum = _dot(tri, hi) + _dot(tri, mid) + _dot(tri, lo) + carry_scr[0:1, :]
        carry_scr[...] = jnp.broadcast_to(cum[tm - 1:tm, :], carry_scr.shape)
        is_forget = (lane >= L_FORGET) & (lane < L_FORGET + N_HEADS_FOX)
        is_widx = (lane >= L_WIDX) & (lane < L_WIDX + N_IDX_HEADS)
        misc = jnp.where(is_forget, cum, jnp.where(is_widx, zm * (N_IDX_HEADS ** -0.5), 0.0))
        misc_t = misc.T
        for c in range(tm // TQ_WIDE):
            misct_ref[0, c] = misc_t[L_FORGET:L_FORGET + MISC_ROWS, c * TQ_WIDE:(c + 1) * TQ_WIDE]
        shared["cum"] = _split3(cum)

    def finish_fox(z, c0, z0, p_ref, ones_ref):
        zn = finish_heads(z, c0, W_FOX, True, False).astype(BF16)
        src = jnp.concatenate((zn,) + shared["cum"], axis=1)
        zb_ref[0, :, z0:z0 + W_FOX_AUG] = (_dot(src, p_ref[...]) + ones_ref[...]).astype(BF16)

    def finish_plain(z, c0, z0, width, norm, rot):
        zb_ref[0, :, z0:z0 + width] = finish_heads(z, c0, width, norm, rot).astype(BF16)

    def finish_values(z, vt_ref, width):
        zt = z.T
        for p in range(width // LANES):
            for c in range(tm // TK):
                vt_ref[0, p, c] = zt[p * LANES:(p + 1) * LANES, c * TK:(c + 1) * TK].astype(BF16)

    P = functools.partial
    stages = [
        [(C_MISC, LANES, finish_misc)],
        [(C_FQ, W_FOX, P(finish_fox, c0=C_FQ, z0=Z_FQ, p_ref=pq_ref, ones_ref=ones_q_ref)),
         (C_FK, W_FOX, P(finish_fox, c0=C_FK, z0=Z_FK, p_ref=pk_ref, ones_ref=ones_k_ref))],
        [(C_FV, W_FOX, P(finish_values, vt_ref=vtf_ref, width=W_FOX)),
         (C_SQ, W_SB, P(finish_plain, c0=C_SQ, z0=Z_SQ, width=W_SB, norm=True, rot=False))],
        [(C_SK, W_SB, P(finish_plain, c0=C_SK, z0=Z_SK, width=W_SB, norm=True, rot=False)),
         (C_SV, W_SB, P(finish_values, vt_ref=vts_ref, width=W_SB))],
        [(C_CQ, W_DSA, P(finish_plain, c0=C_CQ, z0=Z_CQ, width=W_DSA, norm=True, rot=True)),
         (C_CK, W_DSA, P(finish_plain, c0=C_CK, z0=Z_CK, width=W_DSA, norm=True, rot=True))],
        [(C_CV, W_DSA, P(finish_values, vt_ref=vtc_ref, width=W_DSA)),
         (C_IQ, W_IDX, P(finish_plain, c0=C_IQ, z0=Z_IQ, width=W_IDX, norm=False, rot=True))],
    ]

    def finish_chunk(groups, z):
        c_first = groups[0][0]
        for c0, width, finish in groups:
            finish(z[:, c0 - c_first:c0 - c_first + width])

    pending = None
    for groups in stages:
        assert all(a[0] + a[1] == b[0] for a, b in zip(groups, groups[1:]))
        z = project(groups[0][0], sum(width for _, width, _ in groups))
        if pending is not None:
            finish_chunk(*pending)
        pending = (groups, z)
    finish_chunk(*pending)
    finish_index_key(shared["zm"])


def _proj(x, pos3, g, w, gains, consts, fbias):
    b, s, d = x.shape
    tm = min(TM_PROJ, s)
    const = lambda a: pl.BlockSpec(a.shape, lambda bi, ti: (0,) * a.ndim)
    vt_spec = lambda n: pl.BlockSpec((1, n, tm // TK, LANES, TK), lambda bi, ti: (bi, 0, ti, 0, 0))
    vt_shape = lambda n: jax.ShapeDtypeStruct((b, n, s // TK, LANES, TK), BF16)
    cs = (consts["bd"], consts["freq"], fbias, consts["tri_tok"], consts["pq"], consts["pk"],
          consts["ones_q"], consts["ones_k"])
    return pl.pallas_call(
        _proj_body,
        grid=(b, s // tm),
        in_specs=[
            pl.BlockSpec((1, tm, d), lambda bi, ti: (bi, ti, 0)),
            pl.BlockSpec((1, 1, 1, tm), lambda bi, ti: (bi, ti, 0, 0)),
            const(g), const(w), const(gains),
        ] + [const(a) for a in cs],
        out_specs=[
            pl.BlockSpec((1, tm, N_ZB), lambda bi, ti: (bi, ti, 0)),
            vt_spec(W_FOX // LANES), vt_spec(W_SB // LANES), vt_spec(W_DSA // LANES),
            pl.BlockSpec((1, tm // TQ_WIDE, MISC_ROWS, TQ_WIDE), lambda bi, ti: (bi, ti, 0, 0)),
        ],
        out_shape=[
            jax.ShapeDtypeStruct((b, s, N_ZB), BF16),
            vt_shape(W_FOX // LANES), vt_shape(W_SB // LANES), vt_shape(W_DSA // LANES),
            jax.ShapeDtypeStruct((b, s // TQ_WIDE, MISC_ROWS, TQ_WIDE), F32),
        ],
        scratch_shapes=[pltpu.VMEM((SUBLANES, LANES), F32)],
        compiler_params=pltpu.CompilerParams(
            dimension_semantics=("parallel", "arbitrary"), vmem_limit_bytes=VMEM_LIMIT),
        name="proj",
    )(x, pos3, g, w, gains, *cs)


def _head_masks():
    lane = lax.broadcasted_iota(jnp.int32, (1, LANES), 1)
    return lane < HEAD_DIM, lane >= HEAD_DIM


def _split_heads(q128):
    lo_half, hi_half = _head_masks()
    zero = jnp.zeros_like(q128)
    return jnp.where(lo_half, q128, zero), jnp.where(hi_half, q128, zero)


def _visible(tk, tq, strict, offset=0):
    key = lax.broadcasted_iota(jnp.int32, (tk, tq), 0) + offset
    qry = lax.broadcasted_iota(jnp.int32, (tk, tq), 1)
    return key < qry if strict else key <= qry


def _softmax_update(s, m, l):
    m_new = jnp.maximum(m, jnp.max(s, axis=0, keepdims=True))
    alpha = jnp.exp(m - m_new)
    p = jnp.exp(s - m_new)
    return p, alpha, m_new, alpha * l + jnp.sum(p, axis=0, keepdims=True)


def _head_rows(hd):
    return slice(hd * HEAD_DIM, (hd + 1) * HEAD_DIM)


def _walk_blocks(n_blocks, step, state, always_even=False):
    last = n_blocks - 1
    state = lax.fori_loop(0, last // 2, lambda n, st: step(2 * n, 2, False, st), state)
    if always_even:
        return step(last - 1, 2, True, state)
    return lax.cond(last % 2 == 1,
                    lambda st: step(last - 1, 2, True, st),
                    lambda st: step(last, 1, True, st),
                    state)


def _key_rows(ref, j0, count):
    return ref[0, pl.ds(pl.multiple_of(j0 * TK, TK), count * TK), :]


def _fox_body(q_ref, k_ref, vt_ref, o_ref, acc_scr):
    i = pl.program_id(1)
    tq = q_ref.shape[1]
    qb = tq // TK
    nh = N_HEADS_FOX
    q_heads = [q_ref[0, :, hd * LANES:(hd + 1) * LANES] for hd in range(nh)]
    acc_scr[...] = jnp.zeros_like(acc_scr)

    def step(j0, count, ends, state):
        kb = _key_rows(k_ref, j0, count)
        scores = [_dot_nt(kb[:, hd * LANES:(hd + 1) * LANES], q_heads[hd]) for hd in range(nh)]
        weights = []
        for b in range(count):
            s = [sh[b * TK:(b + 1) * TK, :] for sh in scores]
            d = b - (count - qb)
            if ends and d >= 0:
                vis = _visible(TK, tq, False, d * TK)
                s = [jnp.where(vis, sh, -jnp.inf) for sh in s]
            upd = [_softmax_update(s[hd], *state[hd]) for hd in range(nh)]
            state = tuple((u[2], u[3]) for u in upd)
            weights.append([(u[0], u[1]) for u in upd])
        accs = [acc_scr[hd // 2, _head_rows(hd % 2), :] for hd in range(nh)]
        for b in range(count):
            for hd in range(nh):
                p, alpha = weights[b][hd]
                accs[hd] = alpha * accs[hd] + _dot(vt_ref[0, hd // 2, j0 + b, _head_rows(hd % 2), :], p.astype(BF16))
        for hd in range(nh):
            acc_scr[hd // 2, _head_rows(hd % 2), :] = accs[hd]
        return state

    init = tuple((jnp.full((1, tq), -jnp.inf, F32), jnp.zeros((1, tq), F32)) for _ in range(nh))
    state = _walk_blocks(qb * (i + 1), step, init, always_even=qb % 2 == 0)
    for pr in range(nh // 2):
        inv = jnp.concatenate([jnp.broadcast_to(1.0 / state[2 * pr + hd][1], (HEAD_DIM, tq)) for hd in range(2)],
                              axis=0)
        o_ref[0, :, pr * LANES:(pr + 1) * LANES] = (acc_scr[pr] * inv).T.astype(BF16)


def _fox(zb, vt):
    b, s, _ = zb.shape
    nq, nk = s // TQ_WIDE, s // TK
    n_pairs = W_FOX // LANES
    return pl.pallas_call(
        _fox_body,
        grid=(b, nq),
        in_specs=[
            pl.BlockSpec((1, TQ_WIDE, W_FOX_AUG), lambda bi, i: (bi, i, Z_FQ // W_FOX_AUG)),
            pl.BlockSpec((1, s, W_FOX_AUG), lambda bi, i: (bi, 0, Z_FK // W_FOX_AUG)),
            pl.BlockSpec((1, n_pairs, nk, LANES, TK), lambda bi, i: (bi, 0, 0, 0, 0)),
        ],
        out_specs=pl.BlockSpec((1, TQ_WIDE, W_FOX), lambda bi, i: (bi, i, 0)),
        out_shape=jax.ShapeDtypeStruct((b, s, W_FOX), BF16),
        scratch_shapes=[pltpu.VMEM((n_pairs, LANES, TQ_WIDE), F32)],
        compiler_params=pltpu.CompilerParams(
            dimension_semantics=("parallel", "arbitrary"), vmem_limit_bytes=VMEM_LIMIT),
        name="fox",
    )(zb, zb, vt)


def _sb_body(q_ref, k_ref, vt_ref, tri_ref, o_ref, acc_scr):
    i = pl.program_id(1)
    tq = q_ref.shape[1]
    nh = N_HEADS_SB
    q_heads = []
    for pr in range(nh // 2):
        q_heads.extend(_split_heads(q_ref[0, :, pr * LANES:(pr + 1) * LANES]))
    acc_scr[...] = jnp.zeros_like(acc_scr)

    def step(j0, count, diagonal, carry):
        kb = _key_rows(k_ref, j0, count)
        tri = tri_ref[...]
        zs = [_dot_nt(kb[:, (hd // 2) * LANES:(hd // 2 + 1) * LANES], q_heads[hd]) for hd in range(nh)]
        strict = _visible(TK, tq, True)
        order = list(range(count - 1, -1, -1))
        suffixes = {}
        for b in order:
            rows = slice(b * TK, (b + 1) * TK)
            loms = [-_softplus(z[rows, :]) for z in zs]
            if diagonal and b == count - 1:
                loms = [jnp.where(strict, lom, 0.0) for lom in loms]
            splits = [_split2(lom) for lom in loms]
            suffixes[b] = [_dot(tri, hi) + _dot(tri, lo) for hi, lo in splits]
        accs = [acc_scr[hd // 2, _head_rows(hd % 2), :] for hd in range(nh)]
        carry = list(carry)
        for b in order:
            rows = slice(b * TK, (b + 1) * TK)
            for hd in range(nh):
                a = jnp.exp(zs[hd][rows, :] + suffixes[b][hd] + carry[hd])
                if diagonal and b == count - 1:
                    a = jnp.where(strict, a, 0.0)
                accs[hd] = accs[hd] + _dot(vt_ref[0, hd // 2, j0 + b, _head_rows(hd % 2), :], a.astype(BF16))
                carry[hd] = carry[hd] + suffixes[b][hd][0:1, :]
        for hd in range(nh):
            acc_scr[hd // 2, _head_rows(hd % 2), :] = accs[hd]
        return tuple(carry)

    zero = tuple(jnp.zeros((1, tq), F32) for _ in range(nh))
    carry = lax.cond(i > 0, lambda c: step(i - 1, 2, True, c), lambda c: step(i, 1, True, c), zero)

    def live(state):
        j, carry = state
        worst = functools.reduce(jnp.maximum, carry)
        return jnp.logical_and(j >= 0, jnp.max(worst) > EXP_UNDERFLOW)

    def body(state):
        j, carry = state
        return j - 1, step(j, 1, False, carry)

    lax.while_loop(live, body, (i - 2, carry))
    for pr in range(nh // 2):
        o_ref[0, :, pr * LANES:(pr + 1) * LANES] = acc_scr[pr].T.astype(BF16)


def _sb(zb, vt, tri_ge):
    b, s, _ = zb.shape
    nq, nk = s // TQ, s // TK
    n_pairs = W_SB // LANES
    return pl.pallas_call(
        _sb_body,
        grid=(b, nq),
        in_specs=[
            pl.BlockSpec((1, TQ, W_SB), lambda bi, i: (bi, i, Z_SQ // W_SB)),
            pl.BlockSpec((1, s, W_SB), lambda bi, i: (bi, 0, Z_SK // W_SB)),
            pl.BlockSpec((1, n_pairs, nk, LANES, TK), lambda bi, i: (bi, 0, 0, 0, 0)),
            pl.BlockSpec((TK, TK), lambda bi, i: (0, 0)),
        ],
        out_specs=pl.BlockSpec((1, TQ, W_SB), lambda bi, i: (bi, i, 0)),
        out_shape=jax.ShapeDtypeStruct((b, s, W_SB), BF16),
        scratch_shapes=[pltpu.VMEM((n_pairs, LANES, TQ), F32)],
        compiler_params=pltpu.CompilerParams(
            dimension_semantics=("parallel", "arbitrary"), vmem_limit_bytes=VMEM_LIMIT),
        name="sb",
    )(zb, zb, vt, tri_ge)


def _dsa_body(iq_ref, ik_ref, q_ref, k_ref, vt_ref, misct_ref, tri_ref, o_ref,
              keys_scr, hi_scr, lo_scr, bias_scr, acc_scr, *, topk):
    i = pl.program_id(1)
    tq = q_ref.shape[1]
    qb = tq // TK
    n_vis = qb * (i + 1)
    n_pairs = N_HEADS_DSA // 2
    i16_min, i16_max = -2 ** 15, 2 ** 15 - 1

    iq = iq_ref[0]
    iq_heads = _split_heads(iq[:, 0:LANES]) + _split_heads(iq[:, LANES:2 * LANES])
    w_rows = [misct_ref[0, 0, L_WIDX - L_FORGET + hh:L_WIDX - L_FORGET + hh + 1, :] for hh in range(N_IDX_HEADS)]

    def score_step(j0, count, ends, carry):
        kb = _key_rows(ik_ref, j0, count)
        dots = [_dot_nt(kb, iq_heads[hh]) for hh in range(N_IDX_HEADS)]
        for b in range(count):
            rows = slice(b * TK, (b + 1) * TK)
            sc = jnp.zeros((TK, tq), F32)
            for hh in range(N_IDX_HEADS):
                sc = sc + w_rows[hh] * jnp.maximum(dots[hh][rows, :], 0.0)
            d = b - (count - qb)
            if ends and d >= 0:
                sc = jnp.where(_visible(TK, tq, False, d * TK), sc, -jnp.inf)
            bits = lax.bitcast_convert_type(sc, jnp.int32)
            key = jnp.where(bits < 0, bits ^ jnp.int32(0x7FFFFFFF), bits)
            keys_scr[j0 + b] = key
            hi_scr[j0 + b] = lax.shift_right_arithmetic(key, 16).astype(jnp.int16)
            lo_scr[j0 + b] = ((key & 0xFFFF) - 2 ** 15).astype(jnp.int16)
        return carry

    _walk_blocks(n_vis, score_step, 0, always_even=True)

    n_block_pairs = n_vis // 2

    def count16(ref, cand):
        cand_b = jnp.broadcast_to(cand.astype(jnp.int16), (PACK16, tq))
        one, zero = jnp.ones((), jnp.int16), jnp.zeros((), jnp.int16)

        def body(n, accs):
            out = list(accs)
            for b in range(2):
                c = jnp.where(ref[2 * n + b].reshape(TK // PACK16, PACK16, tq) >= cand_b, one, zero)
                for a in range(N_COUNT_ACC):
                    g = [c[r] for r in range(a, TK // PACK16, N_COUNT_ACC)]
                    while len(g) > 1:
                        g = [g[r] + g[r + 1] for r in range(0, len(g), 2)]
                    out[a] = out[a] + g[0]
            return tuple(out)

        zeros = jnp.zeros((PACK16, tq), jnp.int16)
        accs = lax.fori_loop(0, n_block_pairs, body, (zeros,) * N_COUNT_ACC)
        acc = functools.reduce(lambda x, y: x + y, accs)
        return jnp.sum(acc.astype(jnp.int32).astype(F32), axis=0, keepdims=True)

    def count16_gt(ref, thr16):
        return jnp.where(thr16 == i16_max, 0.0, count16(ref, jnp.minimum(thr16 + 1, i16_max)))

    def bisect16(ref, want):
        def it(n, thr16):
            cand = thr16 + lax.shift_left(jnp.int32(1), 15 - n)
            return jnp.where(count16(ref, cand) >= want, cand, thr16)
        return lax.fori_loop(0, 16, it, jnp.full((1, tq), i16_min, jnp.int32))

    kf = jnp.float32(topk)
    t_hi = bisect16(hi_scr, kf)
    n_gt_hi = count16_gt(hi_scr, t_hi)
    t_hi_b = jnp.broadcast_to(t_hi.astype(jnp.int16), (PACK16, tq))

    def low_body(n, c):
        shape = (TK // PACK16, PACK16, tq)
        for j in (2 * n, 2 * n + 1):
            lo = jnp.where(hi_scr[j].reshape(shape) == t_hi_b, lo_scr[j].reshape(shape), jnp.int16(i16_min))
            lo_scr[j] = lo.reshape(TK, tq)
        return c

    lax.fori_loop(0, n_block_pairs, low_body, 0)
    t_lo = bisect16(lo_scr, kf - n_gt_hi)
    thr = t_hi * 2 ** 16 + t_lo + 2 ** 15
    n_gt_lo = count16_gt(lo_scr, t_lo)
    n_eq = count16(lo_scr, t_lo) - n_gt_lo
    need = kf - n_gt_hi - n_gt_lo

    def plain_step(j, diag):
        sel = keys_scr[j] >= thr
        if diag is not None:
            sel = sel & _visible(TK, tq, False, diag * TK)
        bias_scr[j] = jnp.where(sel, 0.0, NEG_BIG)

    def tie_step(j, run, diag):
        kblk = keys_scr[j]
        eq = kblk == thr
        eqf = jnp.where(eq, 1.0, 0.0)
        before = _dot(tri_ref[...], eqf.astype(BF16)) + run
        sel = (kblk > thr) | (eq & (before < need))
        if diag is not None:
            sel = sel & _visible(TK, tq, False, diag * TK)
        bias_scr[j] = jnp.where(sel, 0.0, NEG_BIG)
        return run + jnp.sum(eqf, axis=0, keepdims=True)

    def plain_bias():
        def body(j, c):
            plain_step(j, None)
            return c
        lax.fori_loop(0, n_vis - qb, body, 0)
        for d in range(qb):
            plain_step(n_vis - qb + d, d)

    def tie_bias():
        run = lax.fori_loop(0, n_vis - qb, lambda j, r: tie_step(j, r, None), jnp.zeros((1, tq), F32))
        for d in range(qb):
            run = tie_step(n_vis - qb + d, run, d)

    lax.cond(jnp.max(n_eq - need) > 0.0, tie_bias, plain_bias)

    q = q_ref[0]
    q_heads = _split_heads(q[:, 0:LANES]) + _split_heads(q[:, LANES:2 * LANES])
    acc_scr[...] = jnp.zeros_like(acc_scr)

    def attn_step(j0, count, ends, state):
        kb = _key_rows(k_ref, j0, count)
        scores = [_dot_nt(kb[:, (hh // 2) * LANES:(hh // 2 + 1) * LANES], q_heads[hh]) for hh in range(N_HEADS_DSA)]
        weights = []
        for b in range(count):
            bias = bias_scr[j0 + b]
            upd = [_softmax_update(scores[hh][b * TK:(b + 1) * TK, :] + bias, *state[hh])
                   for hh in range(N_HEADS_DSA)]
            state = tuple((u[2], u[3]) for u in upd)
            weights.append([(u[0], u[1]) for u in upd])
        accs = [acc_scr[hh // 2, _head_rows(hh % 2), :] for hh in range(N_HEADS_DSA)]
        for b in range(count):
            for hh in range(N_HEADS_DSA):
                p, alpha = weights[b][hh]
                accs[hh] = alpha * accs[hh] + _dot(vt_ref[0, hh // 2, j0 + b, _head_rows(hh % 2), :],
                                                   p.astype(BF16))
        for hh in range(N_HEADS_DSA):
            acc_scr[hh // 2, _head_rows(hh % 2), :] = accs[hh]
        return state

    init = tuple((jnp.full((1, tq), -jnp.inf, F32), jnp.zeros((1, tq), F32)) for _ in range(N_HEADS_DSA))
    state = _walk_blocks(n_vis, attn_step, init, always_even=True)
    outs = []
    for pr in range(n_pairs):
        inv = jnp.concatenate([jnp.broadcast_to(1.0 / state[2 * pr + hd][1], (HEAD_DIM, tq)) for hd in range(2)],
                              axis=0)
        outs.append((acc_scr[pr] * inv).T)
    o_ref[0] = jnp.concatenate(outs, axis=1).astype(BF16)


def _dsa(zb, vt, misct, tri_lt, topk):
    b, s, _ = zb.shape
    nq, nk = s // TQ_WIDE, s // TK
    n_pairs = W_DSA // LANES
    return pl.pallas_call(
        functools.partial(_dsa_body, topk=topk),
        grid=(b, nq),
        in_specs=[
            pl.BlockSpec((1, TQ_WIDE, W_IDX), lambda bi, i: (bi, i, Z_IQ // W_IDX)),
            pl.BlockSpec((1, s, LANES), lambda bi, i: (bi, 0, Z_IK // LANES)),
            pl.BlockSpec((1, TQ_WIDE, W_DSA), lambda bi, i: (bi, i, Z_CQ // W_DSA)),
            pl.BlockSpec((1, s, W_DSA), lambda bi, i: (bi, 0, Z_CK // W_DSA)),
            pl.BlockSpec((1, n_pairs, nk, LANES, TK), lambda bi, i: (bi, 0, 0, 0, 0)),
            pl.BlockSpec((1, 1, MISC_ROWS, TQ_WIDE), lambda bi, i: (bi, i, 0, 0)),
            pl.BlockSpec((TK, TK), lambda bi, i: (0, 0)),
        ],
        out_specs=pl.BlockSpec((1, TQ_WIDE, W_DSA), lambda bi, i: (bi, i, 0)),
        out_shape=jax.ShapeDtypeStruct((b, s, W_DSA), BF16),
        scratch_shapes=[
            pltpu.VMEM((nk, TK, TQ_WIDE), jnp.int32),
            pltpu.VMEM((nk, TK, TQ_WIDE), jnp.int16),
            pltpu.VMEM((nk, TK, TQ_WIDE), jnp.int16),
            pltpu.VMEM((nk, TK, TQ_WIDE), F32),
            pltpu.VMEM((n_pairs, LANES, TQ_WIDE), F32),
        ],
        compiler_params=pltpu.CompilerParams(
            dimension_semantics=("parallel", "arbitrary"), vmem_limit_bytes=VMEM_LIMIT),
        name="dsa",
    )(zb, zb, zb, zb, vt, misct, tri_lt)


def _merge_body(x_ref, g_ref, of_ref, os_ref, oc_ref, wg_ref, bg_ref, wf_ref, ws_ref, wc_ref, wo_ref, o_ref):
    x = x_ref[...]
    d = x.shape[1]
    h = _rms_rows(x, g_ref[...]).astype(BF16)
    merged = jnp.zeros_like(x)
    for n, (o_br, w_br) in enumerate(((of_ref, wf_ref), (os_ref, ws_ref), (oc_ref, wc_ref))):
        gate = _dot(h, wg_ref[:, n * d:(n + 1) * d]) + bg_ref[n:n + 1, :]
        gate = 1.0 / (1.0 + jnp.exp(-gate))
        merged = merged + gate * _dot(o_br[...], w_br[...])
    o_ref[...] = x + _dot(merged.astype(BF16), wo_ref[...])


def _merge(x2, g, o_f, o_s, o_c, w_gates, b_gates, w_f, w_s, w_c, w_o):
    m, d = x2.shape
    tm = min(TM_MERGE, m)
    row = lambda w: pl.BlockSpec((tm, w), lambda i: (i, 0))
    const = lambda a: pl.BlockSpec(a.shape, lambda i: (0, 0))
    return pl.pallas_call(
        _merge_body,
        grid=(m // tm,),
        in_specs=[row(d), const(g), row(W_FOX), row(W_SB), row(W_DSA), const(w_gates), const(b_gates),
                  const(w_f), const(w_s), const(w_c), const(w_o)],
        out_specs=row(d),
        out_shape=jax.ShapeDtypeStruct((m, d), F32),
        compiler_params=pltpu.CompilerParams(
            dimension_semantics=("parallel",), vmem_limit_bytes=VMEM_LIMIT),
        name="merge",
    )(x2, g, o_f, o_s, o_c, w_gates, b_gates, w_f, w_s, w_c, w_o)


def _tile_heads(g, n):
    return jnp.tile(g.astype(F32), n)


def _constants(tm_proj):
    r = np.arange(W_FOX)
    bd = (r[:, None] // HEAD_DIM == r[None, :] // HEAD_DIM).astype(np.float32)
    rt = np.arange(tm_proj)
    tri_tok = (rt[None, :] <= rt[:, None]).astype(np.float32)
    rk = np.arange(TK)
    tri_ge = (rk[None, :] >= rk[:, None]).astype(np.float32)
    tri_lt = (rk[None, :] < rk[:, None]).astype(np.float32)

    n_src = W_FOX + N_BIAS_FEATS * LANES
    pq = np.zeros((n_src, W_FOX_AUG), np.float32)
    pk = np.zeros((n_src, W_FOX_AUG), np.float32)
    ones_q = np.zeros((1, W_FOX_AUG), np.float32)
    ones_k = np.zeros((1, W_FOX_AUG), np.float32)
    for hd in range(N_HEADS_FOX):
        for dd in range(HEAD_DIM):
            pq[hd * HEAD_DIM + dd, hd * LANES + dd] = 1.0
            pk[hd * HEAD_DIM + dd, hd * LANES + dd] = 1.0
        for part in range(N_BIAS_FEATS):
            src_row = W_FOX + part * LANES + L_FORGET + hd
            pq[src_row, hd * LANES + HEAD_DIM + part] = 1.0
            ones_q[0, hd * LANES + HEAD_DIM + N_BIAS_FEATS + part] = 1.0
            ones_k[0, hd * LANES + HEAD_DIM + part] = 1.0
            pk[src_row, hd * LANES + HEAD_DIM + N_BIAS_FEATS + part] = -1.0
    half = ROT_DIM // 2
    inv_freq = jnp.power(ROPE_THETA, -jnp.arange(half, dtype=F32) * 2.0 / ROT_DIM)
    dlane = np.arange(LANES) % HEAD_DIM
    freq = jnp.where(jnp.asarray(dlane < ROT_DIM), inv_freq[jnp.asarray(dlane % half)], 0.0)
    return dict(bd=jnp.asarray(bd, BF16), tri_tok=jnp.asarray(tri_tok, BF16), tri_ge=jnp.asarray(tri_ge, BF16),
                tri_lt=jnp.asarray(tri_lt, BF16), pq=jnp.asarray(pq, BF16), pk=jnp.asarray(pk, BF16),
                ones_q=jnp.asarray(ones_q, F32), ones_k=jnp.asarray(ones_k, F32),
                freq=freq.reshape(1, LANES).astype(F32))


def kernel(x, positions, ffn1_norm, ffn1_w_gate, ffn1_w_up, ffn1_w_down, mix_norm, w_in, b_forget, b_gates, q_norm_fox, k_norm_fox, q_norm_sb, k_norm_sb, q_norm_dsa, k_norm_dsa, w_branch_fox, w_branch_sb, w_branch_dsa, w_out, ffn2_norm, ffn2_w_gate, ffn2_w_up, ffn2_w_down):
    b, s, d = x.shape
    depth = w_in.shape[0]
    topk = min(TOPK_MAX, s // 4)
    assert s % TQ_WIDE == 0 and TQ == TK and TQ_WIDE % (2 * TK) == 0 and d % LANES == 0
    consts = _constants(min(TM_PROJ, s))
    tm_proj = min(TM_PROJ, s)
    pos3 = positions.reshape(b, s // tm_proj, 1, tm_proj)
    scale = HEAD_DIM ** -0.5
    idx_scale = IDX_DIM ** -0.5

    splits = (W_FOX, W_FOX, W_FOX, N_HEADS_FOX, W_SB, W_SB, W_SB, W_DSA, W_DSA, W_DSA,
              W_IDX, IDX_DIM, N_IDX_HEADS, d, d, d)
    offs = np.concatenate([[0], np.cumsum(splits)]).tolist()
    (o_qf, o_kf, o_vf, o_ff, o_qs, o_ks, o_vs, o_qc, o_kc, o_vc, o_qi, o_ki, o_wi, o_ga) = offs[:14]

    x2 = x.reshape(b * s, d)
    for l in range(depth):
        wl = w_in[l].astype(BF16)
        pad = jnp.zeros((d, LANES - IDX_DIM - N_HEADS_FOX - N_IDX_HEADS), BF16)
        w_proj = jnp.concatenate([
            wl[:, o_qf:o_ff], wl[:, o_qs:o_qi], wl[:, o_qi:o_ki],
            wl[:, o_ki:o_wi], wl[:, o_ff:o_qs], wl[:, o_wi:o_ga], pad], axis=1)
        w_gates = wl[:, o_ga:]
        ones = lambda n: jnp.ones((n,), F32)
        gains = jnp.concatenate([
            _tile_heads(q_norm_fox[l], N_HEADS_FOX) * scale, _tile_heads(k_norm_fox[l], N_HEADS_FOX), ones(W_FOX),
            _tile_heads(q_norm_sb[l], N_HEADS_SB) * scale, _tile_heads(k_norm_sb[l], N_HEADS_SB), ones(W_SB),
            _tile_heads(q_norm_dsa[l], N_HEADS_DSA) * scale, _tile_heads(k_norm_dsa[l], N_HEADS_DSA), ones(W_DSA),
            ones(W_IDX) * idx_scale, ones(LANES)]).reshape(1, N_PROJ)
        fbias = jnp.zeros((LANES,), F32).at[L_FORGET:L_FORGET + N_HEADS_FOX].set(b_forget[l].astype(F32))
        fbias = fbias.reshape(1, LANES)

        x2 = _ffn(x2, ffn1_norm[l].reshape(1, d), ffn1_w_gate[l].astype(BF16), ffn1_w_up[l].astype(BF16),
                  ffn1_w_down[l].astype(BF16))
        zb, vt_f, vt_s, vt_c, misct = _proj(x2.reshape(b, s, d), pos3, mix_norm[l].reshape(1, d), w_proj, gains,
                                            consts, fbias)
        o_f = _fox(zb, vt_f)
        o_s = _sb(zb, vt_s, consts["tri_ge"])
        o_c = _dsa(zb, vt_c, misct, consts["tri_lt"], topk)
        x2 = _merge(x2, mix_norm[l].reshape(1, d), o_f.reshape(b * s, W_FOX), o_s.reshape(b * s, W_SB),
                    o_c.reshape(b * s, W_DSA), w_gates, b_gates[l].astype(F32),
                    w_branch_fox[l].astype(BF16), w_branch_sb[l].astype(BF16), w_branch_dsa[l].astype(BF16),
                    w_out[l].astype(BF16))
        x2 = _ffn(x2, ffn2_norm[l].reshape(1, d), ffn2_w_gate[l].astype(BF16), ffn2_w_up[l].astype(BF16),
                  ffn2_w_down[l].astype(BF16))
    return x2.reshape(b, s, d)
```

```python
import functools

import jax
import jax.numpy as jnp
import numpy as np
from jax import lax
from jax.experimental import pallas as pl
from jax.experimental.pallas import tpu as pltpu

F32 = jnp.float32
BF16 = jnp.bfloat16

HEAD_DIM = 64
N_HEADS_FOX = 6
N_HEADS_SB = 6
N_HEADS_DSA = 4
N_IDX_HEADS = 4
IDX_DIM = 64
TOPK_MAX = 256
ROPE_THETA = 500000.0
ROT_DIM = HEAD_DIM // 4
EPS = 1e-6

LANES = 128
SUBLANES = 8
MXU_DIM = 256
FFN_CHUNK = 1536
PACK16 = 2 * SUBLANES
N_COUNT_ACC = 4
W_FOX = N_HEADS_FOX * HEAD_DIM
W_SB = N_HEADS_SB * HEAD_DIM
W_DSA = N_HEADS_DSA * HEAD_DIM
W_IDX = N_IDX_HEADS * IDX_DIM

C_FQ, C_FK, C_FV = 0, W_FOX, 2 * W_FOX
C_SQ, C_SK, C_SV = 3 * W_FOX, 3 * W_FOX + W_SB, 3 * W_FOX + 2 * W_SB
C_CQ = 3 * W_FOX + 3 * W_SB
C_CK, C_CV = C_CQ + W_DSA, C_CQ + 2 * W_DSA
C_IQ = C_CQ + 3 * W_DSA
C_MISC = C_IQ + W_IDX
N_PROJ = C_MISC + LANES
L_FORGET = IDX_DIM
L_WIDX = IDX_DIM + N_HEADS_FOX
MISC_ROWS = 16

W_FOX_AUG = N_HEADS_FOX * LANES
Z_FQ, Z_FK = 0, W_FOX_AUG
Z_SQ, Z_SK = 2 * W_FOX_AUG, 2 * W_FOX_AUG + W_SB
Z_CQ = 2 * W_FOX_AUG + 2 * W_SB
Z_CK = Z_CQ + W_DSA
Z_IQ = Z_CK + W_DSA
Z_IK = Z_IQ + W_IDX
N_ZB = Z_IK + LANES
N_BIAS_FEATS = 3

TQ = 256
TK = 256
TQ_WIDE = 2 * TK
TM_PROJ = 512
TM_FFN = 512
TM_MERGE = 512
NEG_BIG = -1e30
EXP_UNDERFLOW = -110.0
VMEM_LIMIT = 56 * 1024 * 1024

_NT = (((1,), (1,)), ((), ()))


def _dot(a, b):
    return jnp.dot(a, b, preferred_element_type=F32)


def _dot_nt(a, b):
    return lax.dot_general(a, b, _NT, preferred_element_type=F32)


def _split2(x):
    hi = x.astype(BF16)
    lo = (x - hi.astype(F32)).astype(BF16)
    return hi, lo


def _split3(x):
    hi = x.astype(BF16)
    r = x - hi.astype(F32)
    mid = r.astype(BF16)
    lo = (r - mid.astype(F32)).astype(BF16)
    return hi, mid, lo


def _rms_rows(x, g):
    ms = jnp.mean(x * x, axis=-1, keepdims=True)
    return x * lax.rsqrt(ms + EPS) * g


def _softplus(z):
    return jnp.maximum(z, 0.0) + jnp.log(1.0 + jnp.exp(-jnp.abs(z)))


def _ffn_body(x_ref, g_ref, wg_ref, wu_ref, wd_ref, o_ref, *, chunks):
    x = x_ref[...]
    h = _rms_rows(x, g_ref[...]).astype(BF16)
    acc = None
    for c0, c1 in chunks:
        a = _dot(h, wg_ref[:, c0:c1])
        u = _dot(h, wu_ref[:, c0:c1])
        p = (a * (1.0 / (1.0 + jnp.exp(-a))) * u).astype(BF16)
        part = _dot(p, wd_ref[c0:c1, :])
        acc = part if acc is None else acc + part
    o_ref[...] = x + 0.5 * acc


def _ffn_chunks(d_ff):
    unit = MXU_DIM if d_ff % MXU_DIM == 0 else LANES
    n_units = d_ff // unit
    n_chunks = -(-d_ff // FFN_CHUNK)
    bounds = [unit * (n_units * c // n_chunks) for c in range(n_chunks + 1)]
    return tuple(zip(bounds[:-1], bounds[1:]))


def _resident(shape):
    return pl.BlockSpec(shape, lambda *_: (0,) * len(shape), pipeline_mode=pl.Buffered(1))


def _ffn(x2, g, wg, wu, wd):
    m, d = x2.shape
    d_ff = wg.shape[1]
    tm = min(TM_FFN, m)
    return pl.pallas_call(
        functools.partial(_ffn_body, chunks=_ffn_chunks(d_ff)),
        grid=(m // tm,),
        in_specs=[
            pl.BlockSpec((tm, d), lambda i: (i, 0)),
            _resident((1, d)), _resident((d, d_ff)), _resident((d, d_ff)), _resident((d_ff, d)),
        ],
        out_specs=pl.BlockSpec((tm, d), lambda i: (i, 0)),
        out_shape=jax.ShapeDtypeStruct((m, d), F32),
        compiler_params=pltpu.CompilerParams(
            dimension_semantics=("parallel",), vmem_limit_bytes=VMEM_LIMIT),
        name="ffn",
    )(x2, g, wg, wu, wd)


def _proj_body(x_ref, pos_ref, g_ref, w_ref, gains_ref, bd_ref, freq_ref, fbias_ref, tri_ref,
               pq_ref, pk_ref, ones_q_ref, ones_k_ref,
               zb_ref, vtf_ref, vts_ref, vtc_ref, misct_ref, carry_scr):
    t = pl.program_id(1)
    tm = x_ref.shape[1]
    h = _rms_rows(x_ref[0], g_ref[...]).astype(BF16)

    lane = lax.broadcasted_iota(jnp.int32, (1, LANES), 1)
    d_in_head = lane % HEAD_DIM
    pos_cols = jnp.broadcast_to(pos_ref[0, 0].astype(F32), (LANES, tm)).T
    first_half = d_in_head < ROT_DIM // 2
    shared = {}

    def rope(x):
        if "cos" not in shared:
            ang = pos_cols * freq_ref[...]
            shared["cos"] = jnp.cos(ang)
            shared["sin"] = jnp.where(first_half, -jnp.sin(ang), jnp.sin(ang))
        partner = jnp.where(first_half, pltpu.roll(x, LANES - ROT_DIM // 2, 1), pltpu.roll(x, ROT_DIM // 2, 1))
        return x * shared["cos"] + partner * shared["sin"]

    def head_norm(z, c0, width):
        hi, lo = _split2(z * z)
        bd = bd_ref[:width, :width]
        ss = _dot(hi, bd) + _dot(lo, bd)
        return z * lax.rsqrt(ss * (1.0 / HEAD_DIM) + EPS) * gains_ref[:, c0:c0 + width]

    def project(c0, width):
        return _dot(h, w_ref[:, c0:c0 + width])

    def finish_heads(z, c0, width, norm, rot):
        if norm:
            z = head_norm(z, c0, width)
        else:
            z = z * gains_ref[:, c0:c0 + width]
        if rot:
            z = jnp.concatenate([rope(z[:, c:c + LANES]) for c in range(0, width, LANES)], axis=1)
        return z

    def finish_index_key(zm):
        ki = rope(zm)
        zb_ref[0, :, Z_IK:Z_IK + LANES] = jnp.where(lane < IDX_DIM, ki, pltpu.roll(ki, IDX_DIM, 1)).astype(BF16)

    def finish_misc(zm):
        shared["zm"] = zm

        @pl.when(t == 0)
        def _():
            carry_scr[...] = jnp.zeros_like(carry_scr)

        logf = -_softplus(-(zm + fbias_ref[...]))
        hi, mid, lo = _split3(logf)
        tri = tri_ref[...]
        cum = _dot(tri, hi) + _dot(tri, mid) + _dot(tri, lo) + carry_scr[0:1, :]
        carry_scr[...] = jnp.broadcast_to(cum[tm - 1:tm, :], carry_scr.shape)
        is_forget = (lane >= L_FORGET) & (lane < L_FORGET + N_HEADS_FOX)
        is_widx = (lane >= L_WIDX) & (lane < L_WIDX + N_IDX_HEADS)
        misc = jnp.where(is_forget, cum, jnp.where(is_widx, zm * (N_IDX_HEADS ** -0.5), 0.0))
        misc_t = misc.T
        for c in range(tm // TQ_WIDE):
            misct_ref[0, c] = misc_t[L_FORGET:L_FORGET + MISC_ROWS, c * TQ_WIDE:(c + 1) * TQ_WIDE]
        shared["cum"] = _split3(cum)

    def finish_fox(z, c0, z0, p_ref, ones_ref):
        zn = finish_heads(z, c0, W_FOX, True, False).astype(BF16)
        src = jnp.concatenate((zn,) + shared["cum"], axis=1)
        zb_ref[0, :, z0:z0 + W_FOX_AUG] = (_dot(src, p_ref[...]) + ones_ref[...]).astype(BF16)

    def finish_plain(z, c0, z0, width, norm, rot):
        zb_ref[0, :, z0:z0 + width] = finish_heads(z, c0, width, norm, rot).astype(BF16)

    def finish_values(z, vt_ref, width):
        zt = z.T
        for p in range(width // LANES):
            for c in range(tm // TK):
                vt_ref[0, p, c] = zt[p * LANES:(p + 1) * LANES, c * TK:(c + 1) * TK].astype(BF16)

    P = functools.partial
    stages = [
        [(C_MISC, LANES, finish_misc)],
        [(C_FQ, W_FOX, P(finish_fox, c0=C_FQ, z0=Z_FQ, p_ref=pq_ref, ones_ref=ones_q_ref)),
         (C_FK, W_FOX, P(finish_fox, c0=C_FK, z0=Z_FK, p_ref=pk_ref, ones_ref=ones_k_ref))],
        [(C_FV, W_FOX, P(finish_values, vt_ref=vtf_ref, width=W_FOX)),
         (C_SQ, W_SB, P(finish_plain, c0=C_SQ, z0=Z_SQ, width=W_SB, norm=True, rot=False))],
        [(C_SK, W_SB, P(finish_plain, c0=C_SK, z0=Z_SK, width=W_SB, norm=True, rot=False)),
         (C_SV, W_SB, P(finish_values, vt_ref=vts_ref, width=W_SB))],
        [(C_CQ, W_DSA, P(finish_plain, c0=C_CQ, z0=Z_CQ, width=W_DSA, norm=True, rot=True)),
         (C_CK, W_DSA, P(finish_plain, c0=C_CK, z0=Z_CK, width=W_DSA, norm=True, rot=True))],
        [(C_CV, W_DSA, P(finish_values, vt_ref=vtc_ref, width=W_DSA)),
         (C_IQ, W_IDX, P(finish_plain, c0=C_IQ, z0=Z_IQ, width=W_IDX, norm=False, rot=True))],
    ]

    def finish_chunk(groups, z):
        c_first = groups[0][0]
        for c0, width, finish in groups:
            finish(z[:, c0 - c_first:c0 - c_first + width])

    pending = None
    for groups in stages:
        assert all(a[0] + a[1] == b[0] for a, b in zip(groups, groups[1:]))
        z = project(groups[0][0], sum(width for _, width, _ in groups))
        if pending is not None:
            finish_chunk(*pending)
        pending = (groups, z)
    finish_chunk(*pending)
    finish_index_key(shared["zm"])


def _proj(x, pos3, g, w, gains, consts, fbias):
    b, s, d = x.shape
    tm = min(TM_PROJ, s)
    const = lambda a: pl.BlockSpec(a.shape, lambda bi, ti: (0,) * a.ndim)
    vt_spec = lambda n: pl.BlockSpec((1, n, tm // TK, LANES, TK), lambda bi, ti: (bi, 0, ti, 0, 0))
    vt_shape = lambda n: jax.ShapeDtypeStruct((b, n, s // TK, LANES, TK), BF16)
    cs = (consts["bd"], consts["freq"], fbias, consts["tri_tok"], consts["pq"], consts["pk"],
          consts["ones_q"], consts["ones_k"])
    return pl.pallas_call(
        _proj_body,
        grid=(b, s // tm),
        in_specs=[
            pl.BlockSpec((1, tm, d), lambda bi, ti: (bi, ti, 0)),
            pl.BlockSpec((1, 1, 1, tm), lambda bi, ti: (bi, ti, 0, 0)),
            const(g), const(w), const(gains),
        ] + [const(a) for a in cs],
        out_specs=[
            pl.BlockSpec((1, tm, N_ZB), lambda bi, ti: (bi, ti, 0)),
            vt_spec(W_FOX // LANES), vt_spec(W_SB // LANES), vt_spec(W_DSA // LANES),
            pl.BlockSpec((1, tm // TQ_WIDE, MISC_ROWS, TQ_WIDE), lambda bi, ti: (bi, ti, 0, 0)),
        ],
        out_shape=[
            jax.ShapeDtypeStruct((b, s, N_ZB), BF16),
            vt_shape(W_FOX // LANES), vt_shape(W_SB // LANES), vt_shape(W_DSA // LANES),
            jax.ShapeDtypeStruct((b, s // TQ_WIDE, MISC_ROWS, TQ_WIDE), F32),
        ],
        scratch_shapes=[pltpu.VMEM((SUBLANES, LANES), F32)],
        compiler_params=pltpu.CompilerParams(
            dimension_semantics=("parallel", "arbitrary"), vmem_limit_bytes=VMEM_LIMIT),
        name="proj",
    )(x, pos3, g, w, gains, *cs)


def _head_masks():
    lane = lax.broadcasted_iota(jnp.int32, (1, LANES), 1)
    return lane < HEAD_DIM, lane >= HEAD_DIM


def _split_heads(q128):
    lo_half, hi_half = _head_masks()
    zero = jnp.zeros_like(q128)
    return jnp.where(lo_half, q128, zero), jnp.where(hi_half, q128, zero)


def _visible(tk, tq, strict, offset=0):
    key = lax.broadcasted_iota(jnp.int32, (tk, tq), 0) + offset
    qry = lax.broadcasted_iota(jnp.int32, (tk, tq), 1)
    return key < qry if strict else key <= qry


def _softmax_update(s, m, l):
    m_new = jnp.maximum(m, jnp.max(s, axis=0, keepdims=True))
    alpha = jnp.exp(m - m_new)
    p = jnp.exp(s - m_new)
    return p, alpha, m_new, alpha * l + jnp.sum(p, axis=0, keepdims=True)


def _head_rows(hd):
    return slice(hd * HEAD_DIM, (hd + 1) * HEAD_DIM)


def _walk_blocks(n_blocks, step, state, always_even=False, group=2):
    last = n_blocks - 1
    if always_even and group == 4:
        state = lax.fori_loop(0, (n_blocks - 2) // 4, lambda n, st: step(4 * n, 4, False, st), state)
        return lax.cond(n_blocks % 4 == 0,
                        lambda st: step(n_blocks - 4, 4, True, st),
                        lambda st: step(n_blocks - 2, 2, True, st),
                        state)
    state = lax.fori_loop(0, last // 2, lambda n, st: step(2 * n, 2, False, st), state)
    if always_even:
        return step(last - 1, 2, True, state)
    return lax.cond(last % 2 == 1,
                    lambda st: step(last - 1, 2, True, st),
                    lambda st: step(last, 1, True, st),
                    state)


def _key_rows(ref, j0, count):
    return ref[0, pl.ds(pl.multiple_of(j0 * TK, TK), count * TK), :]


def _fox_body(q_ref, k_ref, vt_ref, o_ref, acc_scr):
    i = pl.program_id(1)
    tq = q_ref.shape[1]
    qb = tq // TK
    nh = N_HEADS_FOX
    q_heads = [q_ref[0, :, hd * LANES:(hd + 1) * LANES] for hd in range(nh)]
    acc_scr[...] = jnp.zeros_like(acc_scr)

    def step(j0, count, ends, state):
        kb = _key_rows(k_ref, j0, count)
        scores = [_dot_nt(kb[:, hd * LANES:(hd + 1) * LANES], q_heads[hd]) for hd in range(nh)]
        weights = []
        for b in range(count):
            s = [sh[b * TK:(b + 1) * TK, :] for sh in scores]
            d = b - (count - qb)
            if ends and d >= 0:
                vis = _visible(TK, tq, False, d * TK)
                s = [jnp.where(vis, sh, -jnp.inf) for sh in s]
            upd = [_softmax_update(s[hd], *state[hd]) for hd in range(nh)]
            state = tuple((u[2], u[3]) for u in upd)
            weights.append([(u[0], u[1]) for u in upd])
        accs = [acc_scr[hd // 2, _head_rows(hd % 2), :] for hd in range(nh)]
        for b in range(count):
            for hd in range(nh):
                p, alpha = weights[b][hd]
                accs[hd] = alpha * accs[hd] + _dot(vt_ref[0, hd // 2, j0 + b, _head_rows(hd % 2), :], p.astype(BF16))
        for hd in range(nh):
            acc_scr[hd // 2, _head_rows(hd % 2), :] = accs[hd]
        return state

    init = tuple((jnp.full((1, tq), -jnp.inf, F32), jnp.zeros((1, tq), F32)) for _ in range(nh))
    state = _walk_blocks(qb * (i + 1), step, init, always_even=qb % 2 == 0, group=4)
    for pr in range(nh // 2):
        inv = jnp.concatenate([jnp.broadcast_to(1.0 / state[2 * pr + hd][1], (HEAD_DIM, tq)) for hd in range(2)],
                              axis=0)
        o_ref[0, :, pr * LANES:(pr + 1) * LANES] = (acc_scr[pr] * inv).T.astype(BF16)


def _fox(zb, vt):
    b, s, _ = zb.shape
    nq, nk = s // TQ_WIDE, s // TK
    n_pairs = W_FOX // LANES
    return pl.pallas_call(
        _fox_body,
        grid=(b, nq),
        in_specs=[
            pl.BlockSpec((1, TQ_WIDE, W_FOX_AUG), lambda bi, i: (bi, i, Z_FQ // W_FOX_AUG)),
            pl.BlockSpec((1, s, W_FOX_AUG), lambda bi, i: (bi, 0, Z_FK // W_FOX_AUG)),
            pl.BlockSpec((1, n_pairs, nk, LANES, TK), lambda bi, i: (bi, 0, 0, 0, 0)),
        ],
        out_specs=pl.BlockSpec((1, TQ_WIDE, W_FOX), lambda bi, i: (bi, i, 0)),
        out_shape=jax.ShapeDtypeStruct((b, s, W_FOX), BF16),
        scratch_shapes=[pltpu.VMEM((n_pairs, LANES, TQ_WIDE), F32)],
        compiler_params=pltpu.CompilerParams(
            dimension_semantics=("parallel", "arbitrary"), vmem_limit_bytes=VMEM_LIMIT),
        name="fox",
    )(zb, zb, vt)


def _sb_body(q_ref, k_ref, vt_ref, tri_ref, o_ref, acc_scr):
    i = pl.program_id(1)
    tq = q_ref.shape[1]
    nh = N_HEADS_SB
    q_heads = []
    for pr in range(nh // 2):
        q_heads.extend(_split_heads(q_ref[0, :, pr * LANES:(pr + 1) * LANES]))
    acc_scr[...] = jnp.zeros_like(acc_scr)

    def step(j0, count, diagonal, carry):
        kb = _key_rows(k_ref, j0, count)
        tri = tri_ref[...]
        zs = [_dot_nt(kb[:, (hd // 2) * LANES:(hd // 2 + 1) * LANES], q_heads[hd]) for hd in range(nh)]
        strict = _visible(TK, tq, True)
        order = list(range(count - 1, -1, -1))
        suffixes = {}
        for b in order:
            rows = slice(b * TK, (b + 1) * TK)
            loms = [-_softplus(z[rows, :]) for z in zs]
            if diagonal and b == count - 1:
                loms = [jnp.where(strict, lom, 0.0) for lom in loms]
            splits = [_split2(lom) for lom in loms]
            suffixes[b] = [_dot(tri, hi) + _dot(tri, lo) for hi, lo in splits]
        accs = [acc_scr[hd // 2, _head_rows(hd % 2), :] for hd in range(nh)]
        carry = list(carry)
        for b in order:
            rows = slice(b * TK, (b + 1) * TK)
            for hd in range(nh):
                a = jnp.exp(zs[hd][rows, :] + suffixes[b][hd] + carry[hd])
                if diagonal and b == count - 1:
                    a = jnp.where(strict, a, 0.0)
                accs[hd] = accs[hd] + _dot(vt_ref[0, hd // 2, j0 + b, _head_rows(hd % 2), :], a.astype(BF16))
                carry[hd] = carry[hd] + suffixes[b][hd][0:1, :]
        for hd in range(nh):
            acc_scr[hd // 2, _head_rows(hd % 2), :] = accs[hd]
        return tuple(carry)

    zero = tuple(jnp.zeros((1, tq), F32) for _ in range(nh))
    carry = lax.cond(i > 0, lambda c: step(i - 1, 2, True, c), lambda c: step(i, 1, True, c), zero)

    def live(state):
        j, carry = state
        worst = functools.reduce(jnp.maximum, carry)
        return jnp.logical_and(j >= 0, jnp.max(worst) > EXP_UNDERFLOW)

    def body(state):
        j, carry = state
        return j - 1, step(j, 1, False, carry)

    lax.while_loop(live, body, (i - 2, carry))
    for pr in range(nh // 2):
        o_ref[0, :, pr * LANES:(pr + 1) * LANES] = acc_scr[pr].T.astype(BF16)


def _sb(zb, vt, tri_ge):
    b, s, _ = zb.shape
    nq, nk = s // TQ, s // TK
    n_pairs = W_SB // LANES
    return pl.pallas_call(
        _sb_body,
        grid=(b, nq),
        in_specs=[
            pl.BlockSpec((1, TQ, W_SB), lambda bi, i: (bi, i, Z_SQ // W_SB)),
            pl.BlockSpec((1, s, W_SB), lambda bi, i: (bi, 0, Z_SK // W_SB)),
            pl.BlockSpec((1, n_pairs, nk, LANES, TK), lambda bi, i: (bi, 0, 0, 0, 0)),
            pl.BlockSpec((TK, TK), lambda bi, i: (0, 0)),
        ],
        out_specs=pl.BlockSpec((1, TQ, W_SB), lambda bi, i: (bi, i, 0)),
        out_shape=jax.ShapeDtypeStruct((b, s, W_SB), BF16),
        scratch_shapes=[pltpu.VMEM((n_pairs, LANES, TQ), F32)],
        compiler_params=pltpu.CompilerParams(
            dimension_semantics=("parallel", "arbitrary"), vmem_limit_bytes=VMEM_LIMIT),
        name="sb",
    )(zb, zb, vt, tri_ge)


def _dsa_body(iq_ref, ik_ref, q_ref, k_ref, vt_ref, misct_ref, tri_ref, o_ref,
              keys_scr, hi_scr, lo_scr, bias_scr, acc_scr, *, topk):
    i = pl.program_id(1)
    tq = q_ref.shape[1]
    qb = tq // TK
    n_vis = qb * (i + 1)
    n_pairs = N_HEADS_DSA // 2
    i16_min, i16_max = -2 ** 15, 2 ** 15 - 1

    iq = iq_ref[0]
    iq_heads = _split_heads(iq[:, 0:LANES]) + _split_heads(iq[:, LANES:2 * LANES])
    w_rows = [misct_ref[0, 0, L_WIDX - L_FORGET + hh:L_WIDX - L_FORGET + hh + 1, :] for hh in range(N_IDX_HEADS)]

    def score_step(j0, count, ends, carry):
        kb = _key_rows(ik_ref, j0, count)
        dots = [_dot_nt(kb, iq_heads[hh]) for hh in range(N_IDX_HEADS)]
        for b in range(count):
            rows = slice(b * TK, (b + 1) * TK)
            sc = jnp.zeros((TK, tq), F32)
            for hh in range(N_IDX_HEADS):
                sc = sc + w_rows[hh] * jnp.maximum(dots[hh][rows, :], 0.0)
            d = b - (count - qb)
            if ends and d >= 0:
                sc = jnp.where(_visible(TK, tq, False, d * TK), sc, -jnp.inf)
            bits = lax.bitcast_convert_type(sc, jnp.int32)
            key = jnp.where(bits < 0, bits ^ jnp.int32(0x7FFFFFFF), bits)
            keys_scr[j0 + b] = key
            hi_scr[j0 + b] = lax.shift_right_arithmetic(key, 16).astype(jnp.int16)
            lo_scr[j0 + b] = ((key & 0xFFFF) - 2 ** 15).astype(jnp.int16)
        return carry

    _walk_blocks(n_vis, score_step, 0, always_even=True, group=4)

    n_block_pairs = n_vis // 2

    def count16(ref, cand):
        cand_b = jnp.broadcast_to(cand.astype(jnp.int16), (PACK16, tq))
        one, zero = jnp.ones((), jnp.int16), jnp.zeros((), jnp.int16)

        def body(n, accs):
            out = list(accs)
            for b in range(2):
                c = jnp.where(ref[2 * n + b].reshape(TK // PACK16, PACK16, tq) >= cand_b, one, zero)
                for a in range(N_COUNT_ACC):
                    g = [c[r] for r in range(a, TK // PACK16, N_COUNT_ACC)]
                    while len(g) > 1:
                        g = [g[r] + g[r + 1] for r in range(0, len(g), 2)]
                    out[a] = out[a] + g[0]
            return tuple(out)

        zeros = jnp.zeros((PACK16, tq), jnp.int16)
        accs = lax.fori_loop(0, n_block_pairs, body, (zeros,) * N_COUNT_ACC)
        acc = functools.reduce(lambda x, y: x + y, accs)
        return jnp.sum(acc.astype(jnp.int32).astype(F32), axis=0, keepdims=True)

    def count16_gt(ref, thr16):
        return jnp.where(thr16 == i16_max, 0.0, count16(ref, jnp.minimum(thr16 + 1, i16_max)))

    def bisect16(ref, want):
        def it(n, thr16):
            cand = thr16 + lax.shift_left(jnp.int32(1), 15 - n)
            return jnp.where(count16(ref, cand) >= want, cand, thr16)
        return lax.fori_loop(0, 16, it, jnp.full((1, tq), i16_min, jnp.int32))

    kf = jnp.float32(topk)
    t_hi = bisect16(hi_scr, kf)
    n_gt_hi = count16_gt(hi_scr, t_hi)
    t_hi_b = jnp.broadcast_to(t_hi.astype(jnp.int16), (PACK16, tq))

    def low_body(n, c):
        shape = (TK // PACK16, PACK16, tq)
        for j in (2 * n, 2 * n + 1):
            lo = jnp.where(hi_scr[j].reshape(shape) == t_hi_b, lo_scr[j].reshape(shape), jnp.int16(i16_min))
            lo_scr[j] = lo.reshape(TK, tq)
        return c

    lax.fori_loop(0, n_block_pairs, low_body, 0)
    t_lo = bisect16(lo_scr, kf - n_gt_hi)
    thr = t_hi * 2 ** 16 + t_lo + 2 ** 15
    n_gt_lo = count16_gt(lo_scr, t_lo)
    n_eq = count16(lo_scr, t_lo) - n_gt_lo
    need = kf - n_gt_hi - n_gt_lo

    def plain_step(j, diag):
        sel = keys_scr[j] >= thr
        if diag is not None:
            sel = sel & _visible(TK, tq, False, diag * TK)
        bias_scr[j] = jnp.where(sel, 0.0, NEG_BIG)

    def tie_step(j, run, diag):
        kblk = keys_scr[j]
        eq = kblk == thr
        eqf = jnp.where(eq, 1.0, 0.0)
        before = _dot(tri_ref[...], eqf.astype(BF16)) + run
        sel = (kblk > thr) | (eq & (before < need))
        if diag is not None:
            sel = sel & _visible(TK, tq, False, diag * TK)
        bias_scr[j] = jnp.where(sel, 0.0, NEG_BIG)
        return run + jnp.sum(eqf, axis=0, keepdims=True)

    def plain_bias():
        def body(j, c):
            plain_step(j, None)
            return c
        lax.fori_loop(0, n_vis - qb, body, 0)
        for d in range(qb):
            plain_step(n_vis - qb + d, d)

    def tie_bias():
        run = lax.fori_loop(0, n_vis - qb, lambda j, r: tie_step(j, r, None), jnp.zeros((1, tq), F32))
        for d in range(qb):
            run = tie_step(n_vis - qb + d, run, d)

    lax.cond(jnp.max(n_eq - need) > 0.0, tie_bias, plain_bias)

    q = q_ref[0]
    q_heads = _split_heads(q[:, 0:LANES]) + _split_heads(q[:, LANES:2 * LANES])
    acc_scr[...] = jnp.zeros_like(acc_scr)

    def attn_step(j0, count, ends, state):
        kb = _key_rows(k_ref, j0, count)
        scores = [_dot_nt(kb[:, (hh // 2) * LANES:(hh // 2 + 1) * LANES], q_heads[hh]) for hh in range(N_HEADS_DSA)]
        weights = []
        for b in range(count):
            bias = bias_scr[j0 + b]
            upd = [_softmax_update(scores[hh][b * TK:(b + 1) * TK, :] + bias, *state[hh])
                   for hh in range(N_HEADS_DSA)]
            state = tuple((u[2], u[3]) for u in upd)
            weights.append([(u[0], u[1]) for u in upd])
        accs = [acc_scr[hh // 2, _head_rows(hh % 2), :] for hh in range(N_HEADS_DSA)]
        for b in range(count):
            for hh in range(N_HEADS_DSA):
                p, alpha = weights[b][hh]
                accs[hh] = alpha * accs[hh] + _dot(vt_ref[0, hh // 2, j0 + b, _head_rows(hh % 2), :],
                                                   p.astype(BF16))
        for hh in range(N_HEADS_DSA):
            acc_scr[hh // 2, _head_rows(hh % 2), :] = accs[hh]
        return state

    init = tuple((jnp.full((1, tq), -jnp.inf, F32), jnp.zeros((1, tq), F32)) for _ in range(N_HEADS_DSA))
    state = _walk_blocks(n_vis, attn_step, init, always_even=True, group=4)
    outs = []
    for pr in range(n_pairs):
        inv = jnp.concatenate([jnp.broadcast_to(1.0 / state[2 * pr + hd][1], (HEAD_DIM, tq)) for hd in range(2)],
                              axis=0)
        outs.append((acc_scr[pr] * inv).T)
    o_ref[0] = jnp.concatenate(outs, axis=1).astype(BF16)


def _dsa(zb, vt, misct, tri_lt, topk):
    b, s, _ = zb.shape
    nq, nk = s // TQ_WIDE, s // TK
    n_pairs = W_DSA // LANES
    return pl.pallas_call(
        functools.partial(_dsa_body, topk=topk),
        grid=(b, nq),
        in_specs=[
            pl.BlockSpec((1, TQ_WIDE, W_IDX), lambda bi, i: (bi, i, Z_IQ // W_IDX)),
            pl.BlockSpec((1, s, LANES), lambda bi, i: (bi, 0, Z_IK // LANES)),
            pl.BlockSpec((1, TQ_WIDE, W_DSA), lambda bi, i: (bi, i, Z_CQ // W_DSA)),
            pl.BlockSpec((1, s, W_DSA), lambda bi, i: (bi, 0, Z_CK // W_DSA)),
            pl.BlockSpec((1, n_pairs, nk, LANES, TK), lambda bi, i: (bi, 0, 0, 0, 0)),
            pl.BlockSpec((1, 1, MISC_ROWS, TQ_WIDE), lambda bi, i: (bi, i, 0, 0)),
            pl.BlockSpec((TK, TK), lambda bi, i: (0, 0)),
        ],
        out_specs=pl.BlockSpec((1, TQ_WIDE, W_DSA), lambda bi, i: (bi, i, 0)),
        out_shape=jax.ShapeDtypeStruct((b, s, W_DSA), BF16),
        scratch_shapes=[
            pltpu.VMEM((nk, TK, TQ_WIDE), jnp.int32),
            pltpu.VMEM((nk, TK, TQ_WIDE), jnp.int16),
            pltpu.VMEM((nk, TK, TQ_WIDE), jnp.int16),
            pltpu.VMEM((nk, TK, TQ_WIDE), F32),
            pltpu.VMEM((n_pairs, LANES, TQ_WIDE), F32),
        ],
        compiler_params=pltpu.CompilerParams(
            dimension_semantics=("parallel", "arbitrary"), vmem_limit_bytes=VMEM_LIMIT),
        name="dsa",
    )(zb, zb, zb, zb, vt, misct, tri_lt)


def _merge_body(x_ref, g_ref, of_ref, os_ref, oc_ref, wg_ref, bg_ref, wf_ref, ws_ref, wc_ref, wo_ref, o_ref):
    x = x_ref[...]
    d = x.shape[1]
    h = _rms_rows(x, g_ref[...]).astype(BF16)
    merged = jnp.zeros_like(x)
    for n, (o_br, w_br) in enumerate(((of_ref, wf_ref), (os_ref, ws_ref), (oc_ref, wc_ref))):
        gate = _dot(h, wg_ref[:, n * d:(n + 1) * d]) + bg_ref[n:n + 1, :]
        gate = 1.0 / (1.0 + jnp.exp(-gate))
        merged = merged + gate * _dot(o_br[...], w_br[...])
    o_ref[...] = x + _dot(merged.astype(BF16), wo_ref[...])


def _merge(x2, g, o_f, o_s, o_c, w_gates, b_gates, w_f, w_s, w_c, w_o):
    m, d = x2.shape
    tm = min(TM_MERGE, m)
    row = lambda w: pl.BlockSpec((tm, w), lambda i: (i, 0))
    const = lambda a: pl.BlockSpec(a.shape, lambda i: (0, 0))
    return pl.pallas_call(
        _merge_body,
        grid=(m // tm,),
        in_specs=[row(d), const(g), row(W_FOX), row(W_SB), row(W_DSA), const(w_gates), const(b_gates),
                  const(w_f), const(w_s), const(w_c), const(w_o)],
        out_specs=row(d),
        out_shape=jax.ShapeDtypeStruct((m, d), F32),
        compiler_params=pltpu.CompilerParams(
            dimension_semantics=("parallel",), vmem_limit_bytes=VMEM_LIMIT),
        name="merge",
    )(x2, g, o_f, o_s, o_c, w_gates, b_gates, w_f, w_s, w_c, w_o)


def _tile_heads(g, n):
    return jnp.tile(g.astype(F32), n)


def _constants(tm_proj):
    r = np.arange(W_FOX)
    bd = (r[:, None] // HEAD_DIM == r[None, :] // HEAD_DIM).astype(np.float32)
    rt = np.arange(tm_proj)
    tri_tok = (rt[None, :] <= rt[:, None]).astype(np.float32)
    rk = np.arange(TK)
    tri_ge = (rk[None, :] >= rk[:, None]).astype(np.float32)
    tri_lt = (rk[None, :] < rk[:, None]).astype(np.float32)

    n_src = W_FOX + N_BIAS_FEATS * LANES
    pq = np.zeros((n_src, W_FOX_AUG), np.float32)
    pk = np.zeros((n_src, W_FOX_AUG), np.float32)
    ones_q = np.zeros((1, W_FOX_AUG), np.float32)
    ones_k = np.zeros((1, W_FOX_AUG), np.float32)
    for hd in range(N_HEADS_FOX):
        for dd in range(HEAD_DIM):
            pq[hd * HEAD_DIM + dd, hd * LANES + dd] = 1.0
            pk[hd * HEAD_DIM + dd, hd * LANES + dd] = 1.0
        for part in range(N_BIAS_FEATS):
            src_row = W_FOX + part * LANES + L_FORGET + hd
            pq[src_row, hd * LANES + HEAD_DIM + part] = 1.0
            ones_q[0, hd * LANES + HEAD_DIM + N_BIAS_FEATS + part] = 1.0
            ones_k[0, hd * LANES + HEAD_DIM + part] = 1.0
            pk[src_row, hd * LANES + HEAD_DIM + N_BIAS_FEATS + part] = -1.0
    half = ROT_DIM // 2
    inv_freq = jnp.power(ROPE_THETA, -jnp.arange(half, dtype=F32) * 2.0 / ROT_DIM)
    dlane = np.arange(LANES) % HEAD_DIM
    freq = jnp.where(jnp.asarray(dlane < ROT_DIM), inv_freq[jnp.asarray(dlane % half)], 0.0)
    return dict(bd=jnp.asarray(bd, BF16), tri_tok=jnp.asarray(tri_tok, BF16), tri_ge=jnp.asarray(tri_ge, BF16),
                tri_lt=jnp.asarray(tri_lt, BF16), pq=jnp.asarray(pq, BF16), pk=jnp.asarray(pk, BF16),
                ones_q=jnp.asarray(ones_q, F32), ones_k=jnp.asarray(ones_k, F32),
                freq=freq.reshape(1, LANES).astype(F32))


def kernel(x, positions, ffn1_norm, ffn1_w_gate, ffn1_w_up, ffn1_w_down, mix_norm, w_in, b_forget, b_gates, q_norm_fox, k_norm_fox, q_norm_sb, k_norm_sb, q_norm_dsa, k_norm_dsa, w_branch_fox, w_branch_sb, w_branch_dsa, w_out, ffn2_norm, ffn2_w_gate, ffn2_w_up, ffn2_w_down):
    b, s, d = x.shape
    depth = w_in.shape[0]
    topk = min(TOPK_MAX, s // 4)
    assert s % TQ_WIDE == 0 and TQ == TK and TQ_WIDE % (2 * TK) == 0 and d % LANES == 0
    consts = _constants(min(TM_PROJ, s))
    tm_proj = min(TM_PROJ, s)
    pos3 = positions.reshape(b, s // tm_proj, 1, tm_proj)
    scale = HEAD_DIM ** -0.5
    idx_scale = IDX_DIM ** -0.5

    splits = (W_FOX, W_FOX, W_FOX, N_HEADS_FOX, W_SB, W_SB, W_SB, W_DSA, W_DSA, W_DSA,
              W_IDX, IDX_DIM, N_IDX_HEADS, d, d, d)
    offs = np.concatenate([[0], np.cumsum(splits)]).tolist()
    (o_qf, o_kf, o_vf, o_ff, o_qs, o_ks, o_vs, o_qc, o_kc, o_vc, o_qi, o_ki, o_wi, o_ga) = offs[:14]

    x2 = x.reshape(b * s, d)
    for l in range(depth):
        wl = w_in[l].astype(BF16)
        pad = jnp.zeros((d, LANES - IDX_DIM - N_HEADS_FOX - N_IDX_HEADS), BF16)
        w_proj = jnp.concatenate([
            wl[:, o_qf:o_ff], wl[:, o_qs:o_qi], wl[:, o_qi:o_ki],
            wl[:, o_ki:o_wi], wl[:, o_ff:o_qs], wl[:, o_wi:o_ga], pad], axis=1)
        w_gates = wl[:, o_ga:]
        ones = lambda n: jnp.ones((n,), F32)
        gains = jnp.concatenate([
            _tile_heads(q_norm_fox[l], N_HEADS_FOX) * scale, _tile_heads(k_norm_fox[l], N_HEADS_FOX), ones(W_FOX),
            _tile_heads(q_norm_sb[l], N_HEADS_SB) * scale, _tile_heads(k_norm_sb[l], N_HEADS_SB), ones(W_SB),
            _tile_heads(q_norm_dsa[l], N_HEADS_DSA) * scale, _tile_heads(k_norm_dsa[l], N_HEADS_DSA), ones(W_DSA),
            ones(W_IDX) * idx_scale, ones(LANES)]).reshape(1, N_PROJ)
        fbias = jnp.zeros((LANES,), F32).at[L_FORGET:L_FORGET + N_HEADS_FOX].set(b_forget[l].astype(F32))
        fbias = fbias.reshape(1, LANES)

        x2 = _ffn(x2, ffn1_norm[l].reshape(1, d), ffn1_w_gate[l].astype(BF16), ffn1_w_up[l].astype(BF16),
                  ffn1_w_down[l].astype(BF16))
        zb, vt_f, vt_s, vt_c, misct = _proj(x2.reshape(b, s, d), pos3, mix_norm[l].reshape(1, d), w_proj, gains,
                                            consts, fbias)
        o_f = _fox(zb, vt_f)
        o_s = _sb(zb, vt_s, consts["tri_ge"])
        o_c = _dsa(zb, vt_c, misct, consts["tri_lt"], topk)
        x2 = _merge(x2, mix_norm[l].reshape(1, d), o_f.reshape(b * s, W_FOX), o_s.reshape(b * s, W_SB),
                    o_c.reshape(b * s, W_DSA), w_gates, b_gates[l].astype(F32),
                    w_branch_fox[l].astype(BF16), w_branch_sb[l].astype(BF16), w_branch_dsa[l].astype(BF16),
                    w_out[l].astype(BF16))
        x2 = _ffn(x2, ffn2_norm[l].reshape(1, d), ffn2_w_gate[l].astype(BF16), ffn2_w_up[l].astype(BF16),
                  ffn2_w_down[l].astype(BF16))
    return x2.reshape(b, s, d)
```

```python
import functools

import jax
import jax.numpy as jnp
import numpy as np
from jax import lax
from jax.experimental import pallas as pl
from jax.experimental.pallas import tpu as pltpu

F32 = jnp.float32
BF16 = jnp.bfloat16

HEAD_DIM = 64
N_HEADS_FOX = 6
N_HEADS_SB = 6
N_HEADS_DSA = 4
N_IDX_HEADS = 4
IDX_DIM = 64
TOPK_MAX = 256
ROPE_THETA = 500000.0
ROT_DIM = HEAD_DIM // 4
EPS = 1e-6

LANES = 128
SUBLANES = 8
MXU_DIM = 256
FFN_CHUNK = 1536
PACK16 = 2 * SUBLANES
N_COUNT_ACC = 2
W_FOX = N_HEADS_FOX * HEAD_DIM
W_SB = N_HEADS_SB * HEAD_DIM
W_DSA = N_HEADS_DSA * HEAD_DIM
W_IDX = N_IDX_HEADS * IDX_DIM

C_FQ, C_FK, C_FV = 0, W_FOX, 2 * W_FOX
C_SQ, C_SK, C_SV = 3 * W_FOX, 3 * W_FOX + W_SB, 3 * W_FOX + 2 * W_SB
C_CQ = 3 * W_FOX + 3 * W_SB
C_CK, C_CV = C_CQ + W_DSA, C_CQ + 2 * W_DSA
C_IQ = C_CQ + 3 * W_DSA
C_MISC = C_IQ + W_IDX
N_PROJ = C_MISC + LANES
L_FORGET = IDX_DIM
L_WIDX = IDX_DIM + N_HEADS_FOX
MISC_ROWS = 16

W_FOX_AUG = N_HEADS_FOX * LANES
Z_FQ, Z_FK = 0, W_FOX_AUG
Z_SQ, Z_SK = 2 * W_FOX_AUG, 2 * W_FOX_AUG + W_SB
Z_CQ = 2 * W_FOX_AUG + 2 * W_SB
Z_CK = Z_CQ + W_DSA
Z_IQ = Z_CK + W_DSA
Z_IK = Z_IQ + W_IDX
N_ZB = Z_IK + LANES
N_BIAS_FEATS = 3

TQ = 256
TK = 256
TQ_WIDE = 2 * TK
TM_PROJ = 512
TM_FFN = 512
TM_MERGE = 512
NEG_BIG = -1e30
EXP_UNDERFLOW = -110.0
VMEM_LIMIT = 56 * 1024 * 1024

_NT = (((1,), (1,)), ((), ()))


def _dot(a, b):
    return jnp.dot(a, b, preferred_element_type=F32)


def _dot_nt(a, b):
    return lax.dot_general(a, b, _NT, preferred_element_type=F32)


def _split2(x):
    hi = x.astype(BF16)
    lo = (x - hi.astype(F32)).astype(BF16)
    return hi, lo


def _split3(x):
    hi = x.astype(BF16)
    r = x - hi.astype(F32)
    mid = r.astype(BF16)
    lo = (r - mid.astype(F32)).astype(BF16)
    return hi, mid, lo


def _rms_rows(x, g):
    ms = jnp.mean(x * x, axis=-1, keepdims=True)
    return x * lax.rsqrt(ms + EPS) * g


def _softplus(z):
    return jnp.maximum(z, 0.0) + jnp.log(1.0 + jnp.exp(-jnp.abs(z)))


def _ffn_body(x_ref, g_ref, wg_ref, wu_ref, wd_ref, o_ref, *, chunks):
    x = x_ref[...]
    h = _rms_rows(x, g_ref[...]).astype(BF16)
    acc = None
    for c0, c1 in chunks:
        a = _dot(h, wg_ref[:, c0:c1])
        u = _dot(h, wu_ref[:, c0:c1])
        p = (a * (1.0 / (1.0 + jnp.exp(-a))) * u).astype(BF16)
        part = _dot(p, wd_ref[c0:c1, :])
        acc = part if acc is None else acc + part
    o_ref[...] = x + 0.5 * acc


def _ffn_chunks(d_ff):
    unit = MXU_DIM if d_ff % MXU_DIM == 0 else LANES
    n_units = d_ff // unit
    n_chunks = -(-d_ff // FFN_CHUNK)
    bounds = [unit * (n_units * c // n_chunks) for c in range(n_chunks + 1)]
    return tuple(zip(bounds[:-1], bounds[1:]))


def _resident(shape):
    return pl.BlockSpec(shape, lambda *_: (0,) * len(shape), pipeline_mode=pl.Buffered(1))


def _ffn(x2, g, wg, wu, wd):
    m, d = x2.shape
    d_ff = wg.shape[1]
    tm = min(TM_FFN, m)
    return pl.pallas_call(
        functools.partial(_ffn_body, chunks=_ffn_chunks(d_ff)),
        grid=(m // tm,),
        in_specs=[
            pl.BlockSpec((tm, d), lambda i: (i, 0)),
            _resident((1, d)), _resident((d, d_ff)), _resident((d, d_ff)), _resident((d_ff, d)),
        ],
        out_specs=pl.BlockSpec((tm, d), lambda i: (i, 0)),
        out_shape=jax.ShapeDtypeStruct((m, d), F32),
        compiler_params=pltpu.CompilerParams(
            dimension_semantics=("parallel",), vmem_limit_bytes=VMEM_LIMIT),
        name="ffn",
    )(x2, g, wg, wu, wd)


def _proj_body(x_ref, cos_ref, sin_ref, g_ref, w_ref, gains_ref, bd_ref, fbias_ref, tri_ref,
               pq_ref, pk_ref, ones_q_ref, ones_k_ref,
               zb_ref, vtf_ref, vts_ref, vtc_ref, misct_ref, carry_scr):
    t = pl.program_id(1)
    tm = x_ref.shape[1]
    h = _rms_rows(x_ref[0], g_ref[...]).astype(BF16)

    lane = lax.broadcasted_iota(jnp.int32, (1, LANES), 1)
    first_half = lane % HEAD_DIM < ROT_DIM // 2
    shared = {}

    def rope(x):
        partner = jnp.where(first_half, pltpu.roll(x, LANES - ROT_DIM // 2, 1), pltpu.roll(x, ROT_DIM // 2, 1))
        return x * cos_ref[0] + partner * sin_ref[0]

    def head_norm(z, c0, width):
        hi, lo = _split2(z * z)
        bd = bd_ref[:width, :width]
        ss = _dot(hi, bd) + _dot(lo, bd)
        return z * lax.rsqrt(ss * (1.0 / HEAD_DIM) + EPS) * gains_ref[:, c0:c0 + width]

    def project(c0, width):
        return _dot(h, w_ref[:, c0:c0 + width])

    def finish_heads(z, c0, width, norm, rot):
        if norm:
            z = head_norm(z, c0, width)
        else:
            z = z * gains_ref[:, c0:c0 + width]
        if rot:
            z = jnp.concatenate([rope(z[:, c:c + LANES]) for c in range(0, width, LANES)], axis=1)
        return z

    def finish_index_key(zm):
        ki = rope(zm)
        zb_ref[0, :, Z_IK:Z_IK + LANES] = jnp.where(lane < IDX_DIM, ki, pltpu.roll(ki, IDX_DIM, 1)).astype(BF16)

    def finish_misc(zm):
        shared["zm"] = zm

        @pl.when(t == 0)
        def _():
            carry_scr[...] = jnp.zeros_like(carry_scr)

        logf = -_softplus(-(zm + fbias_ref[...]))
        hi, mid, lo = _split3(logf)
        tri = tri_ref[...]
        cum = _dot(tri, hi) + _dot(tri, mid) + _dot(tri, lo) + carry_scr[0:1, :]
        carry_scr[...] = jnp.broadcast_to(cum[tm - 1:tm, :], carry_scr.shape)
        is_forget = (lane >= L_FORGET) & (lane < L_FORGET + N_HEADS_FOX)
        is_widx = (lane >= L_WIDX) & (lane < L_WIDX + N_IDX_HEADS)
        misc = jnp.where(is_forget, cum, jnp.where(is_widx, zm * (N_IDX_HEADS ** -0.5), 0.0))
        misc_t = misc.T
        for c in range(tm // TQ_WIDE):
            misct_ref[0, c] = misc_t[L_FORGET:L_FORGET + MISC_ROWS, c * TQ_WIDE:(c + 1) * TQ_WIDE]
        shared["cum"] = _split3(cum)

    def finish_fox(z, c0, z0, p_ref, ones_ref):
        zn = finish_heads(z, c0, W_FOX, True, False).astype(BF16)
        src = jnp.concatenate((zn,) + shared["cum"], axis=1)
        zb_ref[0, :, z0:z0 + W_FOX_AUG] = (_dot(src, p_ref[...]) + ones_ref[...]).astype(BF16)

    def finish_plain(z, c0, z0, width, norm, rot):
        zb_ref[0, :, z0:z0 + width] = finish_heads(z, c0, width, norm, rot).astype(BF16)

    def finish_values(z, vt_ref, width):
        zt = z.T
        for p in range(width // LANES):
            for c in range(tm // TK):
                vt_ref[0, p, c] = zt[p * LANES:(p + 1) * LANES, c * TK:(c + 1) * TK].astype(BF16)

    P = functools.partial
    stages = [
        [(C_MISC, LANES, finish_misc)],
        [(C_FQ, W_FOX, P(finish_fox, c0=C_FQ, z0=Z_FQ, p_ref=pq_ref, ones_ref=ones_q_ref)),
         (C_FK, W_FOX, P(finish_fox, c0=C_FK, z0=Z_FK, p_ref=pk_ref, ones_ref=ones_k_ref))],
        [(C_FV, W_FOX, P(finish_values, vt_ref=vtf_ref, width=W_FOX)),
         (C_SQ, W_SB, P(finish_plain, c0=C_SQ, z0=Z_SQ, width=W_SB, norm=True, rot=False))],
        [(C_SK, W_SB, P(finish_plain, c0=C_SK, z0=Z_SK, width=W_SB, norm=True, rot=False)),
         (C_SV, W_SB, P(finish_values, vt_ref=vts_ref, width=W_SB))],
        [(C_CQ, W_DSA, P(finish_plain, c0=C_CQ, z0=Z_CQ, width=W_DSA, norm=True, rot=True)),
         (C_CK, W_DSA, P(finish_plain, c0=C_CK, z0=Z_CK, width=W_DSA, norm=True, rot=True))],
        [(C_CV, W_DSA, P(finish_values, vt_ref=vtc_ref, width=W_DSA)),
         (C_IQ, W_IDX, P(finish_plain, c0=C_IQ, z0=Z_IQ, width=W_IDX, norm=False, rot=True))],
    ]

    def finish_chunk(groups, z):
        c_first = groups[0][0]
        for c0, width, finish in groups:
            finish(z[:, c0 - c_first:c0 - c_first + width])

    pending = None
    for groups in stages:
        assert all(a[0] + a[1] == b[0] for a, b in zip(groups, groups[1:]))
        z = project(groups[0][0], sum(width for _, width, _ in groups))
        if pending is not None:
            finish_chunk(*pending)
        pending = (groups, z)
    finish_chunk(*pending)
    finish_index_key(shared["zm"])


def _rope_body(pos_ref, freq_ref, cos_ref, sin_ref):
    tm = pos_ref.shape[3]
    lane = lax.broadcasted_iota(jnp.int32, (1, LANES), 1)
    first_half = lane % HEAD_DIM < ROT_DIM // 2
    pos_cols = jnp.broadcast_to(pos_ref[0, 0].astype(F32), (LANES, tm)).T
    ang = pos_cols * freq_ref[...]
    cos_ref[0] = jnp.cos(ang)
    sin_ref[0] = jnp.where(first_half, -jnp.sin(ang), jnp.sin(ang))


def _rope_tables(pos3, freq):
    b, nt, _, tm = pos3.shape
    table = pl.BlockSpec((1, tm, LANES), lambda bi, ti: (bi, ti, 0))
    return pl.pallas_call(
        _rope_body,
        grid=(b, nt),
        in_specs=[pl.BlockSpec((1, 1, 1, tm), lambda bi, ti: (bi, ti, 0, 0)),
                  pl.BlockSpec((1, LANES), lambda bi, ti: (0, 0))],
        out_specs=[table, table],
        out_shape=[jax.ShapeDtypeStruct((b, nt * tm, LANES), F32)] * 2,
        compiler_params=pltpu.CompilerParams(dimension_semantics=("parallel", "parallel")),
        name="rope",
    )(pos3, freq)


def _proj(x, rope_cos, rope_sin, g, w, gains, consts, fbias):
    b, s, d = x.shape
    tm = min(TM_PROJ, s)
    const = lambda a: pl.BlockSpec(a.shape, lambda bi, ti: (0,) * a.ndim)
    vt_spec = lambda n: pl.BlockSpec((1, n, tm // TK, LANES, TK), lambda bi, ti: (bi, 0, ti, 0, 0))
    vt_shape = lambda n: jax.ShapeDtypeStruct((b, n, s // TK, LANES, TK), BF16)
    cs = (consts["bd"], fbias, consts["tri_tok"], consts["pq"], consts["pk"], consts["ones_q"], consts["ones_k"])
    table = pl.BlockSpec((1, tm, LANES), lambda bi, ti: (bi, ti, 0))
    return pl.pallas_call(
        _proj_body,
        grid=(b, s // tm),
        in_specs=[
            pl.BlockSpec((1, tm, d), lambda bi, ti: (bi, ti, 0)),
            table, table,
            const(g), const(w), const(gains),
        ] + [const(a) for a in cs],
        out_specs=[
            pl.BlockSpec((1, tm, N_ZB), lambda bi, ti: (bi, ti, 0)),
            vt_spec(W_FOX // LANES), vt_spec(W_SB // LANES), vt_spec(W_DSA // LANES),
            pl.BlockSpec((1, tm // TQ_WIDE, MISC_ROWS, TQ_WIDE), lambda bi, ti: (bi, ti, 0, 0)),
        ],
        out_shape=[
            jax.ShapeDtypeStruct((b, s, N_ZB), BF16),
            vt_shape(W_FOX // LANES), vt_shape(W_SB // LANES), vt_shape(W_DSA // LANES),
            jax.ShapeDtypeStruct((b, s // TQ_WIDE, MISC_ROWS, TQ_WIDE), F32),
        ],
        scratch_shapes=[pltpu.VMEM((SUBLANES, LANES), F32)],
        compiler_params=pltpu.CompilerParams(
            dimension_semantics=("parallel", "arbitrary"), vmem_limit_bytes=VMEM_LIMIT),
        name="proj",
    )(x, rope_cos, rope_sin, g, w, gains, *cs)


def _head_masks():
    lane = lax.broadcasted_iota(jnp.int32, (1, LANES), 1)
    return lane < HEAD_DIM, lane >= HEAD_DIM


def _split_heads(q128):
    lo_half, hi_half = _head_masks()
    zero = jnp.zeros_like(q128)
    return jnp.where(lo_half, q128, zero), jnp.where(hi_half, q128, zero)


def _visible(tk, tq, strict, offset=0):
    key = lax.broadcasted_iota(jnp.int32, (tk, tq), 0) + offset
    qry = lax.broadcasted_iota(jnp.int32, (tk, tq), 1)
    return key < qry if strict else key <= qry


def _softmax_update(s, m, l):
    m_new = jnp.maximum(m, jnp.max(s, axis=0, keepdims=True))
    alpha = jnp.exp(m - m_new)
    p = jnp.exp(s - m_new)
    return p, alpha, m_new, alpha * l + jnp.sum(p, axis=0, keepdims=True)


def _head_rows(hd):
    return slice(hd * HEAD_DIM, (hd + 1) * HEAD_DIM)


def _walk_blocks(n_blocks, step, state, always_even=False, group=2):
    last = n_blocks - 1
    if always_even and group == 4:
        state = lax.fori_loop(0, (n_blocks - 2) // 4, lambda n, st: step(4 * n, 4, False, st), state)
        return lax.cond(n_blocks % 4 == 0,
                        lambda st: step(n_blocks - 4, 4, True, st),
                        lambda st: step(n_blocks - 2, 2, True, st),
                        state)
    state = lax.fori_loop(0, last // 2, lambda n, st: step(2 * n, 2, False, st), state)
    if always_even:
        return step(last - 1, 2, True, state)
    return lax.cond(last % 2 == 1,
                    lambda st: step(last - 1, 2, True, st),
                    lambda st: step(last, 1, True, st),
                    state)


def _key_rows(ref, j0, count):
    return ref[0, pl.ds(pl.multiple_of(j0 * TK, TK), count * TK), :]


def _fox_body(q_ref, k_ref, vt_ref, o_ref, acc_scr):
    i = pl.program_id(1)
    tq = q_ref.shape[1]
    qb = tq // TK
    nh = N_HEADS_FOX
    q_heads = [q_ref[0, :, hd * LANES:(hd + 1) * LANES] for hd in range(nh)]
    acc_scr[...] = jnp.zeros_like(acc_scr)

    def step(j0, count, ends, state):
        kb = _key_rows(k_ref, j0, count)
        scores = [_dot_nt(kb[:, hd * LANES:(hd + 1) * LANES], q_heads[hd]) for hd in range(nh)]
        weights = []
        for b in range(count):
            s = [sh[b * TK:(b + 1) * TK, :] for sh in scores]
            d = b - (count - qb)
            if ends and d >= 0:
                vis = _visible(TK, tq, False, d * TK)
                s = [jnp.where(vis, sh, -jnp.inf) for sh in s]
            upd = [_softmax_update(s[hd], *state[hd]) for hd in range(nh)]
            state = tuple((u[2], u[3]) for u in upd)
            weights.append([(u[0], u[1]) for u in upd])
        accs = [acc_scr[hd // 2, _head_rows(hd % 2), :] for hd in range(nh)]
        for b in range(count):
            for hd in range(nh):
                p, alpha = weights[b][hd]
                accs[hd] = alpha * accs[hd] + _dot(vt_ref[0, hd // 2, j0 + b, _head_rows(hd % 2), :], p.astype(BF16))
        for hd in range(nh):
            acc_scr[hd // 2, _head_rows(hd % 2), :] = accs[hd]
        return state

    init = tuple((jnp.full((1, tq), -jnp.inf, F32), jnp.zeros((1, tq), F32)) for _ in range(nh))
    state = _walk_blocks(qb * (i + 1), step, init, always_even=qb % 2 == 0, group=4)
    for pr in range(nh // 2):
        inv = jnp.concatenate([jnp.broadcast_to(1.0 / state[2 * pr + hd][1], (HEAD_DIM, tq)) for hd in range(2)],
                              axis=0)
        o_ref[0, :, pr * LANES:(pr + 1) * LANES] = (acc_scr[pr] * inv).T.astype(BF16)


def _fox(zb, vt):
    b, s, _ = zb.shape
    nq, nk = s // TQ_WIDE, s // TK
    n_pairs = W_FOX // LANES
    return pl.pallas_call(
        _fox_body,
        grid=(b, nq),
        in_specs=[
            pl.BlockSpec((1, TQ_WIDE, W_FOX_AUG), lambda bi, i: (bi, i, Z_FQ // W_FOX_AUG)),
            pl.BlockSpec((1, s, W_FOX_AUG), lambda bi, i: (bi, 0, Z_FK // W_FOX_AUG)),
            pl.BlockSpec((1, n_pairs, nk, LANES, TK), lambda bi, i: (bi, 0, 0, 0, 0)),
        ],
        out_specs=pl.BlockSpec((1, TQ_WIDE, W_FOX), lambda bi, i: (bi, i, 0)),
        out_shape=jax.ShapeDtypeStruct((b, s, W_FOX), BF16),
        scratch_shapes=[pltpu.VMEM((n_pairs, LANES, TQ_WIDE), F32)],
        compiler_params=pltpu.CompilerParams(
            dimension_semantics=("parallel", "arbitrary"), vmem_limit_bytes=VMEM_LIMIT),
        name="fox",
    )(zb, zb, vt)


def _sb_body(q_ref, k_ref, vt_ref, tri_ref, o_ref, acc_scr):
    i = pl.program_id(1)
    tq = q_ref.shape[1]
    nh = N_HEADS_SB
    q_heads = []
    for pr in range(nh // 2):
        q_heads.extend(_split_heads(q_ref[0, :, pr * LANES:(pr + 1) * LANES]))
    acc_scr[...] = jnp.zeros_like(acc_scr)

    def step(j0, count, diagonal, carry):
        kb = _key_rows(k_ref, j0, count)
        tri = tri_ref[...]
        zs = [_dot_nt(kb[:, (hd // 2) * LANES:(hd // 2 + 1) * LANES], q_heads[hd]) for hd in range(nh)]
        strict = _visible(TK, tq, True)
        order = list(range(count - 1, -1, -1))
        suffixes = {}
        for b in order:
            rows = slice(b * TK, (b + 1) * TK)
            loms = [-_softplus(z[rows, :]) for z in zs]
            if diagonal and b == count - 1:
                loms = [jnp.where(strict, lom, 0.0) for lom in loms]
            splits = [_split2(lom) for lom in loms]
            suffixes[b] = [_dot(tri, hi) + _dot(tri, lo) for hi, lo in splits]
        accs = [acc_scr[hd // 2, _head_rows(hd % 2), :] for hd in range(nh)]
        carry = list(carry)
        for b in order:
            rows = slice(b * TK, (b + 1) * TK)
            for hd in range(nh):
                a = jnp.exp(zs[hd][rows, :] + suffixes[b][hd] + carry[hd])
                if diagonal and b == count - 1:
                    a = jnp.where(strict, a, 0.0)
                accs[hd] = accs[hd] + _dot(vt_ref[0, hd // 2, j0 + b, _head_rows(hd % 2), :], a.astype(BF16))
                carry[hd] = carry[hd] + suffixes[b][hd][0:1, :]
        for hd in range(nh):
            acc_scr[hd // 2, _head_rows(hd % 2), :] = accs[hd]
        return tuple(carry)

    zero = tuple(jnp.zeros((1, tq), F32) for _ in range(nh))
    carry = lax.cond(i > 0, lambda c: step(i - 1, 2, True, c), lambda c: step(i, 1, True, c), zero)

    def live(state):
        j, carry = state
        worst = functools.reduce(jnp.maximum, carry)
        return jnp.logical_and(j >= 0, jnp.max(worst) > EXP_UNDERFLOW)

    def body(state):
        j, carry = state
        return j - 1, step(j, 1, False, carry)

    lax.while_loop(live, body, (i - 2, carry))
    for pr in range(nh // 2):
        o_ref[0, :, pr * LANES:(pr + 1) * LANES] = acc_scr[pr].T.astype(BF16)


def _sb(zb, vt, tri_ge):
    b, s, _ = zb.shape
    nq, nk = s // TQ, s // TK
    n_pairs = W_SB // LANES
    return pl.pallas_call(
        _sb_body,
        grid=(b, nq),
        in_specs=[
            pl.BlockSpec((1, TQ, W_SB), lambda bi, i: (bi, i, Z_SQ // W_SB)),
            pl.BlockSpec((1, s, W_SB), lambda bi, i: (bi, 0, Z_SK // W_SB)),
            pl.BlockSpec((1, n_pairs, nk, LANES, TK), lambda bi, i: (bi, 0, 0, 0, 0)),
            pl.BlockSpec((TK, TK), lambda bi, i: (0, 0)),
        ],
        out_specs=pl.BlockSpec((1, TQ, W_SB), lambda bi, i: (bi, i, 0)),
        out_shape=jax.ShapeDtypeStruct((b, s, W_SB), BF16),
        scratch_shapes=[pltpu.VMEM((n_pairs, LANES, TQ), F32)],
        compiler_params=pltpu.CompilerParams(
            dimension_semantics=("parallel", "arbitrary"), vmem_limit_bytes=VMEM_LIMIT),
        name="sb",
    )(zb, zb, vt, tri_ge)


def _dsa_body(iq_ref, ik_ref, q_ref, k_ref, vt_ref, misct_ref, tri_ref, o_ref,
              keys_scr, hi_scr, lo_scr, bias_scr, acc_scr, *, topk):
    i = pl.program_id(1)
    tq = q_ref.shape[1]
    qb = tq // TK
    n_vis = qb * (i + 1)
    n_pairs = N_HEADS_DSA // 2
    i16_min, i16_max = -2 ** 15, 2 ** 15 - 1

    iq = iq_ref[0]
    iq_heads = _split_heads(iq[:, 0:LANES]) + _split_heads(iq[:, LANES:2 * LANES])
    w_rows = [misct_ref[0, 0, L_WIDX - L_FORGET + hh:L_WIDX - L_FORGET + hh + 1, :] for hh in range(N_IDX_HEADS)]

    def score_step(j0, count, ends, carry):
        kb = _key_rows(ik_ref, j0, count)
        dots = [_dot_nt(kb, iq_heads[hh]) for hh in range(N_IDX_HEADS)]
        for b in range(count):
            rows = slice(b * TK, (b + 1) * TK)
            sc = jnp.zeros((TK, tq), F32)
            for hh in range(N_IDX_HEADS):
                sc = sc + w_rows[hh] * jnp.maximum(dots[hh][rows, :], 0.0)
            d = b - (count - qb)
            if ends and d >= 0:
                sc = jnp.where(_visible(TK, tq, False, d * TK), sc, -jnp.inf)
            bits = lax.bitcast_convert_type(sc, jnp.int32)
            key = jnp.where(bits < 0, bits ^ jnp.int32(0x7FFFFFFF), bits)
            keys_scr[j0 + b] = key
            hi_scr[j0 + b] = lax.shift_right_arithmetic(key, 16).astype(jnp.int16)
            lo_scr[j0 + b] = ((key & 0xFFFF) - 2 ** 15).astype(jnp.int16)
        return carry

    _walk_blocks(n_vis, score_step, 0, always_even=True, group=4)

    n_block_pairs = n_vis // 2

    def count16(ref, cand):
        cand_b = jnp.broadcast_to(cand.astype(jnp.int16), (PACK16, tq))
        one, zero = jnp.ones((), jnp.int16), jnp.zeros((), jnp.int16)

        def body(n, accs):
            out = list(accs)
            for b in range(2):
                c = jnp.where(ref[2 * n + b].reshape(TK // PACK16, PACK16, tq) >= cand_b, one, zero)
                for a in range(N_COUNT_ACC):
                    g = [c[r] for r in range(a, TK // PACK16, N_COUNT_ACC)]
                    while len(g) > 1:
                        g = [g[r] + g[r + 1] for r in range(0, len(g), 2)]
                    out[a] = out[a] + g[0]
            return tuple(out)

        zeros = jnp.zeros((PACK16, tq), jnp.int16)
        accs = lax.fori_loop(0, n_block_pairs, body, (zeros,) * N_COUNT_ACC)
        acc = functools.reduce(lambda x, y: x + y, accs)
        return jnp.sum(acc.astype(jnp.int32).astype(F32), axis=0, keepdims=True)

    def count16_gt(ref, thr16):
        return jnp.where(thr16 == i16_max, 0.0, count16(ref, jnp.minimum(thr16 + 1, i16_max)))

    def bisect16(ref, want):
        def it(n, thr16):
            cand = thr16 + lax.shift_left(jnp.int32(1), 15 - n)
            return jnp.where(count16(ref, cand) >= want, cand, thr16)
        return lax.fori_loop(0, 16, it, jnp.full((1, tq), i16_min, jnp.int32))

    kf = jnp.float32(topk)
    t_hi = bisect16(hi_scr, kf)
    n_gt_hi = count16_gt(hi_scr, t_hi)
    t_hi_b = jnp.broadcast_to(t_hi.astype(jnp.int16), (PACK16, tq))

    def low_body(n, c):
        shape = (TK // PACK16, PACK16, tq)
        for j in (2 * n, 2 * n + 1):
            lo = jnp.where(hi_scr[j].reshape(shape) == t_hi_b, lo_scr[j].reshape(shape), jnp.int16(i16_min))
            lo_scr[j] = lo.reshape(TK, tq)
        return c

    lax.fori_loop(0, n_block_pairs, low_body, 0)
    t_lo = bisect16(lo_scr, kf - n_gt_hi)
    thr = t_hi * 2 ** 16 + t_lo + 2 ** 15
    n_gt_lo = count16_gt(lo_scr, t_lo)
    n_eq = count16(lo_scr, t_lo) - n_gt_lo
    need = kf - n_gt_hi - n_gt_lo

    def plain_step(j, diag):
        sel = keys_scr[j] >= thr
        if diag is not None:
            sel = sel & _visible(TK, tq, False, diag * TK)
        bias_scr[j] = jnp.where(sel, 0.0, NEG_BIG)

    def tie_step(j, run, diag):
        kblk = keys_scr[j]
        eq = kblk == thr
        eqf = jnp.where(eq, 1.0, 0.0)
        before = _dot(tri_ref[...], eqf.astype(BF16)) + run
        sel = (kblk > thr) | (eq & (before < need))
        if diag is not None:
            sel = sel & _visible(TK, tq, False, diag * TK)
        bias_scr[j] = jnp.where(sel, 0.0, NEG_BIG)
        return run + jnp.sum(eqf, axis=0, keepdims=True)

    def plain_bias():
        def body(j, c):
            plain_step(j, None)
            return c
        lax.fori_loop(0, n_vis - qb, body, 0)
        for d in range(qb):
            plain_step(n_vis - qb + d, d)

    def tie_bias():
        run = lax.fori_loop(0, n_vis - qb, lambda j, r: tie_step(j, r, None), jnp.zeros((1, tq), F32))
        for d in range(qb):
            run = tie_step(n_vis - qb + d, run, d)

    lax.cond(jnp.max(n_eq - need) > 0.0, tie_bias, plain_bias)

    q = q_ref[0]
    q_heads = _split_heads(q[:, 0:LANES]) + _split_heads(q[:, LANES:2 * LANES])
    acc_scr[...] = jnp.zeros_like(acc_scr)

    def attn_step(j0, count, ends, state):
        kb = _key_rows(k_ref, j0, count)
        scores = [_dot_nt(kb[:, (hh // 2) * LANES:(hh // 2 + 1) * LANES], q_heads[hh]) for hh in range(N_HEADS_DSA)]
        weights = []
        for b in range(count):
            bias = bias_scr[j0 + b]
            upd = [_softmax_update(scores[hh][b * TK:(b + 1) * TK, :] + bias, *state[hh])
                   for hh in range(N_HEADS_DSA)]
            state = tuple((u[2], u[3]) for u in upd)
            weights.append([(u[0], u[1]) for u in upd])
        accs = [acc_scr[hh // 2, _head_rows(hh % 2), :] for hh in range(N_HEADS_DSA)]
        for b in range(count):
            for hh in range(N_HEADS_DSA):
                p, alpha = weights[b][hh]
                accs[hh] = alpha * accs[hh] + _dot(vt_ref[0, hh // 2, j0 + b, _head_rows(hh % 2), :],
                                                   p.astype(BF16))
        for hh in range(N_HEADS_DSA):
            acc_scr[hh // 2, _head_rows(hh % 2), :] = accs[hh]
        return state

    init = tuple((jnp.full((1, tq), -jnp.inf, F32), jnp.zeros((1, tq), F32)) for _ in range(N_HEADS_DSA))
    state = _walk_blocks(n_vis, attn_step, init, always_even=True, group=4)
    outs = []
    for pr in range(n_pairs):
        inv = jnp.concatenate([jnp.broadcast_to(1.0 / state[2 * pr + hd][1], (HEAD_DIM, tq)) for hd in range(2)],
                              axis=0)
        outs.append((acc_scr[pr] * inv).T)
    o_ref[0] = jnp.concatenate(outs, axis=1).astype(BF16)


def _dsa(zb, vt, misct, tri_lt, topk):
    b, s, _ = zb.shape
    nq, nk = s // TQ_WIDE, s // TK
    n_pairs = W_DSA // LANES
    return pl.pallas_call(
        functools.partial(_dsa_body, topk=topk),
        grid=(b, nq),
        in_specs=[
            pl.BlockSpec((1, TQ_WIDE, W_IDX), lambda bi, i: (bi, i, Z_IQ // W_IDX)),
            pl.BlockSpec((1, s, LANES), lambda bi, i: (bi, 0, Z_IK // LANES)),
            pl.BlockSpec((1, TQ_WIDE, W_DSA), lambda bi, i: (bi, i, Z_CQ // W_DSA)),
            pl.BlockSpec((1, s, W_DSA), lambda bi, i: (bi, 0, Z_CK // W_DSA)),
            pl.BlockSpec((1, n_pairs, nk, LANES, TK), lambda bi, i: (bi, 0, 0, 0, 0)),
            pl.BlockSpec((1, 1, MISC_ROWS, TQ_WIDE), lambda bi, i: (bi, i, 0, 0)),
            pl.BlockSpec((TK, TK), lambda bi, i: (0, 0)),
        ],
        out_specs=pl.BlockSpec((1, TQ_WIDE, W_DSA), lambda bi, i: (bi, i, 0)),
        out_shape=jax.ShapeDtypeStruct((b, s, W_DSA), BF16),
        scratch_shapes=[
            pltpu.VMEM((nk, TK, TQ_WIDE), jnp.int32),
            pltpu.VMEM((nk, TK, TQ_WIDE), jnp.int16),
            pltpu.VMEM((nk, TK, TQ_WIDE), jnp.int16),
            pltpu.VMEM((nk, TK, TQ_WIDE), F32),
            pltpu.VMEM((n_pairs, LANES, TQ_WIDE), F32),
        ],
        compiler_params=pltpu.CompilerParams(
            dimension_semantics=("parallel", "arbitrary"), vmem_limit_bytes=VMEM_LIMIT),
        name="dsa",
    )(zb, zb, zb, zb, vt, misct, tri_lt)


def _merge_body(x_ref, g_ref, of_ref, os_ref, oc_ref, wg_ref, bg_ref, wf_ref, ws_ref, wc_ref, wo_ref, o_ref):
    x = x_ref[...]
    d = x.shape[1]
    h = _rms_rows(x, g_ref[...]).astype(BF16)
    merged = jnp.zeros_like(x)
    for n, (o_br, w_br) in enumerate(((of_ref, wf_ref), (os_ref, ws_ref), (oc_ref, wc_ref))):
        gate = _dot(h, wg_ref[:, n * d:(n + 1) * d]) + bg_ref[n:n + 1, :]
        gate = 1.0 / (1.0 + jnp.exp(-gate))
        merged = merged + gate * _dot(o_br[...], w_br[...])
    o_ref[...] = x + _dot(merged.astype(BF16), wo_ref[...])


def _merge(x2, g, o_f, o_s, o_c, w_gates, b_gates, w_f, w_s, w_c, w_o):
    m, d = x2.shape
    tm = min(TM_MERGE, m)
    row = lambda w: pl.BlockSpec((tm, w), lambda i: (i, 0))
    const = lambda a: pl.BlockSpec(a.shape, lambda i: (0, 0))
    return pl.pallas_call(
        _merge_body,
        grid=(m // tm,),
        in_specs=[row(d), const(g), row(W_FOX), row(W_SB), row(W_DSA), const(w_gates), const(b_gates),
                  const(w_f), const(w_s), const(w_c), const(w_o)],
        out_specs=row(d),
        out_shape=jax.ShapeDtypeStruct((m, d), F32),
        compiler_params=pltpu.CompilerParams(
            dimension_semantics=("parallel",), vmem_limit_bytes=VMEM_LIMIT),
        name="merge",
    )(x2, g, o_f, o_s, o_c, w_gates, b_gates, w_f, w_s, w_c, w_o)


def _tile_heads(g, n):
    return jnp.tile(g.astype(F32), n)


def _constants(tm_proj):
    r = np.arange(W_FOX)
    bd = (r[:, None] // HEAD_DIM == r[None, :] // HEAD_DIM).astype(np.float32)
    rt = np.arange(tm_proj)
    tri_tok = (rt[None, :] <= rt[:, None]).astype(np.float32)
    rk = np.arange(TK)
    tri_ge = (rk[None, :] >= rk[:, None]).astype(np.float32)
    tri_lt = (rk[None, :] < rk[:, None]).astype(np.float32)

    n_src = W_FOX + N_BIAS_FEATS * LANES
    pq = np.zeros((n_src, W_FOX_AUG), np.float32)
    pk = np.zeros((n_src, W_FOX_AUG), np.float32)
    ones_q = np.zeros((1, W_FOX_AUG), np.float32)
    ones_k = np.zeros((1, W_FOX_AUG), np.float32)
    for hd in range(N_HEADS_FOX):
        for dd in range(HEAD_DIM):
            pq[hd * HEAD_DIM + dd, hd * LANES + dd] = 1.0
            pk[hd * HEAD_DIM + dd, hd * LANES + dd] = 1.0
        for part in range(N_BIAS_FEATS):
            src_row = W_FOX + part * LANES + L_FORGET + hd
            pq[src_row, hd * LANES + HEAD_DIM + part] = 1.0
            ones_q[0, hd * LANES + HEAD_DIM + N_BIAS_FEATS + part] = 1.0
            ones_k[0, hd * LANES + HEAD_DIM + part] = 1.0
            pk[src_row, hd * LANES + HEAD_DIM + N_BIAS_FEATS + part] = -1.0
    half = ROT_DIM // 2
    inv_freq = jnp.power(ROPE_THETA, -jnp.arange(half, dtype=F32) * 2.0 / ROT_DIM)
    dlane = np.arange(LANES) % HEAD_DIM
    freq = jnp.where(jnp.asarray(dlane < ROT_DIM), inv_freq[jnp.asarray(dlane % half)], 0.0)
    return dict(bd=jnp.asarray(bd, BF16), tri_tok=jnp.asarray(tri_tok, BF16), tri_ge=jnp.asarray(tri_ge, BF16),
                tri_lt=jnp.asarray(tri_lt, BF16), pq=jnp.asarray(pq, BF16), pk=jnp.asarray(pk, BF16),
                ones_q=jnp.asarray(ones_q, F32), ones_k=jnp.asarray(ones_k, F32),
                freq=freq.reshape(1, LANES).astype(F32))


def kernel(x, positions, ffn1_norm, ffn1_w_gate, ffn1_w_up, ffn1_w_down, mix_norm, w_in, b_forget, b_gates, q_norm_fox, k_norm_fox, q_norm_sb, k_norm_sb, q_norm_dsa, k_norm_dsa, w_branch_fox, w_branch_sb, w_branch_dsa, w_out, ffn2_norm, ffn2_w_gate, ffn2_w_up, ffn2_w_down):
    b, s, d = x.shape
    depth = w_in.shape[0]
    topk = min(TOPK_MAX, s // 4)
    assert s % TQ_WIDE == 0 and TQ == TK and TQ_WIDE % (2 * TK) == 0 and d % LANES == 0
    consts = _constants(min(TM_PROJ, s))
    tm_proj = min(TM_PROJ, s)
    rope_cos, rope_sin = _rope_tables(positions.reshape(b, s // tm_proj, 1, tm_proj), consts["freq"])
    scale = HEAD_DIM ** -0.5
    idx_scale = IDX_DIM ** -0.5

    splits = (W_FOX, W_FOX, W_FOX, N_HEADS_FOX, W_SB, W_SB, W_SB, W_DSA, W_DSA, W_DSA,
              W_IDX, IDX_DIM, N_IDX_HEADS, d, d, d)
    offs = np.concatenate([[0], np.cumsum(splits)]).tolist()
    (o_qf, o_kf, o_vf, o_ff, o_qs, o_ks, o_vs, o_qc, o_kc, o_vc, o_qi, o_ki, o_wi, o_ga) = offs[:14]

    x2 = x.reshape(b * s, d)
    for l in range(depth):
        wl = w_in[l].astype(BF16)
        pad = jnp.zeros((d, LANES - IDX_DIM - N_HEADS_FOX - N_IDX_HEADS), BF16)
        w_proj = jnp.concatenate([
            wl[:, o_qf:o_ff], wl[:, o_qs:o_qi], wl[:, o_qi:o_ki],
            wl[:, o_ki:o_wi], wl[:, o_ff:o_qs], wl[:, o_wi:o_ga], pad], axis=1)
        w_gates = wl[:, o_ga:]
        ones = lambda n: jnp.ones((n,), F32)
        gains = jnp.concatenate([
            _tile_heads(q_norm_fox[l], N_HEADS_FOX) * scale, _tile_heads(k_norm_fox[l], N_HEADS_FOX), ones(W_FOX),
            _tile_heads(q_norm_sb[l], N_HEADS_SB) * scale, _tile_heads(k_norm_sb[l], N_HEADS_SB), ones(W_SB),
            _tile_heads(q_norm_dsa[l], N_HEADS_DSA) * scale, _tile_heads(k_norm_dsa[l], N_HEADS_DSA), ones(W_DSA),
            ones(W_IDX) * idx_scale, ones(LANES)]).reshape(1, N_PROJ)
        fbias = jnp.zeros((LANES,), F32).at[L_FORGET:L_FORGET + N_HEADS_FOX].set(b_forget[l].astype(F32))
        fbias = fbias.reshape(1, LANES)

        x2 = _ffn(x2, ffn1_norm[l].reshape(1, d), ffn1_w_gate[l].astype(BF16), ffn1_w_up[l].astype(BF16),
                  ffn1_w_down[l].astype(BF16))
        zb, vt_f, vt_s, vt_c, misct = _proj(x2.reshape(b, s, d), rope_cos, rope_sin, mix_norm[l].reshape(1, d),
                                            w_proj, gains, consts, fbias)
        o_f = _fox(zb, vt_f)
        o_s = _sb(zb, vt_s, consts["tri_ge"])
        o_c = _dsa(zb, vt_c, misct, consts["tri_lt"], topk)
        x2 = _merge(x2, mix_norm[l].reshape(1, d), o_f.reshape(b * s, W_FOX), o_s.reshape(b * s, W_SB),
                    o_c.reshape(b * s, W_DSA), w_gates, b_gates[l].astype(F32),
                    w_branch_fox[l].astype(BF16), w_branch_sb[l].astype(BF16), w_branch_dsa[l].astype(BF16),
                    w_out[l].astype(BF16))
        x2 = _ffn(x2, ffn2_norm[l].reshape(1, d), ffn2_w_gate[l].astype(BF16), ffn2_w_up[l].astype(BF16),
                  ffn2_w_down[l].astype(BF16))
    return x2.reshape(b, s, d)
```

```python
import functools

import jax
import jax.numpy as jnp
import numpy as np
from jax import lax
from jax.experimental import pallas as pl
from jax.experimental.pallas import tpu as pltpu

F32 = jnp.float32
BF16 = jnp.bfloat16

HEAD_DIM = 64
N_HEADS_FOX = 6
N_HEADS_SB = 6
N_HEADS_DSA = 4
N_IDX_HEADS = 4
IDX_DIM = 64
TOPK_MAX = 256
ROPE_THETA = 500000.0
ROT_DIM = HEAD_DIM // 4
EPS = 1e-6

LANES = 128
SUBLANES = 8
MXU_DIM = 256
FFN_CHUNK = 1536
PACK16 = 2 * SUBLANES
N_COUNT_ACC = 2
W_FOX = N_HEADS_FOX * HEAD_DIM
W_SB = N_HEADS_SB * HEAD_DIM
W_DSA = N_HEADS_DSA * HEAD_DIM
W_IDX = N_IDX_HEADS * IDX_DIM

C_FQ, C_FK, C_FV = 0, W_FOX, 2 * W_FOX
C_SQ, C_SK, C_SV = 3 * W_FOX, 3 * W_FOX + W_SB, 3 * W_FOX + 2 * W_SB
C_CQ = 3 * W_FOX + 3 * W_SB
C_CK, C_CV = C_CQ + W_DSA, C_CQ + 2 * W_DSA
C_IQ = C_CQ + 3 * W_DSA
C_MISC = C_IQ + W_IDX
N_PROJ = C_MISC + LANES
L_FORGET = IDX_DIM
L_WIDX = IDX_DIM + N_HEADS_FOX
MISC_ROWS = 16

W_FOX_AUG = N_HEADS_FOX * LANES
Z_FQ, Z_FK = 0, W_FOX_AUG
Z_SQ, Z_SK = 2 * W_FOX_AUG, 2 * W_FOX_AUG + W_SB
Z_CQ = 2 * W_FOX_AUG + 2 * W_SB
Z_CK = Z_CQ + W_DSA
Z_IQ = Z_CK + W_DSA
Z_IK = Z_IQ + W_IDX
N_ZB = Z_IK + LANES
N_BIAS_FEATS = 3

TQ = 256
TK = 256
TQ_WIDE = 2 * TK
TM_PROJ = 512
TM_FFN = 512
TM_MERGE = 512
NEG_BIG = -1e30
EXP_UNDERFLOW = -110.0
VMEM_LIMIT = 56 * 1024 * 1024

_NT = (((1,), (1,)), ((), ()))


def _dot(a, b):
    return jnp.dot(a, b, preferred_element_type=F32)


def _dot_nt(a, b):
    return lax.dot_general(a, b, _NT, preferred_element_type=F32)


def _split2(x):
    hi = x.astype(BF16)
    lo = (x - hi.astype(F32)).astype(BF16)
    return hi, lo


def _split3(x):
    hi = x.astype(BF16)
    r = x - hi.astype(F32)
    mid = r.astype(BF16)
    lo = (r - mid.astype(F32)).astype(BF16)
    return hi, mid, lo


def _rms_rows(x, g):
    ms = jnp.mean(x * x, axis=-1, keepdims=True)
    return x * lax.rsqrt(ms + EPS) * g


def _softplus(z):
    return jnp.maximum(z, 0.0) + jnp.log(1.0 + jnp.exp(-jnp.abs(z)))


def _ffn_body(x_ref, g_ref, wg_ref, wu_ref, wd_ref, o_ref, *, chunks):
    x = x_ref[...]
    h = _rms_rows(x, g_ref[...]).astype(BF16)
    acc = None
    for c0, c1 in chunks:
        a = _dot(h, wg_ref[:, c0:c1])
        u = _dot(h, wu_ref[:, c0:c1])
        p = (a * (1.0 / (1.0 + jnp.exp(-a))) * u).astype(BF16)
        part = _dot(p, wd_ref[c0:c1, :])
        acc = part if acc is None else acc + part
    o_ref[...] = x + 0.5 * acc


def _ffn_chunks(d_ff):
    unit = MXU_DIM if d_ff % MXU_DIM == 0 else LANES
    n_units = d_ff // unit
    n_chunks = -(-d_ff // FFN_CHUNK)
    bounds = [unit * (n_units * c // n_chunks) for c in range(n_chunks + 1)]
    return tuple(zip(bounds[:-1], bounds[1:]))


def _resident(shape):
    return pl.BlockSpec(shape, lambda *_: (0,) * len(shape), pipeline_mode=pl.Buffered(1))


def _ffn(x2, g, wg, wu, wd):
    m, d = x2.shape
    d_ff = wg.shape[1]
    tm = min(TM_FFN, m)
    return pl.pallas_call(
        functools.partial(_ffn_body, chunks=_ffn_chunks(d_ff)),
        grid=(m // tm,),
        in_specs=[
            pl.BlockSpec((tm, d), lambda i: (i, 0)),
            _resident((1, d)), _resident((d, d_ff)), _resident((d, d_ff)), _resident((d_ff, d)),
        ],
        out_specs=pl.BlockSpec((tm, d), lambda i: (i, 0)),
        out_shape=jax.ShapeDtypeStruct((m, d), F32),
        compiler_params=pltpu.CompilerParams(
            dimension_semantics=("parallel",), vmem_limit_bytes=VMEM_LIMIT),
        name="ffn",
    )(x2, g, wg, wu, wd)


def _proj_body(x_ref, cos_ref, sin_ref, g_ref, w_ref, gains_ref, bd_ref, fbias_ref, tri_ref,
               pq_ref, pk_ref, ones_q_ref, ones_k_ref,
               zb_ref, vtf_ref, vts_ref, vtc_ref, misct_ref, carry_scr):
    t = pl.program_id(1)
    tm = x_ref.shape[1]
    h = _rms_rows(x_ref[0], g_ref[...]).astype(BF16)

    lane = lax.broadcasted_iota(jnp.int32, (1, LANES), 1)
    first_half = lane % HEAD_DIM < ROT_DIM // 2
    shared = {}

    def rope(x):
        partner = jnp.where(first_half, pltpu.roll(x, LANES - ROT_DIM // 2, 1), pltpu.roll(x, ROT_DIM // 2, 1))
        return x * cos_ref[0] + partner * sin_ref[0]

    def head_norm(z, c0, width):
        hi, lo = _split2(z * z)
        bd = bd_ref[:width, :width]
        ss = _dot(hi, bd) + _dot(lo, bd)
        return z * lax.rsqrt(ss * (1.0 / HEAD_DIM) + EPS) * gains_ref[:, c0:c0 + width]

    def project(c0, width):
        return _dot(h, w_ref[:, c0:c0 + width])

    def finish_heads(z, c0, width, norm, rot):
        if norm:
            z = head_norm(z, c0, width)
        else:
            z = z * gains_ref[:, c0:c0 + width]
        if rot:
            z = jnp.concatenate([rope(z[:, c:c + LANES]) for c in range(0, width, LANES)], axis=1)
        return z

    def finish_index_key(zm):
        ki = rope(zm)
        zb_ref[0, :, Z_IK:Z_IK + LANES] = jnp.where(lane < IDX_DIM, ki, pltpu.roll(ki, IDX_DIM, 1)).astype(BF16)

    def finish_misc(zm):
        shared["zm"] = zm

        @pl.when(t == 0)
        def _():
            carry_scr[...] = jnp.zeros_like(carry_scr)

        logf = -_softplus(-(zm + fbias_ref[...]))
        hi, mid, lo = _split3(logf)
        tri = tri_ref[...]
        cum = _dot(tri, hi) + _dot(tri, mid) + _dot(tri, lo) + carry_scr[0:1, :]
        carry_scr[...] = jnp.broadcast_to(cum[tm - 1:tm, :], carry_scr.shape)
        is_forget = (lane >= L_FORGET) & (lane < L_FORGET + N_HEADS_FOX)
        is_widx = (lane >= L_WIDX) & (lane < L_WIDX + N_IDX_HEADS)
        misc = jnp.where(is_forget, cum, jnp.where(is_widx, zm * (N_IDX_HEADS ** -0.5), 0.0))
        misc_t = misc.T
        for c in range(tm // TQ_WIDE):
            misct_ref[0, c] = misc_t[L_FORGET:L_FORGET + MISC_ROWS, c * TQ_WIDE:(c + 1) * TQ_WIDE]
        shared["cum"] = _split3(cum)

    def finish_fox(z, c0, z0, p_ref, ones_ref):
        zn = finish_heads(z, c0, W_FOX, True, False).astype(BF16)
        src = jnp.concatenate((zn,) + shared["cum"], axis=1)
        zb_ref[0, :, z0:z0 + W_FOX_AUG] = (_dot(src, p_ref[...]) + ones_ref[...]).astype(BF16)

    def finish_plain(z, c0, z0, width, norm, rot):
        zb_ref[0, :, z0:z0 + width] = finish_heads(z, c0, width, norm, rot).astype(BF16)

    def finish_values(z, vt_ref, width):
        zt = z.T
        for p in range(width // LANES):
            for c in range(tm // TK):
                vt_ref[0, p, c] = zt[p * LANES:(p + 1) * LANES, c * TK:(c + 1) * TK].astype(BF16)

    P = functools.partial
    stages = [
        [(C_MISC, LANES, finish_misc)],
        [(C_FQ, W_FOX, P(finish_fox, c0=C_FQ, z0=Z_FQ, p_ref=pq_ref, ones_ref=ones_q_ref)),
         (C_FK, W_FOX, P(finish_fox, c0=C_FK, z0=Z_FK, p_ref=pk_ref, ones_ref=ones_k_ref))],
        [(C_FV, W_FOX, P(finish_values, vt_ref=vtf_ref, width=W_FOX)),
         (C_SQ, W_SB, P(finish_plain, c0=C_SQ, z0=Z_SQ, width=W_SB, norm=True, rot=False))],
        [(C_SK, W_SB, P(finish_plain, c0=C_SK, z0=Z_SK, width=W_SB, norm=True, rot=False)),
         (C_SV, W_SB, P(finish_values, vt_ref=vts_ref, width=W_SB))],
        [(C_CQ, W_DSA, P(finish_plain, c0=C_CQ, z0=Z_CQ, width=W_DSA, norm=True, rot=True)),
         (C_CK, W_DSA, P(finish_plain, c0=C_CK, z0=Z_CK, width=W_DSA, norm=True, rot=True))],
        [(C_CV, W_DSA, P(finish_values, vt_ref=vtc_ref, width=W_DSA)),
         (C_IQ, W_IDX, P(finish_plain, c0=C_IQ, z0=Z_IQ, width=W_IDX, norm=False, rot=True))],
    ]

    def finish_chunk(groups, z):
        c_first = groups[0][0]
        for c0, width, finish in groups:
            finish(z[:, c0 - c_first:c0 - c_first + width])

    pending = None
    for groups in stages:
        assert all(a[0] + a[1] == b[0] for a, b in zip(groups, groups[1:]))
        z = project(groups[0][0], sum(width for _, width, _ in groups))
        if pending is not None:
            finish_chunk(*pending)
        pending = (groups, z)
    finish_chunk(*pending)
    finish_index_key(shared["zm"])


def _rope_body(pos_ref, freq_ref, cos_ref, sin_ref):
    tm = pos_ref.shape[3]
    lane = lax.broadcasted_iota(jnp.int32, (1, LANES), 1)
    first_half = lane % HEAD_DIM < ROT_DIM // 2
    pos_cols = jnp.broadcast_to(pos_ref[0, 0].astype(F32), (LANES, tm)).T
    ang = pos_cols * freq_ref[...]
    cos_ref[0] = jnp.cos(ang)
    sin_ref[0] = jnp.where(first_half, -jnp.sin(ang), jnp.sin(ang))


def _rope_tables(pos3, freq):
    b, nt, _, tm = pos3.shape
    table = pl.BlockSpec((1, tm, LANES), lambda bi, ti: (bi, ti, 0))
    return pl.pallas_call(
        _rope_body,
        grid=(b, nt),
        in_specs=[pl.BlockSpec((1, 1, 1, tm), lambda bi, ti: (bi, ti, 0, 0)),
                  pl.BlockSpec((1, LANES), lambda bi, ti: (0, 0))],
        out_specs=[table, table],
        out_shape=[jax.ShapeDtypeStruct((b, nt * tm, LANES), F32)] * 2,
        compiler_params=pltpu.CompilerParams(dimension_semantics=("parallel", "parallel")),
        name="rope",
    )(pos3, freq)


def _proj(x, rope_cos, rope_sin, g, w, gains, consts, fbias):
    b, s, d = x.shape
    tm = min(TM_PROJ, s)
    const = lambda a: pl.BlockSpec(a.shape, lambda bi, ti: (0,) * a.ndim)
    vt_spec = lambda n: pl.BlockSpec((1, n, tm // TK, LANES, TK), lambda bi, ti: (bi, 0, ti, 0, 0))
    vt_shape = lambda n: jax.ShapeDtypeStruct((b, n, s // TK, LANES, TK), BF16)
    cs = (consts["bd"], fbias, consts["tri_tok"], consts["pq"], consts["pk"], consts["ones_q"], consts["ones_k"])
    table = pl.BlockSpec((1, tm, LANES), lambda bi, ti: (bi, ti, 0))
    return pl.pallas_call(
        _proj_body,
        grid=(b, s // tm),
        in_specs=[
            pl.BlockSpec((1, tm, d), lambda bi, ti: (bi, ti, 0)),
            table, table,
            const(g), const(w), const(gains),
        ] + [const(a) for a in cs],
        out_specs=[
            pl.BlockSpec((1, tm, N_ZB), lambda bi, ti: (bi, ti, 0)),
            vt_spec(W_FOX // LANES), vt_spec(W_SB // LANES), vt_spec(W_DSA // LANES),
            pl.BlockSpec((1, tm // TQ_WIDE, MISC_ROWS, TQ_WIDE), lambda bi, ti: (bi, ti, 0, 0)),
        ],
        out_shape=[
            jax.ShapeDtypeStruct((b, s, N_ZB), BF16),
            vt_shape(W_FOX // LANES), vt_shape(W_SB // LANES), vt_shape(W_DSA // LANES),
            jax.ShapeDtypeStruct((b, s // TQ_WIDE, MISC_ROWS, TQ_WIDE), F32),
        ],
        scratch_shapes=[pltpu.VMEM((SUBLANES, LANES), F32)],
        compiler_params=pltpu.CompilerParams(
            dimension_semantics=("parallel", "arbitrary"), vmem_limit_bytes=VMEM_LIMIT),
        name="proj",
    )(x, rope_cos, rope_sin, g, w, gains, *cs)


def _head_masks():
    lane = lax.broadcasted_iota(jnp.int32, (1, LANES), 1)
    return lane < HEAD_DIM, lane >= HEAD_DIM


def _split_heads(q128):
    lo_half, hi_half = _head_masks()
    zero = jnp.zeros_like(q128)
    return jnp.where(lo_half, q128, zero), jnp.where(hi_half, q128, zero)


def _visible(tk, tq, strict, offset=0):
    key = lax.broadcasted_iota(jnp.int32, (tk, tq), 0) + offset
    qry = lax.broadcasted_iota(jnp.int32, (tk, tq), 1)
    return key < qry if strict else key <= qry


def _softmax_update(s, m, l):
    m_new = jnp.maximum(m, jnp.max(s, axis=0, keepdims=True))
    alpha = jnp.exp(m - m_new)
    p = jnp.exp(s - m_new)
    return p, alpha, m_new, alpha * l + jnp.sum(p, axis=0, keepdims=True)


def _head_rows(hd):
    return slice(hd * HEAD_DIM, (hd + 1) * HEAD_DIM)


def _walk_blocks(n_blocks, step, state, always_even=False, group=2):
    last = n_blocks - 1
    if always_even and group == 4:
        state = lax.fori_loop(0, (n_blocks - 2) // 4, lambda n, st: step(4 * n, 4, False, st), state)
        return lax.cond(n_blocks % 4 == 0,
                        lambda st: step(n_blocks - 4, 4, True, st),
                        lambda st: step(n_blocks - 2, 2, True, st),
                        state)
    state = lax.fori_loop(0, last // 2, lambda n, st: step(2 * n, 2, False, st), state)
    if always_even:
        return step(last - 1, 2, True, state)
    return lax.cond(last % 2 == 1,
                    lambda st: step(last - 1, 2, True, st),
                    lambda st: step(last, 1, True, st),
                    state)


def _key_rows(ref, j0, count):
    return ref[0, pl.ds(pl.multiple_of(j0 * TK, TK), count * TK), :]


def _fox_body(q_ref, k_ref, vt_ref, o_ref, acc_scr):
    i = pl.program_id(1)
    tq = q_ref.shape[1]
    qb = tq // TK
    nh = N_HEADS_FOX
    q_heads = [q_ref[0, :, hd * LANES:(hd + 1) * LANES] for hd in range(nh)]
    acc_scr[...] = jnp.zeros_like(acc_scr)

    def step(j0, count, ends, state):
        kb = _key_rows(k_ref, j0, count)
        scores = [_dot_nt(kb[:, hd * LANES:(hd + 1) * LANES], q_heads[hd]) for hd in range(nh)]
        weights = []
        for b in range(count):
            s = [sh[b * TK:(b + 1) * TK, :] for sh in scores]
            d = b - (count - qb)
            if ends and d >= 0:
                vis = _visible(TK, tq, False, d * TK)
                s = [jnp.where(vis, sh, -jnp.inf) for sh in s]
            upd = [_softmax_update(s[hd], *state[hd]) for hd in range(nh)]
            state = tuple((u[2], u[3]) for u in upd)
            weights.append([(u[0], u[1]) for u in upd])
        accs = [acc_scr[hd // 2, _head_rows(hd % 2), :] for hd in range(nh)]
        for b in range(count):
            for hd in range(nh):
                p, alpha = weights[b][hd]
                accs[hd] = alpha * accs[hd] + _dot(vt_ref[0, hd // 2, j0 + b, _head_rows(hd % 2), :], p.astype(BF16))
        for hd in range(nh):
            acc_scr[hd // 2, _head_rows(hd % 2), :] = accs[hd]
        return state

    init = tuple((jnp.full((1, tq), -jnp.inf, F32), jnp.zeros((1, tq), F32)) for _ in range(nh))
    state = _walk_blocks(qb * (i + 1), step, init, always_even=qb % 2 == 0, group=4)
    for pr in range(nh // 2):
        inv = jnp.concatenate([jnp.broadcast_to(1.0 / state[2 * pr + hd][1], (HEAD_DIM, tq)) for hd in range(2)],
                              axis=0)
        o_ref[0, :, pr * LANES:(pr + 1) * LANES] = (acc_scr[pr] * inv).T.astype(BF16)


def _fox(zb, vt):
    b, s, _ = zb.shape
    nq, nk = s // TQ_WIDE, s // TK
    n_pairs = W_FOX // LANES
    return pl.pallas_call(
        _fox_body,
        grid=(b, nq),
        in_specs=[
            pl.BlockSpec((1, TQ_WIDE, W_FOX_AUG), lambda bi, i: (bi, i, Z_FQ // W_FOX_AUG)),
            pl.BlockSpec((1, s, W_FOX_AUG), lambda bi, i: (bi, 0, Z_FK // W_FOX_AUG)),
            pl.BlockSpec((1, n_pairs, nk, LANES, TK), lambda bi, i: (bi, 0, 0, 0, 0)),
        ],
        out_specs=pl.BlockSpec((1, TQ_WIDE, W_FOX), lambda bi, i: (bi, i, 0)),
        out_shape=jax.ShapeDtypeStruct((b, s, W_FOX), BF16),
        scratch_shapes=[pltpu.VMEM((n_pairs, LANES, TQ_WIDE), F32)],
        compiler_params=pltpu.CompilerParams(
            dimension_semantics=("parallel", "arbitrary"), vmem_limit_bytes=VMEM_LIMIT),
        name="fox",
    )(zb, zb, vt)


def _sb_body(q_ref, k_ref, vt_ref, tri_ref, o_ref, acc_scr):
    i = pl.program_id(1)
    tq = q_ref.shape[1]
    nh = N_HEADS_SB
    q_heads = []
    for pr in range(nh // 2):
        q_heads.extend(_split_heads(q_ref[0, :, pr * LANES:(pr + 1) * LANES]))
    acc_scr[...] = jnp.zeros_like(acc_scr)

    def step(j0, count, diagonal, carry):
        kb = _key_rows(k_ref, j0, count)
        tri = tri_ref[...]
        zs = [_dot_nt(kb[:, (hd // 2) * LANES:(hd // 2 + 1) * LANES], q_heads[hd]) for hd in range(nh)]
        strict = _visible(TK, tq, True)
        order = list(range(count - 1, -1, -1))
        suffixes = {}
        for b in order:
            rows = slice(b * TK, (b + 1) * TK)
            loms = [-_softplus(z[rows, :]) for z in zs]
            if diagonal and b == count - 1:
                loms = [jnp.where(strict, lom, 0.0) for lom in loms]
            splits = [_split2(lom) for lom in loms]
            suffixes[b] = [_dot(tri, hi) + _dot(tri, lo) for hi, lo in splits]
        accs = [acc_scr[hd // 2, _head_rows(hd % 2), :] for hd in range(nh)]
        carry = list(carry)
        for b in order:
            rows = slice(b * TK, (b + 1) * TK)
            for hd in range(nh):
                a = jnp.exp(zs[hd][rows, :] + suffixes[b][hd] + carry[hd])
                if diagonal and b == count - 1:
                    a = jnp.where(strict, a, 0.0)
                accs[hd] = accs[hd] + _dot(vt_ref[0, hd // 2, j0 + b, _head_rows(hd % 2), :], a.astype(BF16))
                carry[hd] = carry[hd] + suffixes[b][hd][0:1, :]
        for hd in range(nh):
            acc_scr[hd // 2, _head_rows(hd % 2), :] = accs[hd]
        return tuple(carry)

    zero = tuple(jnp.zeros((1, tq), F32) for _ in range(nh))
    carry = lax.cond(i > 0, lambda c: step(i - 1, 2, True, c), lambda c: step(i, 1, True, c), zero)

    def live(state):
        j, carry = state
        worst = functools.reduce(jnp.maximum, carry)
        return jnp.logical_and(j >= 0, jnp.max(worst) > EXP_UNDERFLOW)

    def body(state):
        j, carry = state
        return j - 1, step(j, 1, False, carry)

    lax.while_loop(live, body, (i - 2, carry))
    for pr in range(nh // 2):
        o_ref[0, :, pr * LANES:(pr + 1) * LANES] = acc_scr[pr].T.astype(BF16)


def _sb(zb, vt, tri_ge):
    b, s, _ = zb.shape
    nq, nk = s // TQ, s // TK
    n_pairs = W_SB // LANES
    return pl.pallas_call(
        _sb_body,
        grid=(b, nq),
        in_specs=[
            pl.BlockSpec((1, TQ, W_SB), lambda bi, i: (bi, i, Z_SQ // W_SB)),
            pl.BlockSpec((1, s, W_SB), lambda bi, i: (bi, 0, Z_SK // W_SB)),
            pl.BlockSpec((1, n_pairs, nk, LANES, TK), lambda bi, i: (bi, 0, 0, 0, 0)),
            pl.BlockSpec((TK, TK), lambda bi, i: (0, 0)),
        ],
        out_specs=pl.BlockSpec((1, TQ, W_SB), lambda bi, i: (bi, i, 0)),
        out_shape=jax.ShapeDtypeStruct((b, s, W_SB), BF16),
        scratch_shapes=[pltpu.VMEM((n_pairs, LANES, TQ), F32)],
        compiler_params=pltpu.CompilerParams(
            dimension_semantics=("parallel", "arbitrary"), vmem_limit_bytes=VMEM_LIMIT),
        name="sb",
    )(zb, zb, vt, tri_ge)


def _dsa_body(iq_ref, ik_ref, q_ref, k_ref, vt_ref, misct_ref, tri_ref, o_ref,
              keys_scr, hi_scr, lo_scr, bias_scr, acc_scr, *, topk):
    i = pl.program_id(1)
    tq = q_ref.shape[1]
    qb = tq // TK
    n_vis = qb * (i + 1)
    n_pairs = N_HEADS_DSA // 2
    i16_min, i16_max = -2 ** 15, 2 ** 15 - 1

    iq = iq_ref[0]
    iq_heads = _split_heads(iq[:, 0:LANES]) + _split_heads(iq[:, LANES:2 * LANES])
    w_rows = [misct_ref[0, 0, L_WIDX - L_FORGET + hh:L_WIDX - L_FORGET + hh + 1, :] for hh in range(N_IDX_HEADS)]

    def score_step(j0, count, ends, carry):
        kb = _key_rows(ik_ref, j0, count)
        dots = [_dot_nt(kb, iq_heads[hh]) for hh in range(N_IDX_HEADS)]
        for b in range(count):
            rows = slice(b * TK, (b + 1) * TK)
            sc = jnp.zeros((TK, tq), F32)
            for hh in range(N_IDX_HEADS):
                sc = sc + w_rows[hh] * jnp.maximum(dots[hh][rows, :], 0.0)
            d = b - (count - qb)
            if ends and d >= 0:
                sc = jnp.where(_visible(TK, tq, False, d * TK), sc, -jnp.inf)
            bits = lax.bitcast_convert_type(sc, jnp.int32)
            key = jnp.where(bits < 0, bits ^ jnp.int32(0x7FFFFFFF), bits)
            keys_scr[j0 + b] = key
            hi_scr[j0 + b] = lax.shift_right_arithmetic(key, 16).astype(jnp.int16)
            lo_scr[j0 + b] = ((key & 0xFFFF) - 2 ** 15).astype(jnp.int16)
        return carry

    _walk_blocks(n_vis, score_step, 0, always_even=True, group=4)

    n_block_pairs = n_vis // 2

    def count16(ref, cand):
        cand_b = jnp.broadcast_to(cand.astype(jnp.int16), (PACK16, tq))
        one, zero = jnp.ones((), jnp.int16), jnp.zeros((), jnp.int16)

        def body(n, accs):
            out = list(accs)
            for b in range(2):
                c = jnp.where(ref[2 * n + b].reshape(TK // PACK16, PACK16, tq) >= cand_b, one, zero)
                for a in range(N_COUNT_ACC):
                    g = [c[r] for r in range(a, TK // PACK16, N_COUNT_ACC)]
                    while len(g) > 1:
                        g = [g[r] + g[r + 1] for r in range(0, len(g), 2)]
                    out[a] = out[a] + g[0]
            return tuple(out)

        zeros = jnp.zeros((PACK16, tq), jnp.int16)
        accs = lax.fori_loop(0, n_block_pairs, body, (zeros,) * N_COUNT_ACC)
        acc = functools.reduce(lambda x, y: x + y, accs)
        return jnp.sum(acc.astype(jnp.int32).astype(F32), axis=0, keepdims=True)

    def count16_gt(ref, thr16):
        return jnp.where(thr16 == i16_max, 0.0, count16(ref, jnp.minimum(thr16 + 1, i16_max)))

    def bisect16(ref, want):
        def it(n, state):
            thr16, cnt = state
            cand = thr16 + lax.shift_left(jnp.int32(1), 15 - n)
            c = count16(ref, cand)
            ok = c >= want
            return jnp.where(ok, cand, thr16), jnp.where(ok, c, cnt)
        everything = jnp.full((1, tq), n_vis * TK, jnp.int32).astype(F32)
        return lax.fori_loop(0, 16, it, (jnp.full((1, tq), i16_min, jnp.int32), everything))

    kf = jnp.float32(topk)
    t_hi, _ = bisect16(hi_scr, kf)
    n_gt_hi = count16_gt(hi_scr, t_hi)
    t_hi_b = jnp.broadcast_to(t_hi.astype(jnp.int16), (PACK16, tq))

    def low_body(n, c):
        shape = (TK // PACK16, PACK16, tq)
        for j in (2 * n, 2 * n + 1):
            lo = jnp.where(hi_scr[j].reshape(shape) == t_hi_b, lo_scr[j].reshape(shape), jnp.int16(i16_min))
            lo_scr[j] = lo.reshape(TK, tq)
        return c

    lax.fori_loop(0, n_block_pairs, low_body, 0)
    want_lo = kf - n_gt_hi
    t_lo, n_ge_lo = bisect16(lo_scr, want_lo)
    thr = t_hi * 2 ** 16 + t_lo + 2 ** 15

    def plain_step(j, diag):
        sel = keys_scr[j] >= thr
        if diag is not None:
            sel = sel & _visible(TK, tq, False, diag * TK)
        bias_scr[j] = jnp.where(sel, 0.0, NEG_BIG)

    def tie_step(j, run, diag, need):
        kblk = keys_scr[j]
        eq = kblk == thr
        eqf = jnp.where(eq, 1.0, 0.0)
        before = _dot(tri_ref[...], eqf.astype(BF16)) + run
        sel = (kblk > thr) | (eq & (before < need))
        if diag is not None:
            sel = sel & _visible(TK, tq, False, diag * TK)
        bias_scr[j] = jnp.where(sel, 0.0, NEG_BIG)
        return run + jnp.sum(eqf, axis=0, keepdims=True)

    def plain_bias():
        def body(j, c):
            plain_step(j, None)
            return c
        lax.fori_loop(0, n_vis - qb, body, 0)
        for d in range(qb):
            plain_step(n_vis - qb + d, d)

    def tie_bias():
        need = want_lo - count16_gt(lo_scr, t_lo)
        run = lax.fori_loop(0, n_vis - qb, lambda j, r: tie_step(j, r, None, need), jnp.zeros((1, tq), F32))
        for d in range(qb):
            run = tie_step(n_vis - qb + d, run, d, need)

    lax.cond(jnp.max(n_ge_lo - want_lo) > 0.0, tie_bias, plain_bias)

    q = q_ref[0]
    q_heads = _split_heads(q[:, 0:LANES]) + _split_heads(q[:, LANES:2 * LANES])
    acc_scr[...] = jnp.zeros_like(acc_scr)

    def attn_step(j0, count, ends, state):
        kb = _key_rows(k_ref, j0, count)
        scores = [_dot_nt(kb[:, (hh // 2) * LANES:(hh // 2 + 1) * LANES], q_heads[hh]) for hh in range(N_HEADS_DSA)]
        weights = []
        for b in range(count):
            bias = bias_scr[j0 + b]
            upd = [_softmax_update(scores[hh][b * TK:(b + 1) * TK, :] + bias, *state[hh])
                   for hh in range(N_HEADS_DSA)]
            state = tuple((u[2], u[3]) for u in upd)
            weights.append([(u[0], u[1]) for u in upd])
        accs = [acc_scr[hh // 2, _head_rows(hh % 2), :] for hh in range(N_HEADS_DSA)]
        for b in range(count):
            for hh in range(N_HEADS_DSA):
                p, alpha = weights[b][hh]
                accs[hh] = alpha * accs[hh] + _dot(vt_ref[0, hh // 2, j0 + b, _head_rows(hh % 2), :],
                                                   p.astype(BF16))
        for hh in range(N_HEADS_DSA):
            acc_scr[hh // 2, _head_rows(hh % 2), :] = accs[hh]
        return state

    init = tuple((jnp.full((1, tq), -jnp.inf, F32), jnp.zeros((1, tq), F32)) for _ in range(N_HEADS_DSA))
    state = _walk_blocks(n_vis, attn_step, init, always_even=True, group=4)
    outs = []
    for pr in range(n_pairs):
        inv = jnp.concatenate([jnp.broadcast_to(1.0 / state[2 * pr + hd][1], (HEAD_DIM, tq)) for hd in range(2)],
                              axis=0)
        outs.append((acc_scr[pr] * inv).T)
    o_ref[0] = jnp.concatenate(outs, axis=1).astype(BF16)


def _dsa(zb, vt, misct, tri_lt, topk):
    b, s, _ = zb.shape
    nq, nk = s // TQ_WIDE, s // TK
    n_pairs = W_DSA // LANES
    return pl.pallas_call(
        functools.partial(_dsa_body, topk=topk),
        grid=(b, nq),
        in_specs=[
            pl.BlockSpec((1, TQ_WIDE, W_IDX), lambda bi, i: (bi, i, Z_IQ // W_IDX)),
            pl.BlockSpec((1, s, LANES), lambda bi, i: (bi, 0, Z_IK // LANES)),
            pl.BlockSpec((1, TQ_WIDE, W_DSA), lambda bi, i: (bi, i, Z_CQ // W_DSA)),
            pl.BlockSpec((1, s, W_DSA), lambda bi, i: (bi, 0, Z_CK // W_DSA)),
            pl.BlockSpec((1, n_pairs, nk, LANES, TK), lambda bi, i: (bi, 0, 0, 0, 0)),
            pl.BlockSpec((1, 1, MISC_ROWS, TQ_WIDE), lambda bi, i: (bi, i, 0, 0)),
            pl.BlockSpec((TK, TK), lambda bi, i: (0, 0)),
        ],
        out_specs=pl.BlockSpec((1, TQ_WIDE, W_DSA), lambda bi, i: (bi, i, 0)),
        out_shape=jax.ShapeDtypeStruct((b, s, W_DSA), BF16),
        scratch_shapes=[
            pltpu.VMEM((nk, TK, TQ_WIDE), jnp.int32),
            pltpu.VMEM((nk, TK, TQ_WIDE), jnp.int16),
            pltpu.VMEM((nk, TK, TQ_WIDE), jnp.int16),
            pltpu.VMEM((nk, TK, TQ_WIDE), F32),
            pltpu.VMEM((n_pairs, LANES, TQ_WIDE), F32),
        ],
        compiler_params=pltpu.CompilerParams(
            dimension_semantics=("parallel", "arbitrary"), vmem_limit_bytes=VMEM_LIMIT),
        name="dsa",
    )(zb, zb, zb, zb, vt, misct, tri_lt)


def _merge_body(x_ref, g_ref, of_ref, os_ref, oc_ref, wg_ref, bg_ref, wf_ref, ws_ref, wc_ref, wo_ref, o_ref):
    x = x_ref[...]
    d = x.shape[1]
    h = _rms_rows(x, g_ref[...]).astype(BF16)
    merged = jnp.zeros_like(x)
    for n, (o_br, w_br) in enumerate(((of_ref, wf_ref), (os_ref, ws_ref), (oc_ref, wc_ref))):
        gate = _dot(h, wg_ref[:, n * d:(n + 1) * d]) + bg_ref[n:n + 1, :]
        gate = 1.0 / (1.0 + jnp.exp(-gate))
        merged = merged + gate * _dot(o_br[...], w_br[...])
    o_ref[...] = x + _dot(merged.astype(BF16), wo_ref[...])


def _merge(x2, g, o_f, o_s, o_c, w_gates, b_gates, w_f, w_s, w_c, w_o):
    m, d = x2.shape
    tm = min(TM_MERGE, m)
    row = lambda w: pl.BlockSpec((tm, w), lambda i: (i, 0))
    const = lambda a: pl.BlockSpec(a.shape, lambda i: (0, 0))
    return pl.pallas_call(
        _merge_body,
        grid=(m // tm,),
        in_specs=[row(d), const(g), row(W_FOX), row(W_SB), row(W_DSA), const(w_gates), const(b_gates),
                  const(w_f), const(w_s), const(w_c), const(w_o)],
        out_specs=row(d),
        out_shape=jax.ShapeDtypeStruct((m, d), F32),
        compiler_params=pltpu.CompilerParams(
            dimension_semantics=("parallel",), vmem_limit_bytes=VMEM_LIMIT),
        name="merge",
    )(x2, g, o_f, o_s, o_c, w_gates, b_gates, w_f, w_s, w_c, w_o)


def _tile_heads(g, n):
    return jnp.tile(g.astype(F32), n)


def _constants(tm_proj):
    r = np.arange(W_FOX)
    bd = (r[:, None] // HEAD_DIM == r[None, :] // HEAD_DIM).astype(np.float32)
    rt = np.arange(tm_proj)
    tri_tok = (rt[None, :] <= rt[:, None]).astype(np.float32)
    rk = np.arange(TK)
    tri_ge = (rk[None, :] >= rk[:, None]).astype(np.float32)
    tri_lt = (rk[None, :] < rk[:, None]).astype(np.float32)

    n_src = W_FOX + N_BIAS_FEATS * LANES
    pq = np.zeros((n_src, W_FOX_AUG), np.float32)
    pk = np.zeros((n_src, W_FOX_AUG), np.float32)
    ones_q = np.zeros((1, W_FOX_AUG), np.float32)
    ones_k = np.zeros((1, W_FOX_AUG), np.float32)
    for hd in range(N_HEADS_FOX):
        for dd in range(HEAD_DIM):
            pq[hd * HEAD_DIM + dd, hd * LANES + dd] = 1.0
            pk[hd * HEAD_DIM + dd, hd * LANES + dd] = 1.0
        for part in range(N_BIAS_FEATS):
            src_row = W_FOX + part * LANES + L_FORGET + hd
            pq[src_row, hd * LANES + HEAD_DIM + part] = 1.0
            ones_q[0, hd * LANES + HEAD_DIM + N_BIAS_FEATS + part] = 1.0
            ones_k[0, hd * LANES + HEAD_DIM + part] = 1.0
            pk[src_row, hd * LANES + HEAD_DIM + N_BIAS_FEATS + part] = -1.0
    half = ROT_DIM // 2
    inv_freq = jnp.power(ROPE_THETA, -jnp.arange(half, dtype=F32) * 2.0 / ROT_DIM)
    dlane = np.arange(LANES) % HEAD_DIM
    freq = jnp.where(jnp.asarray(dlane < ROT_DIM), inv_freq[jnp.asarray(dlane % half)], 0.0)
    return dict(bd=jnp.asarray(bd, BF16), tri_tok=jnp.asarray(tri_tok, BF16), tri_ge=jnp.asarray(tri_ge, BF16),
                tri_lt=jnp.asarray(tri_lt, BF16), pq=jnp.asarray(pq, BF16), pk=jnp.asarray(pk, BF16),
                ones_q=jnp.asarray(ones_q, F32), ones_k=jnp.asarray(ones_k, F32),
                freq=freq.reshape(1, LANES).astype(F32))


def kernel(x, positions, ffn1_norm, ffn1_w_gate, ffn1_w_up, ffn1_w_down, mix_norm, w_in, b_forget, b_gates, q_norm_fox, k_norm_fox, q_norm_sb, k_norm_sb, q_norm_dsa, k_norm_dsa, w_branch_fox, w_branch_sb, w_branch_dsa, w_out, ffn2_norm, ffn2_w_gate, ffn2_w_up, ffn2_w_down):
    b, s, d = x.shape
    depth = w_in.shape[0]
    topk = min(TOPK_MAX, s // 4)
    assert s % TQ_WIDE == 0 and TQ == TK and TQ_WIDE % (2 * TK) == 0 and d % LANES == 0
    consts = _constants(min(TM_PROJ, s))
    tm_proj = min(TM_PROJ, s)
    rope_cos, rope_sin = _rope_tables(positions.reshape(b, s // tm_proj, 1, tm_proj), consts["freq"])
    scale = HEAD_DIM ** -0.5
    idx_scale = IDX_DIM ** -0.5

    splits = (W_FOX, W_FOX, W_FOX, N_HEADS_FOX, W_SB, W_SB, W_SB, W_DSA, W_DSA, W_DSA,
              W_IDX, IDX_DIM, N_IDX_HEADS, d, d, d)
    offs = np.concatenate([[0], np.cumsum(splits)]).tolist()
    (o_qf, o_kf, o_vf, o_ff, o_qs, o_ks, o_vs, o_qc, o_kc, o_vc, o_qi, o_ki, o_wi, o_ga) = offs[:14]

    x2 = x.reshape(b * s, d)
    for l in range(depth):
        wl = w_in[l].astype(BF16)
        pad = jnp.zeros((d, LANES - IDX_DIM - N_HEADS_FOX - N_IDX_HEADS), BF16)
        w_proj = jnp.concatenate([
            wl[:, o_qf:o_ff], wl[:, o_qs:o_qi], wl[:, o_qi:o_ki],
            wl[:, o_ki:o_wi], wl[:, o_ff:o_qs], wl[:, o_wi:o_ga], pad], axis=1)
        w_gates = wl[:, o_ga:]
        ones = lambda n: jnp.ones((n,), F32)
        gains = jnp.concatenate([
            _tile_heads(q_norm_fox[l], N_HEADS_FOX) * scale, _tile_heads(k_norm_fox[l], N_HEADS_FOX), ones(W_FOX),
            _tile_heads(q_norm_sb[l], N_HEADS_SB) * scale, _tile_heads(k_norm_sb[l], N_HEADS_SB), ones(W_SB),
            _tile_heads(q_norm_dsa[l], N_HEADS_DSA) * scale, _tile_heads(k_norm_dsa[l], N_HEADS_DSA), ones(W_DSA),
            ones(W_IDX) * idx_scale, ones(LANES)]).reshape(1, N_PROJ)
        fbias = jnp.zeros((LANES,), F32).at[L_FORGET:L_FORGET + N_HEADS_FOX].set(b_forget[l].astype(F32))
        fbias = fbias.reshape(1, LANES)

        x2 = _ffn(x2, ffn1_norm[l].reshape(1, d), ffn1_w_gate[l].astype(BF16), ffn1_w_up[l].astype(BF16),
                  ffn1_w_down[l].astype(BF16))
        zb, vt_f, vt_s, vt_c, misct = _proj(x2.reshape(b, s, d), rope_cos, rope_sin, mix_norm[l].reshape(1, d),
                                            w_proj, gains, consts, fbias)
        o_f = _fox(zb, vt_f)
        o_s = _sb(zb, vt_s, consts["tri_ge"])
        o_c = _dsa(zb, vt_c, misct, consts["tri_lt"], topk)
        x2 = _merge(x2, mix_norm[l].reshape(1, d), o_f.reshape(b * s, W_FOX), o_s.reshape(b * s, W_SB),
                    o_c.reshape(b * s, W_DSA), w_gates, b_gates[l].astype(F32),
                    w_branch_fox[l].astype(BF16), w_branch_sb[l].astype(BF16), w_branch_dsa[l].astype(BF16),
                    w_out[l].astype(BF16))
        x2 = _ffn(x2, ffn2_norm[l].reshape(1, d), ffn2_w_gate[l].astype(BF16), ffn2_w_up[l].astype(BF16),
                  ffn2_w_down[l].astype(BF16))
    return x2.reshape(b, s, d)
```
